```python
import math
import jax
import jax.numpy as jnp
from jax import lax
import numpy as np

D_MODEL = 2048
BATCH = 2
SEQ = 4096
DEPTH = 2
DEC_BATCH = 128
DEC_SEQ = 1
PAST_LEN = 2048
PAGE_SIZE = 128

HEAD_DIM = 128
N_MIXERS = 4
D_MIX = D_MODEL // N_MIXERS
H_MIX = D_MIX // HEAD_DIM
D_FF = 4 * D_MODEL
CONV_W = 4
GDN_CHUNK = 64
MLSTM_CHUNK = 64
MOBA_BLOCK = 256
MOBA_TOPK = 3
MOBA_QBLOCK = 64
GM_CHUNK = 128
GM_GROUPS = 4
GM_GROUP_W = D_MIX // GM_GROUPS
ROPE_THETA = 500000.0
ROT_DIM = HEAD_DIM // 4
ALPHA = (2.0 * DEPTH) ** 0.25
BETA_INIT = (8.0 * DEPTH) ** -0.25
LN_EPS = 1e-5
IN_SIZES = (3 * D_MIX, D_MIX, H_MIX, H_MIX,
            D_MIX, D_MIX, D_MIX, D_MIX, H_MIX, H_MIX,
            3 * D_MIX,
            D_MIX, D_MIX)
D_IN = sum(IN_SIZES)
IN_SPLIT_POINTS = tuple(int(s) for s in np.cumsum(IN_SIZES)[:-1])

kernel_name = 'hymba_gdn_mlstm_moba_gmlp_step'


def _standardize(x):
    mu = x.mean(-1, keepdims=True)
    var = jnp.square(x - mu).mean(-1, keepdims=True)
    return (x - mu) * lax.rsqrt(var + LN_EPS)


def layer_norm(x, w, b):
    f32 = jnp.float32
    return (_standardize(x.astype(f32)) * w.astype(f32) + b.astype(f32)).astype(x.dtype)


def rms_norm(x, w):
    return x * lax.rsqrt(jnp.mean(jnp.square(x), -1, keepdims=True) + LN_EPS) * w


def l2_normalize(x):
    return x * lax.rsqrt(jnp.sum(jnp.square(x), -1, keepdims=True) + 1e-6)


def causal_conv_silu(x, buf, w):
    seqlen = x.shape[1]
    xp = jnp.concatenate([buf, x], axis=1)
    y = sum(w[j] * xp[:, j:j + seqlen] for j in range(CONV_W))
    return jax.nn.silu(y), xp[:, seqlen:]


def rope_partial(x, pos):
    half = ROT_DIM // 2
    inv_freq = ROPE_THETA ** (-jnp.arange(half, dtype=jnp.float32) * (2.0 / ROT_DIM))
    ang = pos.astype(jnp.float32)[:, None] * inv_freq[None, :]
    cos = jnp.cos(ang)[None, :, None, :]
    sin = jnp.sin(ang)[None, :, None, :]
    x1, x2, rest = x[..., :half], x[..., half:ROT_DIM], x[..., ROT_DIM:]
    return jnp.concatenate([x1 * cos - x2 * sin, x1 * sin + x2 * cos, rest], axis=-1)


def gdn_chunked(q, k, v, beta, g, s0):
    bsz, seqlen, nh, _ = q.shape
    dv = v.shape[-1]
    cl = math.gcd(seqlen, GDN_CHUNK)
    nc = seqlen // cl

    def to_chunks(t):
        return jnp.moveaxis(t.reshape(bsz, nc, cl, nh, *t.shape[3:]), 3, 1)

    q, k, v, beta, g = (to_chunks(t) for t in (q, k, v, beta, g))
    gc = jnp.cumsum(g, axis=-1)
    tri = jnp.tril(jnp.ones((cl, cl), bool))
    tri_s = jnp.tril(jnp.ones((cl, cl), bool), -1)
    decay = jnp.exp(jnp.where(tri, gc[..., :, None] - gc[..., None, :], -jnp.inf))
    kb = k * beta[..., None]
    vb = v * beta[..., None]
    a = jnp.where(tri_s, jnp.einsum('bhnid,bhnjd->bhnij', kb, k) * decay, 0.0)
    eye = jnp.eye(cl, dtype=a.dtype)
    t_inv = lax.linalg.triangular_solve(a + eye, jnp.broadcast_to(eye, a.shape),
                                        left_side=True, lower=True, unit_diagonal=True)
    u = t_inv @ vb
    w = t_inv @ (kb * jnp.exp(gc)[..., None])
    attn = jnp.where(tri, jnp.einsum('bhnid,bhnjd->bhnij', q, k) * decay, 0.0)
    q_dec = q * jnp.exp(gc)[..., None]
    k_tail = k * jnp.exp(gc[..., -1:] - gc)[..., None]
    g_tot = jnp.exp(gc[..., -1])
    xs = tuple(jnp.moveaxis(t, 2, 0) for t in (u, w, attn, q_dec, k_tail, g_tot))

    def step(s, inp):
        u_c, w_c, attn_c, qd_c, kt_c, gt_c = inp
        v_new = u_c - w_c @ s
        o = qd_c @ s + attn_c @ v_new
        s = s * gt_c[..., None, None] + jnp.einsum('bhck,bhcv->bhkv', kt_c, v_new)
        return s, o

    s_fin, o = lax.scan(step, s0, xs)
    o = o.transpose(1, 0, 3, 2, 4).reshape(bsz, seqlen, nh, dv)
    return o, s_fin


def mlstm_chunked(q, k, v, log_i, log_f, c0, n0, m0):
    bsz, seqlen, nh, hd = q.shape
    cl = math.gcd(seqlen, MLSTM_CHUNK)
    nc = seqlen // cl

    def to_chunks(t):
        t = t.reshape(bsz, nc, cl, nh, *t.shape[3:])
        return jnp.moveaxis(jnp.moveaxis(t, 3, 1), 2, 0)

    tri = jnp.tril(jnp.ones((cl, cl), bool))

    def step(carry, inp):
        c_prev, n_prev, m_prev = carry
        qc, kc, vc, ic, fc = inp
        b = jnp.cumsum(fc, axis=-1)
        dlog = jnp.where(tri, b[..., :, None] - b[..., None, :] + ic[..., None, :], -jnp.inf)
        inter = b + m_prev[..., None]
        m = jnp.maximum(dlog.max(-1), inter)
        s = jnp.einsum('bhtd,bhsd->bhts', qc, kc) * jnp.exp(dlog - m[..., None])
        scale_prev = jnp.exp(inter - m)
        num = s @ vc + scale_prev[..., None] * jnp.einsum('bhtk,bhkv->bhtv', qc, c_prev)
        den = s.sum(-1) + scale_prev * jnp.einsum('bhtk,bhk->bht', qc, n_prev)
        h = num / jnp.maximum(jnp.abs(den), jnp.exp(-m))[..., None]
        b_end = b[..., -1]
        wlog = b_end[..., None] - b + ic
        m_new = jnp.maximum(b_end + m_prev, wlog.max(-1))
        wk = jnp.exp(wlog - m_new[..., None])[..., None] * kc
        dec = jnp.exp(b_end + m_prev - m_new)
        c_new = dec[..., None, None] * c_prev + jnp.einsum('bhsk,bhsv->bhkv', wk, vc)
        n_new = dec[..., None] * n_prev + wk.sum(-2)
        return (c_new, n_new, m_new), h

    xs = tuple(to_chunks(t) for t in (q, k, v, log_i, log_f))
    (c_fin, n_fin, m_fin), h = lax.scan(step, (c0, n0, m0), xs)
    h = h.transpose(1, 0, 3, 2, 4).reshape(bsz, seqlen, nh, hd)
    return h, c_fin, n_fin, m_fin


def moba_attend(q, k, v, q_pos):
    bsz, nq_tot, nh, hd = q.shape
    t = k.shape[1]
    nb = -(-t // MOBA_BLOCK)
    pad = nb * MOBA_BLOCK - t
    k = jnp.pad(k, ((0, 0), (0, pad), (0, 0), (0, 0)))
    v = jnp.pad(v, ((0, 0), (0, pad), (0, 0), (0, 0)))
    kb = jnp.moveaxis(k.reshape(bsz, nb, MOBA_BLOCK, nh, hd), 3, 1)
    vb = jnp.moveaxis(v.reshape(bsz, nb, MOBA_BLOCK, nh, hd), 3, 1)
    kmean = kb.mean(axis=3)
    n_sel = min(MOBA_TOPK, nb)
    qbs = math.gcd(nq_tot, MOBA_QBLOCK)
    nqb = nq_tot // qbs
    qs = q.reshape(bsz, nqb, qbs, nh, hd).transpose(1, 0, 3, 2, 4)
    ps = q_pos.reshape(nqb, qbs)
    bi = jnp.arange(bsz)[:, None, None, None]
    hi = jnp.arange(nh)[None, :, None, None]
    offs = jnp.arange(MOBA_BLOCK)
    scale = hd ** -0.5

    def attend_block(args):
        qc, pc = args
        own = pc // MOBA_BLOCK
        gate = jnp.einsum('bhqd,bhnd->bhqn', qc, kmean)
        past = jnp.arange(nb)[None, :] < own[:, None]
        gate = jnp.where(past, gate, -jnp.inf)
        gsel, sel = lax.top_k(gate, n_sel)
        own_b = jnp.broadcast_to(own[:, None], sel.shape[:-1] + (1,)).astype(sel.dtype)
        blocks = jnp.concatenate([sel, own_b], axis=-1)
        slot_ok = jnp.concatenate([gsel > -jnp.inf, jnp.ones(own_b.shape, bool)], axis=-1)
        kg = kb[bi, hi, blocks]
        vg = vb[bi, hi, blocks]
        s = jnp.einsum('bhqd,bhqnkd->bhqnk', qc, kg) * scale
        key_pos = blocks[..., None] * MOBA_BLOCK + offs
        mask = slot_ok[..., None] & (key_pos <= pc[:, None, None])
        s = jnp.where(mask, s, -jnp.inf)
        p = jax.nn.softmax(s.reshape(*s.shape[:3], -1), axis=-1).reshape(s.shape)
        return jnp.einsum('bhqnk,bhqnkd->bhqd', p, vg)

    out = lax.map(attend_block, (qs, ps))
    return out.transpose(1, 0, 3, 2, 4).reshape(bsz, nq_tot, nh, hd)


def gmlp_spatial(u, v, ws, b):
    bsz, seqlen, _ = u.shape
    lc = min(GM_CHUNK, seqlen)
    nc = seqlen // lc
    wm = jnp.tril(ws[:, :lc, :lc])
    vr = v.reshape(bsz, nc, lc, GM_GROUPS, GM_GROUP_W)
    z = jnp.einsum('gts,bnsgc->bntgc', wm, vr) + b[:, :lc].T[None, None, :, :, None]
    return u * z.reshape(bsz, seqlen, D_MIX)


def mixer_block(h, pos, conv_buf, gdn_s0, ml_c0, ml_n0, ml_m0, k_past, v_past,
                w_in, conv_w, a_log, dt_bias, gdn_nw, ml_gb, gm_nw, gm_ws, gm_b, w_out):
    f32 = jnp.float32
    dt = h.dtype
    bsz, seqlen, _ = h.shape
    proj = jnp.einsum('bld,de->ble', h, w_in).astype(f32)
    (a_qkv, a_z, a_beta, a_dec, b_q, b_k, b_v, b_o, b_i, b_f,
     c_qkv, d_u, d_v) = jnp.split(proj, IN_SPLIT_POINTS, axis=-1)

    def heads(t):
        return t.reshape(bsz, seqlen, -1, HEAD_DIM)

    a_qkv, conv_new = causal_conv_silu(a_qkv, conv_buf.astype(f32), conv_w.astype(f32))
    qa, ka, va = (heads(t) for t in jnp.split(a_qkv, 3, axis=-1))
    qa = l2_normalize(qa) * HEAD_DIM ** -0.5
    ka = l2_normalize(ka)
    beta = jax.nn.sigmoid(a_beta)
    g = -jnp.exp(a_log.astype(f32)) * jax.nn.softplus(a_dec + dt_bias.astype(f32))
    oa, gdn_s = gdn_chunked(qa, ka, va, beta, g, gdn_s0.astype(f32))
    oa = (rms_norm(oa, gdn_nw.astype(f32)) * jax.nn.silu(heads(a_z))).reshape(bsz, seqlen, D_MIX)

    gb = ml_gb.astype(f32)
    log_i = b_i + gb[:H_MIX]
    log_f = jax.nn.log_sigmoid(b_f + gb[H_MIX:])
    hb, ml_c, ml_n, ml_m = mlstm_chunked(heads(b_q), heads(b_k) * HEAD_DIM ** -0.5, heads(b_v),
                                         log_i, log_f, ml_c0.astype(f32), ml_n0.astype(f32),
                                         ml_m0.astype(f32))
    ob = jax.nn.sigmoid(b_o) * hb.reshape(bsz, seqlen, D_MIX)

    qc, kc, vc = (heads(t) for t in jnp.split(c_qkv, 3, axis=-1))
    qc = rope_partial(qc, pos)
    kc = rope_partial(kc, pos)
    if k_past is None:
        k_all, v_all = kc, vc
    else:
        k_all = jnp.concatenate([k_past.astype(f32), kc], axis=1)
        v_all = jnp.concatenate([v_past.astype(f32), vc], axis=1)
    oc = moba_attend(qc, k_all, v_all, pos).reshape(bsz, seqlen, D_MIX)

    gu = jax.nn.gelu(d_u)
    gv = _standardize(jax.nn.gelu(d_v)) * gm_nw.astype(f32)
    od = gmlp_spatial(gu, gv, gm_ws.astype(f32), gm_b.astype(f32))

    mix = jnp.concatenate([oa, ob, oc, od], axis=-1).astype(dt)
    out = jnp.einsum('ble,ed->bld', mix, w_out)
    states = (conv_new.astype(dt), gdn_s.astype(dt), ml_c.astype(dt), ml_n.astype(dt),
              ml_m.astype(dt), kc.astype(dt), vc.astype(dt), gv.astype(dt))
    return out, states


def trunk_layer(x, pos, conv_buf, gdn_s0, ml_c0, ml_n0, ml_m0, k_past, v_past, params):
    (w_in, conv_w, a_log, dt_bias, gdn_nw, ml_gb, gm_nw, gm_ws, gm_b, w_out,
     ln1_w, ln1_b, w_up, w_down, ln2_w, ln2_b) = params
    mix, states = mixer_block(x, pos, conv_buf, gdn_s0, ml_c0, ml_n0, ml_m0, k_past, v_past,
                              w_in, conv_w, a_log, dt_bias, gdn_nw, ml_gb, gm_nw, gm_ws, gm_b, w_out)
    x = layer_norm(ALPHA * x + mix, ln1_w, ln1_b)
    ff = jnp.square(jax.nn.relu(x @ w_up)) @ w_down
    x = layer_norm(ALPHA * x + ff, ln2_w, ln2_b)
    return x, states


def setup_inputs(seed: int = 0) -> dict:
    key = jax.random.key(seed)
    ks = iter(jax.random.split(key, 40))
    f32 = jnp.float32
    n_pages = PAST_LEN // PAGE_SIZE
    n_pool = (DEC_BATCH * n_pages * 5) // 4

    def nrm(shape, scale):
        return scale * jax.random.normal(next(ks), shape, f32)

    x_prompt = nrm((BATCH, SEQ, D_MODEL), 1.0)
    x_sample = nrm((DEC_BATCH, DEC_SEQ, D_MODEL), 1.0)
    state_gdn_conv = nrm((DEPTH, DEC_BATCH, CONV_W - 1, 3 * D_MIX), 1.0)
    state_gdn_s = nrm((DEPTH, DEC_BATCH, H_MIX, HEAD_DIM, HEAD_DIM), 0.1)
    state_mlstm_c = nrm((DEPTH, DEC_BATCH, H_MIX, HEAD_DIM, HEAD_DIM), 0.1)
    state_mlstm_n = nrm((DEPTH, DEC_BATCH, H_MIX, HEAD_DIM), 0.1)
    state_mlstm_m = nrm((DEPTH, DEC_BATCH, H_MIX), 1.0)
    cache_k = nrm((DEPTH, n_pool, PAGE_SIZE, H_MIX, HEAD_DIM), 1.0)
    cache_v = nrm((DEPTH, n_pool, PAGE_SIZE, H_MIX, HEAD_DIM), 1.0)
    page_table = jax.random.permutation(next(ks), n_pool)[:DEC_BATCH * n_pages]
    page_table = page_table.reshape(DEC_BATCH, n_pages).astype(jnp.int32)

    w_in = nrm((DEPTH, D_MODEL, D_IN), D_MODEL ** -0.5)
    gdn_conv_w = nrm((DEPTH, CONV_W, 3 * D_MIX), CONV_W ** -0.5)
    gdn_a_log = jnp.log(jax.random.uniform(next(ks), (DEPTH, H_MIX), f32, 1.0, 16.0))
    dt0 = jnp.exp(jax.random.uniform(next(ks), (DEPTH, H_MIX), f32, math.log(1e-3), math.log(1e-1)))
    gdn_dt_bias = dt0 + jnp.log(-jnp.expm1(-dt0))
    gdn_norm_w = 1.0 + nrm((DEPTH, HEAD_DIM), 0.02)
    mlstm_gate_b = jnp.concatenate([nrm((DEPTH, H_MIX), 0.1), 3.0 + nrm((DEPTH, H_MIX), 0.5)], axis=-1)
    gmlp_norm_w = 1.0 + nrm((DEPTH, D_MIX), 0.02)
    gmlp_ws = nrm((DEPTH, GM_GROUPS, GM_CHUNK, GM_CHUNK), 0.5 * GM_CHUNK ** -0.5)
    gmlp_b = 1.0 + nrm((DEPTH, GM_GROUPS, GM_CHUNK), 0.02)
    w_out = nrm((DEPTH, D_MODEL, D_MODEL), BETA_INIT * D_MODEL ** -0.5)
    ln1_w = 1.0 + nrm((DEPTH, D_MODEL), 0.02)
    ln1_b = nrm((DEPTH, D_MODEL), 0.02)
    w_up = nrm((DEPTH, D_MODEL, D_FF), D_MODEL ** -0.5)
    w_down = nrm((DEPTH, D_FF, D_MODEL), BETA_INIT * D_FF ** -0.5)
    ln2_w = 1.0 + nrm((DEPTH, D_MODEL), 0.02)
    ln2_b = nrm((DEPTH, D_MODEL), 0.02)
    return {'x_prompt': x_prompt, 'x_sample': x_sample,
            'state_gdn_conv': state_gdn_conv, 'state_gdn_s': state_gdn_s,
            'state_mlstm_c': state_mlstm_c, 'state_mlstm_n': state_mlstm_n,
            'state_mlstm_m': state_mlstm_m, 'cache_k': cache_k, 'cache_v': cache_v,
            'page_table': page_table, 'w_in': w_in, 'gdn_conv_w': gdn_conv_w,
            'gdn_a_log': gdn_a_log, 'gdn_dt_bias': gdn_dt_bias, 'gdn_norm_w': gdn_norm_w,
            'mlstm_gate_b': mlstm_gate_b, 'gmlp_norm_w': gmlp_norm_w, 'gmlp_ws': gmlp_ws,
            'gmlp_b': gmlp_b, 'w_out': w_out, 'ln1_w': ln1_w, 'ln1_b': ln1_b,
            'w_up': w_up, 'w_down': w_down, 'ln2_w': ln2_w, 'ln2_b': ln2_b}


def reference(x_prompt, x_sample, state_gdn_conv, state_gdn_s, state_mlstm_c, state_mlstm_n,
              state_mlstm_m, cache_k, cache_v, page_table, w_in, gdn_conv_w, gdn_a_log,
              gdn_dt_bias, gdn_norm_w, mlstm_gate_b, gmlp_norm_w, gmlp_ws, gmlp_b, w_out,
              ln1_w, ln1_b, w_up, w_down, ln2_w, ln2_b):
    f32 = jnp.float32
    bsz, seq = x_prompt.shape[0], x_prompt.shape[1]
    dec_b, dec_s = x_sample.shape[0], x_sample.shape[1]
    past_len = page_table.shape[1] * cache_k.shape[2]
    pos_p = jnp.arange(seq, dtype=jnp.int32)
    pos_s = past_len + jnp.arange(dec_s, dtype=jnp.int32)
    conv0 = jnp.zeros((bsz, CONV_W - 1, 3 * D_MIX), f32)
    mat0 = jnp.zeros((bsz, H_MIX, HEAD_DIM, HEAD_DIM), f32)
    vec0 = jnp.zeros((bsz, H_MIX, HEAD_DIM), f32)
    m0 = jnp.zeros((bsz, H_MIX), f32)
    yp, ys = x_prompt, x_sample
    p_st, s_st = [], []
    for l in range(DEPTH):
        params = (w_in[l], gdn_conv_w[l], gdn_a_log[l], gdn_dt_bias[l], gdn_norm_w[l],
                  mlstm_gate_b[l], gmlp_norm_w[l], gmlp_ws[l], gmlp_b[l], w_out[l],
                  ln1_w[l], ln1_b[l], w_up[l], w_down[l], ln2_w[l], ln2_b[l])
        yp, st = trunk_layer(yp, pos_p, conv0, mat0, mat0, vec0, m0, None, None, params)
        p_st.append(st)
        k_past = cache_k[l][page_table].reshape(dec_b, past_len, H_MIX, HEAD_DIM)
        v_past = cache_v[l][page_table].reshape(dec_b, past_len, H_MIX, HEAD_DIM)
        ys, st = trunk_layer(ys, pos_s, state_gdn_conv[l], state_gdn_s[l], state_mlstm_c[l],
                             state_mlstm_n[l], state_mlstm_m[l], k_past, v_past, params)
        s_st.append(st)

    def stk(sts, i):
        return jnp.stack([s[i] for s in sts], axis=0)

    return (yp, ys,
            stk(p_st, 0), stk(p_st, 1), stk(p_st, 2), stk(p_st, 3), stk(p_st, 4), stk(p_st, 5), stk(p_st, 6),
            stk(s_st, 0), stk(s_st, 1), stk(s_st, 2), stk(s_st, 3), stk(s_st, 4), stk(s_st, 5), stk(s_st, 6),
            stk(s_st, 7))
```

```python
import functools
import math

import jax
import jax.numpy as jnp
from jax import lax
from jax.experimental import pallas as pl
from jax.experimental.pallas import tpu as pltpu

F32 = jnp.float32
BF16 = jnp.bfloat16
HIGHEST = lax.Precision.HIGHEST

HEAD_DIM = 128
N_MIXERS = 4
CONV_W = 4
GDN_CHUNK = 64
MLSTM_CHUNK = 64
MOBA_BLOCK = 256
MOBA_TOPK = 3
GM_CHUNK = 128
GM_GROUPS = 4
ROPE_THETA = 500000.0
ROT_DIM = HEAD_DIM // 4
LN_EPS = 1e-5
NEG_BIG = -1e30

LANES = 128
SUBLANES = 8
VMEM_LIMIT_BYTES = 56 * 1024 * 1024


def _cparams(*sem):
    return pltpu.CompilerParams(dimension_semantics=sem, vmem_limit_bytes=VMEM_LIMIT_BYTES)


def _dot(a, b):
    return jnp.dot(a.astype(BF16), b.astype(BF16), preferred_element_type=F32)


def _dot_nt(a, b):
    return lax.dot_general(a.astype(BF16), b.astype(BF16), (((1,), (1,)), ((), ())),
                           preferred_element_type=F32)


def _dot_tn(a, b):
    return lax.dot_general(a.astype(BF16), b.astype(BF16), (((0,), (0,)), ((), ())),
                           preferred_element_type=F32)


def _dot_hi(a, b):
    return jnp.dot(a, b, precision=HIGHEST, preferred_element_type=F32)


def _dot_nt_hi(a, b):
    return lax.dot_general(a, b, (((1,), (1,)), ((), ())), precision=HIGHEST,
                           preferred_element_type=F32)


def _dot_tn_hi(a, b):
    return lax.dot_general(a, b, (((0,), (0,)), ((), ())), precision=HIGHEST,
                           preferred_element_type=F32)


def _sigmoid(x):
    return 1.0 / (1.0 + jnp.exp(-x))


def _silu(x):
    return x * _sigmoid(x)


def _softplus(x):
    return jnp.maximum(x, 0.0) + jnp.log(1.0 + jnp.exp(-jnp.abs(x)))


def _log_sigmoid(x):
    return -_softplus(-x)


def _gelu_tanh(x):
    return 0.5 * x * (1.0 + jnp.tanh(math.sqrt(2.0 / math.pi) * (x + 0.044715 * (x * x * x))))


def _layer_norm(x, w, b):
    mu = jnp.mean(x, axis=-1, keepdims=True)
    xc = x - mu
    var = jnp.mean(xc * xc, axis=-1, keepdims=True)
    return xc * lax.rsqrt(var + LN_EPS) * w + b


def _proj_kernel(x_ref, w_ref, wg_ref, o_ref, g_ref):
    x = x_ref[...]
    o_ref[...] = jnp.dot(x, w_ref[...], preferred_element_type=F32)

    @pl.when(pl.program_id(1) == 0)
    def _():
        g_ref[...] = jnp.dot(x, wg_ref[...], preferred_element_type=F32)


def _proj(x_bf, w_main, w_gate, *, tm, tn):
    m, d = x_bf.shape
    n = w_main.shape[1]
    return pl.pallas_call(
        _proj_kernel,
        grid=(m // tm, n // tn),
        in_specs=[pl.BlockSpec((tm, d), lambda i, j: (i, 0)),
                  pl.BlockSpec((d, tn), lambda i, j: (0, j)),
                  pl.BlockSpec((d, LANES), lambda i, j: (0, 0))],
        out_specs=[pl.BlockSpec((tm, tn), lambda i, j: (i, j)),
                   pl.BlockSpec((tm, LANES), lambda i, j: (i, 0))],
        out_shape=[jax.ShapeDtypeStruct((m, n), F32), jax.ShapeDtypeStruct((m, LANES), F32)],
        compiler_params=_cparams("parallel", "arbitrary"),
        name="proj",
    )(x_bf, w_main, w_gate)


def _outproj_ln_kernel(alpha, mix_ref, w_ref, x_ref, lw_ref, lb_ref, o_ref, ob_ref):
    y = alpha * x_ref[...] + jnp.dot(mix_ref[...], w_ref[...], preferred_element_type=F32)
    y = _layer_norm(y, lw_ref[...], lb_ref[...])
    o_ref[...] = y
    ob_ref[...] = y.astype(BF16)


def _outproj_ln(mix_bf, w_out_bf, x, ln_w, ln_b, *, alpha, tm):
    m, d = x.shape
    return pl.pallas_call(
        functools.partial(_outproj_ln_kernel, alpha),
        grid=(m // tm,),
        in_specs=[pl.BlockSpec((tm, d), lambda i: (i, 0)),
                  pl.BlockSpec((d, d), lambda i: (0, 0)),
                  pl.BlockSpec((tm, d), lambda i: (i, 0)),
                  pl.BlockSpec((1, d), lambda i: (0, 0)),
                  pl.BlockSpec((1, d), lambda i: (0, 0))],
        out_specs=[pl.BlockSpec((tm, d), lambda i: (i, 0)),
                   pl.BlockSpec((tm, d), lambda i: (i, 0))],
        out_shape=[jax.ShapeDtypeStruct((m, d), F32), jax.ShapeDtypeStruct((m, d), BF16)],
        compiler_params=_cparams("parallel"),
        name="outproj_ln",
    )(mix_bf, w_out_bf, x, ln_w, ln_b)


def _ffn_ln_kernel(alpha, xb_ref, wu_ref, wd_ref, x_ref, lw_ref, lb_ref, o_ref, ob_ref, acc_ref):
    j = pl.program_id(1)

    @pl.when(j == 0)
    def _():
        acc_ref[...] = alpha * x_ref[...]

    h = jnp.dot(xb_ref[...], wu_ref[...], preferred_element_type=F32)
    h = jnp.maximum(h, 0.0)
    h = (h * h).astype(BF16)
    acc_ref[...] += jnp.dot(h, wd_ref[...], preferred_element_type=F32)

    @pl.when(j == pl.num_programs(1) - 1)
    def _():
        y = _layer_norm(acc_ref[...], lw_ref[...], lb_ref[...])
        o_ref[...] = y
        ob_ref[...] = y.astype(BF16)


def _ffn_ln(x_bf, w_up_bf, w_down_bf, x, ln_w, ln_b, *, alpha, tm, tf):
    m, d = x.shape
    f = w_up_bf.shape[1]
    return pl.pallas_call(
        functools.partial(_ffn_ln_kernel, alpha),
        grid=(m // tm, f // tf),
        in_specs=[pl.BlockSpec((tm, d), lambda i, j: (i, 0)),
                  pl.BlockSpec((d, tf), lambda i, j: (0, j)),
                  pl.BlockSpec((tf, d), lambda i, j: (j, 0)),
                  pl.BlockSpec((tm, d), lambda i, j: (i, 0)),
                  pl.BlockSpec((1, d), lambda i, j: (0, 0)),
                  pl.BlockSpec((1, d), lambda i, j: (0, 0))],
        out_specs=[pl.BlockSpec((tm, d), lambda i, j: (i, 0)),
                   pl.BlockSpec((tm, d), lambda i, j: (i, 0))],
        out_shape=[jax.ShapeDtypeStruct((m, d), F32), jax.ShapeDtypeStruct((m, d), BF16)],
        scratch_shapes=[pltpu.VMEM((tm, d), F32)],
        compiler_params=_cparams("parallel", "arbitrary"),
        name="ffn_ln",
    )(x_bf, w_up_bf, w_down_bf, x, ln_w, ln_b)


D_MIX = 512
H_MIX = D_MIX // HEAD_DIM
COL_A_Q, COL_A_K, COL_A_V, COL_A_Z = 0, 512, 1024, 1536
COL_B_Q, COL_B_K, COL_B_V, COL_B_O = 2048, 2560, 3072, 3584
COL_C_Q, COL_C_K, COL_C_V = 4096, 4608, 5120
COL_D_U, COL_D_V = 5632, 6144
N_MAIN = 6656
GATE_A_BETA, GATE_A_DEC, GATE_B_I, GATE_B_F = 0, 4, 8, 12


def _tri_masks(n):
    r = lax.broadcasted_iota(jnp.int32, (n, n), 0)
    c = lax.broadcasted_iota(jnp.int32, (n, n), 1)
    return r >= c, r > c


def _unit_lower_inverse(a):
    n = a.shape[0]
    r = lax.broadcasted_iota(jnp.int32, (n, n), 0)
    c = lax.broadcasted_iota(jnp.int32, (n, n), 1)
    eye = jnp.where(r == c, 1.0, 0.0).astype(F32)
    t = eye - a
    p = a
    span = 2
    while span < n:
        p = _dot_hi(p, p)
        t = t + _dot_hi(t, p)
        span *= 2
    return t


def _gdn_prompt_kernel(tb, cl, x_ref, g_ref, cw_ref, alog_ref, dtb_ref, nw_ref,
                       o_ref, s_out_ref, xbuf, s_ref):
    t = pl.program_id(1)
    dqkv = 3 * D_MIX

    @pl.when(t == 0)
    def _():
        xbuf[0:SUBLANES, :] = jnp.zeros((SUBLANES, dqkv), F32)
        s_ref[...] = jnp.zeros_like(s_ref)

    @pl.when(t > 0)
    def _():
        xbuf[0:SUBLANES, :] = xbuf[tb:tb + SUBLANES, :]

    xbuf[SUBLANES:SUBLANES + tb, :] = x_ref[:, 0:dqkv]
    cw = cw_ref[...]
    y = cw[0:1] * xbuf[SUBLANES - 3:SUBLANES - 3 + tb, :]
    for j in range(1, CONV_W):
        y = y + cw[j:j + 1] * xbuf[SUBLANES - 3 + j:SUBLANES - 3 + j + tb, :]
    y = _silu(y)

    gates = g_ref[...]
    beta_all = _sigmoid(gates)
    g_all = -jnp.exp(alog_ref[...]) * _softplus(gates + dtb_ref[...])
    tri, tri_s = _tri_masks(cl)
    tril_f = jnp.where(tri, 1.0, 0.0).astype(F32)
    nw = nw_ref[...]
    states = [s_ref[h] for h in range(H_MIX)]

    for c in range(tb // cl):
        r0 = c * cl
        gcum = _dot_hi(tril_f, g_all[r0:r0 + cl, :])
        gcum_t = gcum.T
        for h in range(H_MIX):
            q = y[r0:r0 + cl, COL_A_Q + h * HEAD_DIM:COL_A_Q + (h + 1) * HEAD_DIM]
            k = y[r0:r0 + cl, COL_A_K + h * HEAD_DIM:COL_A_K + (h + 1) * HEAD_DIM]
            v = y[r0:r0 + cl, COL_A_V + h * HEAD_DIM:COL_A_V + (h + 1) * HEAD_DIM]
            z = x_ref[r0:r0 + cl, COL_A_Z + h * HEAD_DIM:COL_A_Z + (h + 1) * HEAD_DIM]
            q = q * lax.rsqrt(jnp.sum(q * q, axis=-1, keepdims=True) + 1e-6) * (HEAD_DIM ** -0.5)
            k = k * lax.rsqrt(jnp.sum(k * k, axis=-1, keepdims=True) + 1e-6)
            beta = beta_all[r0:r0 + cl, GATE_A_BETA + h:GATE_A_BETA + h + 1]
            gc = gcum[:, GATE_A_DEC + h:GATE_A_DEC + h + 1]
            gr = gcum_t[GATE_A_DEC + h:GATE_A_DEC + h + 1, :]
            decay = jnp.exp(jnp.where(tri, gc - gr, NEG_BIG))
            kb = k * beta
            vb = v * beta
            a = jnp.where(tri_s, _dot_nt_hi(kb, k) * decay, 0.0)
            t_inv = _unit_lower_inverse(a)
            egc = jnp.exp(gc)
            u = _dot_hi(t_inv, vb)
            w = _dot_hi(t_inv, kb * egc)
            attn = jnp.where(tri, _dot_nt_hi(q, k) * decay, 0.0)
            gc_last = gc[cl - 1:cl, :]
            s = states[h]
            v_new = u - _dot_hi(w, s)
            o = _dot_hi(q * egc, s) + _dot_hi(attn, v_new)
            states[h] = s * jnp.exp(gc_last) + _dot_tn_hi(k * jnp.exp(gc_last - gc), v_new)
            o = o * lax.rsqrt(jnp.mean(o * o, axis=-1, keepdims=True) + LN_EPS) * nw
            o_ref[r0:r0 + cl, h * HEAD_DIM:(h + 1) * HEAD_DIM] = (o * _silu(z)).astype(o_ref.dtype)

    for h in range(H_MIX):
        s_ref[h] = states[h]

    @pl.when(t == pl.num_programs(1) - 1)
    def _():
        s_out_ref[0] = s_ref[...]


def _gdn_prompt(proj, gates, conv_w, alog_row, dtb_row, nw_row, *, bsz, seqlen, tb):
    cl = math.gcd(seqlen, GDN_CHUNK)
    nt = seqlen // tb
    return pl.pallas_call(
        functools.partial(_gdn_prompt_kernel, tb, cl),
        grid=(bsz, nt),
        in_specs=[pl.BlockSpec((tb, 4 * D_MIX), lambda b, t: (b * nt + t, 0)),
                  pl.BlockSpec((tb, LANES), lambda b, t: (b * nt + t, 0)),
                  pl.BlockSpec((CONV_W, 3 * D_MIX), lambda b, t: (0, 0)),
                  pl.BlockSpec((1, LANES), lambda b, t: (0, 0)),
                  pl.BlockSpec((1, LANES), lambda b, t: (0, 0)),
                  pl.BlockSpec((1, HEAD_DIM), lambda b, t: (0, 0))],
        out_specs=[pl.BlockSpec((tb, D_MIX), lambda b, t: (b * nt + t, 0)),
                   pl.BlockSpec((1, H_MIX, HEAD_DIM, HEAD_DIM), lambda b, t: (b, 0, 0, 0))],
        out_shape=[jax.ShapeDtypeStruct((bsz * seqlen, D_MIX), BF16),
                   jax.ShapeDtypeStruct((bsz, H_MIX, HEAD_DIM, HEAD_DIM), F32)],
        scratch_shapes=[pltpu.VMEM((tb + SUBLANES, 3 * D_MIX), F32),
                        pltpu.VMEM((H_MIX, HEAD_DIM, HEAD_DIM), F32)],
        compiler_params=_cparams("parallel", "arbitrary"),
        name="gdn_prompt",
    )(proj, gates, conv_w, alog_row, dtb_row, nw_row)


def _mlstm_prompt_kernel(tb, cl, x_ref, g_ref, gbi_ref, gbf_ref,
                         o_ref, c_out_ref, n_out_ref, m_out_ref, c_ref, n_ref, m_ref):
    t = pl.program_id(1)

    @pl.when(t == 0)
    def _():
        c_ref[...] = jnp.zeros_like(c_ref)
        n_ref[...] = jnp.zeros_like(n_ref)
        m_ref[...] = jnp.zeros_like(m_ref)

    gates = g_ref[...]
    li_all = gates + gbi_ref[...]
    lf_all = _log_sigmoid(gates + gbf_ref[...])
    tri, _ = _tri_masks(cl)
    tril_f = jnp.where(tri, 1.0, 0.0).astype(F32)
    cs = [c_ref[h] for h in range(H_MIX)]
    ns = [n_ref[h:h + 1, :] for h in range(H_MIX)]
    ms = [m_ref[h:h + 1, 0:1] for h in range(H_MIX)]

    for c in range(tb // cl):
        r0 = c * cl
        bcum = _dot_hi(tril_f, lf_all[r0:r0 + cl, :])
        bcum_t = bcum.T
        li_t = li_all[r0:r0 + cl, :].T
        for h in range(H_MIX):
            q = x_ref[r0:r0 + cl, h * HEAD_DIM:(h + 1) * HEAD_DIM]
            k = x_ref[r0:r0 + cl, D_MIX + h * HEAD_DIM:D_MIX + (h + 1) * HEAD_DIM] * (HEAD_DIM ** -0.5)
            v = x_ref[r0:r0 + cl, 2 * D_MIX + h * HEAD_DIM:2 * D_MIX + (h + 1) * HEAD_DIM]
            og = x_ref[r0:r0 + cl, 3 * D_MIX + h * HEAD_DIM:3 * D_MIX + (h + 1) * HEAD_DIM]
            b = bcum[:, GATE_B_F + h:GATE_B_F + h + 1]
            br = bcum_t[GATE_B_F + h:GATE_B_F + h + 1, :]
            ic = li_all[r0:r0 + cl, GATE_B_I + h:GATE_B_I + h + 1]
            ir = li_t[GATE_B_I + h:GATE_B_I + h + 1, :]
            m_prev = ms[h]
            dlog = jnp.where(tri, b - br + ir, NEG_BIG)
            inter = b + m_prev
            m = jnp.maximum(jnp.max(dlog, axis=1, keepdims=True), inter)
            s = _dot_nt_hi(q, k) * jnp.exp(dlog - m)
            scale_prev = jnp.exp(inter - m)
            num = _dot_hi(s, v) + scale_prev * _dot_hi(q, cs[h])
            den = jnp.sum(s, axis=1, keepdims=True) + scale_prev * jnp.sum(q * ns[h], axis=1, keepdims=True)
            hh = num / jnp.maximum(jnp.abs(den), jnp.exp(-m))
            b_end = b[cl - 1:cl, :]
            wlog = b_end - b + ic
            m_new = jnp.maximum(b_end + m_prev, jnp.max(wlog, axis=0, keepdims=True))
            wk = jnp.exp(wlog - m_new) * k
            dec = jnp.exp(b_end + m_prev - m_new)
            cs[h] = dec * cs[h] + _dot_tn_hi(wk, v)
            ns[h] = dec * ns[h] + jnp.sum(wk, axis=0, keepdims=True)
            ms[h] = m_new
            o_ref[r0:r0 + cl, h * HEAD_DIM:(h + 1) * HEAD_DIM] = (_sigmoid(og) * hh).astype(o_ref.dtype)

    for h in range(H_MIX):
        c_ref[h] = cs[h]
        n_ref[h:h + 1, :] = ns[h]
        m_ref[h:h + 1, :] = jnp.broadcast_to(ms[h], (1, LANES))

    @pl.when(t == pl.num_programs(1) - 1)
    def _():
        c_out_ref[0] = c_ref[...]
        n_out_ref[0] = n_ref[...]
        m_out_ref[0] = m_ref[...]


def _mlstm_prompt(proj, gates, gbi_row, gbf_row, *, bsz, seqlen, tb):
    cl = math.gcd(seqlen, MLSTM_CHUNK)
    nt = seqlen // tb
    return pl.pallas_call(
        functools.partial(_mlstm_prompt_kernel, tb, cl),
        grid=(bsz, nt),
        in_specs=[pl.BlockSpec((tb, 4 * D_MIX), lambda b, t: (b * nt + t, COL_B_Q // (4 * D_MIX))),
                  pl.BlockSpec((tb, LANES), lambda b, t: (b * nt + t, 0)),
                  pl.BlockSpec((1, LANES), lambda b, t: (0, 0)),
                  pl.BlockSpec((1, LANES), lambda b, t: (0, 0))],
        out_specs=[pl.BlockSpec((tb, D_MIX), lambda b, t: (b * nt + t, 0)),
                   pl.BlockSpec((1, H_MIX, HEAD_DIM, HEAD_DIM), lambda b, t: (b, 0, 0, 0)),
                   pl.BlockSpec((1, SUBLANES, HEAD_DIM), lambda b, t: (b, 0, 0)),
                   pl.BlockSpec((1, SUBLANES, LANES), lambda b, t: (b, 0, 0))],
        out_shape=[jax.ShapeDtypeStruct((bsz * seqlen, D_MIX), BF16),
                   jax.ShapeDtypeStruct((bsz, H_MIX, HEAD_DIM, HEAD_DIM), F32),
                   jax.ShapeDtypeStruct((bsz, SUBLANES, HEAD_DIM), F32),
                   jax.ShapeDtypeStruct((bsz, SUBLANES, LANES), F32)],
        scratch_shapes=[pltpu.VMEM((H_MIX, HEAD_DIM, HEAD_DIM), F32),
                        pltpu.VMEM((SUBLANES, HEAD_DIM), F32),
                        pltpu.VMEM((SUBLANES, LANES), F32)],
        compiler_params=_cparams("parallel", "arbitrary"),
        name="mlstm_prompt",
    )(proj, gates, gbi_row, gbf_row)


def _gmlp_gv(dv, nw):
    gv = _gelu_tanh(dv)
    mu = jnp.mean(gv, axis=-1, keepdims=True)
    gc = gv - mu
    var = jnp.mean(gc * gc, axis=-1, keepdims=True)
    return gc * lax.rsqrt(var + LN_EPS) * nw


def _gmlp_prompt_kernel(tb, u_ref, v_ref, nw_ref, ws_ref, bt_ref, o_ref):
    gu = _gelu_tanh(u_ref[...])
    gv = _gmlp_gv(v_ref[...], nw_ref[...])
    tri, _ = _tri_masks(GM_CHUNK)
    gw = D_MIX // GM_GROUPS
    for g in range(GM_GROUPS):
        wm = jnp.where(tri, ws_ref[g], 0.0)
        bias = bt_ref[:, g:g + 1]
        for c in range(tb // GM_CHUNK):
            r0 = c * GM_CHUNK
            z = _dot_hi(wm, gv[r0:r0 + GM_CHUNK, g * gw:(g + 1) * gw]) + bias
            o_ref[r0:r0 + GM_CHUNK, g * gw:(g + 1) * gw] = (
                gu[r0:r0 + GM_CHUNK, g * gw:(g + 1) * gw] * z).astype(o_ref.dtype)


def _gmlp_prompt(proj, nw_row, ws, b_t, *, rows, tb):
    return pl.pallas_call(
        functools.partial(_gmlp_prompt_kernel, tb),
        grid=(rows // tb,),
        in_specs=[pl.BlockSpec((tb, D_MIX), lambda i: (i, COL_D_U // D_MIX)),
                  pl.BlockSpec((tb, D_MIX), lambda i: (i, COL_D_V // D_MIX)),
                  pl.BlockSpec((1, D_MIX), lambda i: (0, 0)),
                  pl.BlockSpec((GM_GROUPS, GM_CHUNK, GM_CHUNK), lambda i: (0, 0, 0)),
                  pl.BlockSpec((GM_CHUNK, GM_GROUPS), lambda i: (0, 0))],
        out_specs=pl.BlockSpec((tb, D_MIX), lambda i: (i, 0)),
        out_shape=jax.ShapeDtypeStruct((rows, D_MIX), BF16),
        compiler_params=_cparams("parallel"),
        name="gmlp_prompt",
    )(proj, proj, nw_row, ws, b_t)


KMEAN_ROWS = 128


def _rope_tables(pos):
    half = ROT_DIM // 2
    inv_freq = ROPE_THETA ** (-jnp.arange(half, dtype=F32) * (2.0 / ROT_DIM))
    ang = pos.astype(F32)[:, None] * inv_freq[None, :]
    cos, sin = jnp.cos(ang), jnp.sin(ang)
    rest = jnp.ones((pos.shape[0], HEAD_DIM - ROT_DIM), F32)
    return (jnp.concatenate([cos, cos, rest], axis=1),
            jnp.concatenate([-sin, sin, 0.0 * rest], axis=1))


def _rope(x, cos, sin):
    lane = lax.broadcasted_iota(jnp.int32, (x.shape[0], HEAD_DIM), 1)
    first_half = lane < ROT_DIM // 2
    outs = []
    for h in range(H_MIX):
        xh = x[:, h * HEAD_DIM:(h + 1) * HEAD_DIM]
        rot = jnp.where(first_half, pltpu.roll(xh, HEAD_DIM - ROT_DIM // 2, 1), pltpu.roll(xh, ROT_DIM // 2, 1))
        outs.append(xh * cos + rot * sin)
    return jnp.concatenate(outs, axis=1)


def _rope_kernel(q_ref, k_ref, cos_ref, sin_ref, qo_ref, ko_ref, km_ref):
    t = pl.program_id(1)
    cos = cos_ref[...]
    sin = sin_ref[...]
    qo_ref[...] = _rope(q_ref[...], cos, sin)
    kr = _rope(k_ref[...], cos, sin)
    ko_ref[...] = kr

    @pl.when(t == 0)
    def _():
        km_ref[...] = jnp.zeros_like(km_ref)

    km_ref[0, pl.ds(t, 1), :] = jnp.mean(kr, axis=0, keepdims=True)


def _rope_prompt(proj, cos_tab, sin_tab, *, bsz, seqlen):
    nb = seqlen // MOBA_BLOCK
    return pl.pallas_call(
        _rope_kernel,
        grid=(bsz, nb),
        in_specs=[pl.BlockSpec((MOBA_BLOCK, D_MIX), lambda b, t: (b * nb + t, COL_C_Q // D_MIX)),
                  pl.BlockSpec((MOBA_BLOCK, D_MIX), lambda b, t: (b * nb + t, COL_C_K // D_MIX)),
                  pl.BlockSpec((MOBA_BLOCK, HEAD_DIM), lambda b, t: (t, 0)),
                  pl.BlockSpec((MOBA_BLOCK, HEAD_DIM), lambda b, t: (t, 0))],
        out_specs=[pl.BlockSpec((MOBA_BLOCK, D_MIX), lambda b, t: (b * nb + t, 0)),
                   pl.BlockSpec((MOBA_BLOCK, D_MIX), lambda b, t: (b * nb + t, 0)),
                   pl.BlockSpec((1, KMEAN_ROWS, D_MIX), lambda b, t: (b, 0, 0))],
        out_shape=[jax.ShapeDtypeStruct((bsz * seqlen, D_MIX), F32),
                   jax.ShapeDtypeStruct((bsz * seqlen, D_MIX), F32),
                   jax.ShapeDtypeStruct((bsz, KMEAN_ROWS, D_MIX), F32)],
        compiler_params=_cparams("parallel", "arbitrary"),
        name="rope_prompt",
    )(proj, proj, cos_tab, sin_tab)


def _topk_block_mask(gate, n_valid):
    lane = lax.broadcasted_iota(jnp.int32, gate.shape, 1)
    neg_inf = float("-inf")
    g = jnp.where(lane < n_valid, gate, neg_inf)
    sel = jnp.zeros(gate.shape, F32)
    for _ in range(MOBA_TOPK):
        mx = jnp.max(g, axis=1, keepdims=True)
        first = jnp.min(jnp.where(g == mx, lane, KMEAN_ROWS), axis=1, keepdims=True)
        pick = (lane == first) & (mx > neg_inf)
        sel = jnp.where(pick, 1.0, sel)
        g = jnp.where(pick, neg_inf, g)
    return sel


def _moba_prompt_kernel(q_ref, k_ref, v_ref, km_ref, o_ref):
    qt = pl.program_id(2)
    blk = MOBA_BLOCK
    q = q_ref[...]
    sel = _topk_block_mask(_dot_nt_hi(q, km_ref[0]), qt)
    lane = lax.broadcasted_iota(jnp.int32, sel.shape, 1)
    qs = (q * (HEAD_DIM ** -0.5)).astype(BF16)
    tri, _ = _tri_masks(blk)

    own0 = pl.multiple_of(qt * blk, blk)
    s = _dot_nt(qs, k_ref[pl.ds(own0, blk), :])
    s = jnp.where(tri, s, NEG_BIG)
    m = jnp.max(s, axis=1, keepdims=True)
    p = jnp.exp(s - m)
    l = jnp.sum(p, axis=1, keepdims=True)
    acc = _dot(p, v_ref[pl.ds(own0, blk), :])

    def body(j, carry):
        m, l, acc = carry
        r0 = pl.multiple_of(j * blk, blk)
        picked = jnp.sum(jnp.where(lane == j, sel, 0.0), axis=1, keepdims=True) > 0.0
        s = jnp.where(picked, _dot_nt(qs, k_ref[pl.ds(r0, blk), :]), NEG_BIG)
        m_new = jnp.maximum(m, jnp.max(s, axis=1, keepdims=True))
        alpha = jnp.exp(m - m_new)
        p = jnp.where(picked, jnp.exp(s - m_new), 0.0)
        l = alpha * l + jnp.sum(p, axis=1, keepdims=True)
        acc = alpha * acc + _dot(p, v_ref[pl.ds(r0, blk), :])
        return m_new, l, acc

    m, l, acc = lax.fori_loop(0, qt, body, (m, l, acc))
    o_ref[...] = (acc / l).astype(o_ref.dtype)


def _moba_prompt(q_rope, k_rope, proj, kmean, *, bsz, seqlen):
    nb = seqlen // MOBA_BLOCK
    return pl.pallas_call(
        _moba_prompt_kernel,
        grid=(bsz, H_MIX, nb),
        in_specs=[pl.BlockSpec((MOBA_BLOCK, HEAD_DIM), lambda b, h, t: (b * nb + t, h)),
                  pl.BlockSpec((seqlen, HEAD_DIM), lambda b, h, t: (b, h)),
                  pl.BlockSpec((seqlen, HEAD_DIM), lambda b, h, t: (b, COL_C_V // HEAD_DIM + h)),
                  pl.BlockSpec((1, KMEAN_ROWS, HEAD_DIM), lambda b, h, t: (b, 0, h))],
        out_specs=pl.BlockSpec((MOBA_BLOCK, HEAD_DIM), lambda b, h, t: (b * nb + t, h)),
        out_shape=jax.ShapeDtypeStruct((bsz * seqlen, D_MIX), BF16),
        compiler_params=_cparams("parallel", "parallel", "arbitrary"),
        name="moba_prompt",
    )(q_rope, k_rope, proj, kmean)


SAMPLE_GROUP = SUBLANES


def _columns(rows):
    pad = jnp.zeros((HEAD_DIM - SAMPLE_GROUP, HEAD_DIM), F32)
    return jnp.concatenate([rows, pad], axis=0).T


def _sample_state_kernel(p_ref, g_ref, conv_ref, s_ref, c_ref, n_ref, m_ref,
                         cw_ref, alog_ref, dtb_ref, nw_ref, gbi_ref, gbf_ref,
                         gmnw_ref, gmw0_ref, gmb0_ref, cos_ref, sin_ref,
                         oa_ref, ob_ref, od_ref, convo_ref, so_ref, co_ref, no_ref, mo_ref,
                         qr_ref, kr_ref, gv_ref):
    bg = SAMPLE_GROUP
    gates = g_ref[...]

    cw = cw_ref[...]
    xa = p_ref[:, COL_A_Q:COL_A_Q + 3 * D_MIX]
    y = cw[CONV_W - 1:CONV_W] * xa
    for j in range(CONV_W - 1):
        y = y + cw[j:j + 1] * conv_ref[j]
    y = _silu(y)
    for j in range(CONV_W - 2):
        convo_ref[j] = conv_ref[j + 1]
    convo_ref[CONV_W - 2] = xa
    beta_all = _sigmoid(gates)
    eg_all = jnp.exp(-jnp.exp(alog_ref[...]) * _softplus(gates + dtb_ref[...]))
    nw = nw_ref[...]
    for h in range(H_MIX):
        hs = slice(h * HEAD_DIM, (h + 1) * HEAD_DIM)
        q = y[:, COL_A_Q + h * HEAD_DIM:COL_A_Q + (h + 1) * HEAD_DIM]
        k = y[:, COL_A_K + h * HEAD_DIM:COL_A_K + (h + 1) * HEAD_DIM]
        v = y[:, COL_A_V + h * HEAD_DIM:COL_A_V + (h + 1) * HEAD_DIM]
        z = p_ref[:, COL_A_Z + h * HEAD_DIM:COL_A_Z + (h + 1) * HEAD_DIM]
        q = q * lax.rsqrt(jnp.sum(q * q, axis=-1, keepdims=True) + 1e-6) * (HEAD_DIM ** -0.5)
        k = k * lax.rsqrt(jnp.sum(k * k, axis=-1, keepdims=True) + 1e-6)
        beta = beta_all[:, GATE_A_BETA + h:GATE_A_BETA + h + 1]
        eg = eg_all[:, GATE_A_DEC + h:GATE_A_DEC + h + 1]
        qk = jnp.sum(q * k, axis=-1, keepdims=True)
        kt = _columns(k)
        qt = _columns(q)
        for i in range(bg):
            s = s_ref[i, h]
            kcol = kt[:, i:i + 1]
            e_i = eg[i:i + 1, :]
            ks = jnp.sum(kcol * s, axis=0, keepdims=True)
            qs = jnp.sum(qt[:, i:i + 1] * s, axis=0, keepdims=True)
            v_new = beta[i:i + 1, :] * (v[i:i + 1, :] - e_i * ks)
            oa_ref[i:i + 1, hs] = e_i * qs + qk[i:i + 1, :] * v_new
            so_ref[i, h] = e_i * s + kcol * v_new
        o = oa_ref[:, hs]
        o = o * lax.rsqrt(jnp.mean(o * o, axis=-1, keepdims=True) + LN_EPS) * nw
        oa_ref[:, hs] = o * _silu(z)

    li_all = gates + gbi_ref[...]
    lf_all = _log_sigmoid(gates + gbf_ref[...])
    for h in range(H_MIX):
        hs = slice(h * HEAD_DIM, (h + 1) * HEAD_DIM)
        q = p_ref[:, COL_B_Q + h * HEAD_DIM:COL_B_Q + (h + 1) * HEAD_DIM]
        k = p_ref[:, COL_B_K + h * HEAD_DIM:COL_B_K + (h + 1) * HEAD_DIM] * (HEAD_DIM ** -0.5)
        v = p_ref[:, COL_B_V + h * HEAD_DIM:COL_B_V + (h + 1) * HEAD_DIM]
        og = p_ref[:, COL_B_O + h * HEAD_DIM:COL_B_O + (h + 1) * HEAD_DIM]
        li = li_all[:, GATE_B_I + h:GATE_B_I + h + 1]
        lf = lf_all[:, GATE_B_F + h:GATE_B_F + h + 1]
        m_prev = m_ref[:, h:h + 1]
        n_prev = n_ref[:, hs]
        m_new = jnp.maximum(lf + m_prev, li)
        w_in = jnp.exp(li - m_new)
        dec = jnp.exp(lf + m_prev - m_new)
        sc = jnp.sum(q * k, axis=-1, keepdims=True) * w_in
        den = sc + dec * jnp.sum(q * n_prev, axis=-1, keepdims=True)
        denom = jnp.maximum(jnp.abs(den), jnp.exp(-m_new))
        no_ref[:, hs] = dec * n_prev + w_in * k
        mo_ref[:, h:h + 1] = m_new
        kt = _columns(k)
        qt = _columns(q)
        wv = w_in * v
        for i in range(bg):
            c = c_ref[i, h]
            d_i = dec[i:i + 1, :]
            qc = jnp.sum(qt[:, i:i + 1] * c, axis=0, keepdims=True)
            ob_ref[i:i + 1, hs] = sc[i:i + 1, :] * v[i:i + 1, :] + d_i * qc
            co_ref[i, h] = d_i * c + kt[:, i:i + 1] * wv[i:i + 1, :]
        ob_ref[:, hs] = _sigmoid(og) * (ob_ref[:, hs] / denom)

    gu = _gelu_tanh(p_ref[:, COL_D_U:COL_D_U + D_MIX])
    gv = _gmlp_gv(p_ref[:, COL_D_V:COL_D_V + D_MIX], gmnw_ref[...])
    gv_ref[...] = gv
    od_ref[...] = gu * (gmw0_ref[...] * gv + gmb0_ref[...])

    cos = cos_ref[...]
    sin = sin_ref[...]
    qr_ref[...] = _rope(p_ref[:, COL_C_Q:COL_C_Q + D_MIX], cos, sin)
    kr_ref[...] = _rope(p_ref[:, COL_C_K:COL_C_K + D_MIX], cos, sin)


def _sample_state(proj, gates, conv_t, s0, c0, n0, m0, conv_w, alog_row, dtb_row, nw_row, gbi_row, gbf_row,
                  gm_nw_row, gm_w0_row, gm_b0_row, cos_row, sin_row, *, row0, nrows):
    bg = SAMPLE_GROUP
    rb0 = row0 // bg
    full = lambda shape: pl.BlockSpec(shape, lambda i: (0,) * len(shape))
    row_blk = lambda w: pl.BlockSpec((bg, w), lambda i: (i, 0))
    mat_blk = pl.BlockSpec((bg, H_MIX, HEAD_DIM, HEAD_DIM), lambda i: (i, 0, 0, 0))
    conv_blk = pl.BlockSpec((CONV_W - 1, bg, 3 * D_MIX), lambda i: (0, i, 0))
    f = lambda shape: jax.ShapeDtypeStruct(shape, F32)
    return pl.pallas_call(
        _sample_state_kernel,
        grid=(nrows // bg,),
        in_specs=[pl.BlockSpec((bg, N_MAIN), lambda i: (rb0 + i, 0)),
                  pl.BlockSpec((bg, LANES), lambda i: (rb0 + i, 0)),
                  conv_blk, mat_blk, mat_blk, row_blk(D_MIX), row_blk(H_MIX),
                  full((CONV_W, 3 * D_MIX)), full((1, LANES)), full((1, LANES)), full((1, HEAD_DIM)),
                  full((1, LANES)), full((1, LANES)),
                  full((1, D_MIX)), full((1, D_MIX)), full((1, D_MIX)),
                  full((1, HEAD_DIM)), full((1, HEAD_DIM))],
        out_specs=[row_blk(D_MIX), row_blk(D_MIX), row_blk(D_MIX), conv_blk, mat_blk, mat_blk,
                   row_blk(D_MIX), row_blk(H_MIX), row_blk(D_MIX), row_blk(D_MIX), row_blk(D_MIX)],
        out_shape=[f((nrows, D_MIX)), f((nrows, D_MIX)), f((nrows, D_MIX)),
                   f((CONV_W - 1, nrows, 3 * D_MIX)),
                   f((nrows, H_MIX, HEAD_DIM, HEAD_DIM)), f((nrows, H_MIX, HEAD_DIM, HEAD_DIM)),
                   f((nrows, D_MIX)), f((nrows, H_MIX)),
                   f((nrows, D_MIX)), f((nrows, D_MIX)), f((nrows, D_MIX))],
        compiler_params=_cparams("parallel"),
        name="sample_state",
    )(proj, gates, conv_t, s0, c0, n0, m0, conv_w, alog_row, dtb_row, nw_row, gbi_row, gbf_row,
      gm_nw_row, gm_w0_row, gm_b0_row, cos_row, sin_row)


def _head_lane_weights(q_row):
    lane = lax.broadcasted_iota(jnp.int32, (HEAD_DIM, LANES), 1)
    parts = []
    for h in range(H_MIX):
        rows = jnp.broadcast_to(q_row[:, h * HEAD_DIM:(h + 1) * HEAD_DIM], (HEAD_DIM, HEAD_DIM))
        parts.append(jnp.where(lane == h, rows.T, 0.0))
    return jnp.concatenate(parts, axis=0)


def _moba_decode_kernel(n_pages, page_size, pt_ref, q_ref, kn_ref, vn_ref, *refs):
    del pt_ref
    k_refs = refs[:n_pages]
    v_refs = refs[n_pages:2 * n_pages]
    o_ref = refs[2 * n_pages]
    pages_per_block = MOBA_BLOCK // page_size
    n_blocks = n_pages // pages_per_block
    scale = HEAD_DIM ** -0.5

    w = _head_lane_weights(q_ref[0])
    s_pages = [_dot_hi(k_refs[p][0, 0], w) for p in range(n_pages)]
    own = _dot_hi(jnp.broadcast_to(kn_ref[0], (SUBLANES, D_MIX)), w)[0:1, :] * scale

    gates, bmax = [], []
    for j in range(n_blocks):
        blk = s_pages[j * pages_per_block:(j + 1) * pages_per_block]
        tot = sum(jnp.sum(s, axis=0, keepdims=True) for s in blk)
        gates.append(tot * (1.0 / MOBA_BLOCK))
        mx = blk[0]
        for s in blk[1:]:
            mx = jnp.maximum(mx, s)
        bmax.append(jnp.max(mx, axis=0, keepdims=True) * scale)
    sels = []
    for n in range(n_blocks):
        rank = jnp.zeros((1, LANES), F32)
        for m in range(n_blocks):
            if m == n:
                continue
            ahead = gates[m] > gates[n]
            if m < n:
                ahead = ahead | (gates[m] == gates[n])
            rank = rank + jnp.where(ahead, 1.0, 0.0)
        sels.append(rank < float(MOBA_TOPK))

    m_all = own
    for j in range(n_blocks):
        m_all = jnp.maximum(m_all, jnp.where(sels[j], bmax[j], NEG_BIG))
    e_own = jnp.exp(own - m_all)
    l = e_own
    acc = [e_own[:, h:h + 1] * vn_ref[0][:, h * HEAD_DIM:(h + 1) * HEAD_DIM] for h in range(H_MIX)]
    for p in range(n_pages):
        prob = jnp.where(sels[p // pages_per_block], jnp.exp(s_pages[p] * scale - m_all), 0.0)
        l = l + jnp.sum(prob, axis=0, keepdims=True)
        vp = v_refs[p][0, 0]
        for h in range(H_MIX):
            acc[h] = acc[h] + jnp.sum(prob[:, h:h + 1] * vp[:, h * HEAD_DIM:(h + 1) * HEAD_DIM],
                                      axis=0, keepdims=True)
    for h in range(H_MIX):
        o_ref[0, :, h * HEAD_DIM:(h + 1) * HEAD_DIM] = acc[h] / l[:, h:h + 1]


def _moba_decode(page_table, q_rope, k_new, v_new, cache_k, cache_v, *, layer):
    bsz, n_pages = page_table.shape
    page_size = cache_k.shape[2]
    assert MOBA_BLOCK % page_size == 0 and (n_pages * page_size) % MOBA_BLOCK == 0
    assert (n_pages * page_size) // MOBA_BLOCK >= MOBA_TOPK
    row = pl.BlockSpec((1, 1, D_MIX), lambda b, pt: (b, 0, 0))
    page_specs = [pl.BlockSpec((1, 1, page_size, D_MIX), lambda b, pt, p=p: (layer, pt[b, p], 0, 0))
                  for p in range(n_pages)]
    return pl.pallas_call(
        functools.partial(_moba_decode_kernel, n_pages, page_size),
        grid_spec=pltpu.PrefetchScalarGridSpec(
            num_scalar_prefetch=1,
            grid=(bsz,),
            in_specs=[row, row, row] + page_specs + page_specs,
            out_specs=row),
        out_shape=jax.ShapeDtypeStruct((bsz, 1, D_MIX), F32),
        compiler_params=_cparams("parallel"),
        name="moba_decode",
    )(page_table, q_rope, k_new, v_new, *([cache_k] * n_pages), *([cache_v] * n_pages))


def _largest_divisor(n, candidates):
    for c in candidates:
        if n % c == 0:
            return c
    raise ValueError(f"no tile in {candidates} divides {n}")


def _tiles(m_rows, seqlen):
    return dict(
        proj_tm=_largest_divisor(m_rows, (1664, 1024, 512, 256, 128)),
        proj_tn=512,
        out_tm=_largest_divisor(m_rows, (320, 256, 128)),
        ffn_tm=_largest_divisor(m_rows, (640, 512, 256, 128)),
        ffn_tf=512,
        scan_tb=_largest_divisor(seqlen, (256, 128)),
        gmlp_tb=_largest_divisor(seqlen, (512, 256, 128)),
    )


def _lane_row(vals, offset):
    return jnp.zeros((1, LANES), F32).at[0, offset:offset + vals.shape[0]].set(vals.astype(F32))


def kernel(x_prompt, x_sample, state_gdn_conv, state_gdn_s, state_mlstm_c, state_mlstm_n, state_mlstm_m,
           cache_k, cache_v, page_table, w_in, gdn_conv_w, gdn_a_log, gdn_dt_bias, gdn_norm_w, mlstm_gate_b,
           gmlp_norm_w, gmlp_ws, gmlp_b, w_out, ln1_w, ln1_b, w_up, w_down, ln2_w, ln2_b):
    bsz, seq, d_model = x_prompt.shape
    dec_b, dec_s, _ = x_sample.shape
    depth = w_in.shape[0]
    assert dec_s == 1 and d_model == N_MIXERS * D_MIX and seq >= CONV_W - 1
    assert seq % MOBA_BLOCK == 0 and seq // MOBA_BLOCK <= KMEAN_ROWS and seq % GM_CHUNK == 0
    assert w_in.shape[2] == N_MAIN + 4 * H_MIX
    alpha = (2.0 * depth) ** 0.25
    mp = bsz * seq
    m_rows = mp + dec_b
    past_len = page_table.shape[1] * cache_k.shape[2]
    n_pool, page_size = cache_k.shape[1], cache_k.shape[2]
    tl = _tiles(m_rows, seq)
    gw = D_MIX // GM_GROUPS

    x = jnp.concatenate([x_prompt.reshape(mp, d_model), x_sample.reshape(dec_b, d_model)], axis=0)
    x_bf = x.astype(BF16)
    cos_p, sin_p = _rope_tables(jnp.arange(seq, dtype=jnp.int32))
    cos_s, sin_s = _rope_tables(past_len + jnp.arange(dec_s, dtype=jnp.int32))
    ck = cache_k.reshape(depth, n_pool, page_size, D_MIX)
    cv = cache_v.reshape(depth, n_pool, page_size, D_MIX)

    a_end = 4 * D_MIX
    b0 = a_end + 2 * H_MIX
    b_end = b0 + 4 * D_MIX
    c0 = b_end + 2 * H_MIX

    p_st, s_st = [], []
    for l in range(depth):
        w = w_in[l]
        w_main = jnp.concatenate([w[:, :a_end], w[:, b0:b_end], w[:, c0:]], axis=1).astype(BF16)
        w_gate = jnp.concatenate([w[:, a_end:b0], w[:, b_end:c0],
                                  jnp.zeros((d_model, LANES - 4 * H_MIX), F32)], axis=1).astype(BF16)
        alog_row = _lane_row(gdn_a_log[l], GATE_A_DEC)
        dtb_row = _lane_row(gdn_dt_bias[l], GATE_A_DEC)
        nw_row = gdn_norm_w[l].reshape(1, HEAD_DIM).astype(F32)
        gbi_row = _lane_row(mlstm_gate_b[l][:H_MIX], GATE_B_I)
        gbf_row = _lane_row(mlstm_gate_b[l][H_MIX:], GATE_B_F)
        gm_nw_row = gmlp_norm_w[l].reshape(1, D_MIX).astype(F32)
        conv_w = gdn_conv_w[l].astype(F32)

        proj, gates = _proj(x_bf, w_main, w_gate, tm=tl["proj_tm"], tn=tl["proj_tn"])

        oa_p, gdn_s_p = _gdn_prompt(proj, gates, conv_w, alog_row, dtb_row, nw_row,
                                    bsz=bsz, seqlen=seq, tb=tl["scan_tb"])
        ob_p, ml_c_p, ml_n_p, ml_m_p = _mlstm_prompt(proj, gates, gbi_row, gbf_row,
                                                     bsz=bsz, seqlen=seq, tb=tl["scan_tb"])
        q_rope, k_rope, kmean = _rope_prompt(proj, cos_p, sin_p, bsz=bsz, seqlen=seq)
        oc_p = _moba_prompt(q_rope, k_rope, proj, kmean, bsz=bsz, seqlen=seq)
        od_p = _gmlp_prompt(proj, gm_nw_row, gmlp_ws[l].astype(F32), gmlp_b[l].astype(F32).T,
                            rows=mp, tb=tl["gmlp_tb"])

        (oa_s, ob_s, od_s, conv_s, gdn_s_s, ml_c_s, ml_n_s, ml_m_s, q_s, k_s, gv_s) = _sample_state(
            proj, gates, jnp.transpose(state_gdn_conv[l].astype(F32), (1, 0, 2)),
            state_gdn_s[l].astype(F32), state_mlstm_c[l].astype(F32),
            state_mlstm_n[l].astype(F32).reshape(dec_b, D_MIX), state_mlstm_m[l].astype(F32),
            conv_w, alog_row, dtb_row, nw_row, gbi_row, gbf_row, gm_nw_row,
            jnp.repeat(gmlp_ws[l][:, 0, 0].astype(F32), gw).reshape(1, D_MIX),
            jnp.repeat(gmlp_b[l][:, 0].astype(F32), gw).reshape(1, D_MIX),
            cos_s, sin_s, row0=mp, nrows=dec_b)
        v_s = proj[mp:, COL_C_V:COL_C_V + D_MIX]
        oc_s = _moba_decode(page_table, q_s.reshape(dec_b, 1, D_MIX), k_s.reshape(dec_b, 1, D_MIX),
                            v_s.reshape(dec_b, 1, D_MIX), ck, cv, layer=l).reshape(dec_b, D_MIX)

        mix = jnp.concatenate([jnp.concatenate([oa_p, ob_p, oc_p, od_p], axis=1),
                               jnp.concatenate([oa_s, ob_s, oc_s, od_s], axis=1).astype(BF16)], axis=0)
        x1, x1_bf = _outproj_ln(mix, w_out[l].astype(BF16), x, ln1_w[l].reshape(1, d_model).astype(F32),
                                ln1_b[l].reshape(1, d_model).astype(F32), alpha=alpha, tm=tl["out_tm"])
        x, x_bf = _ffn_ln(x1_bf, w_up[l].astype(BF16), w_down[l].astype(BF16), x1,
                          ln2_w[l].reshape(1, d_model).astype(F32), ln2_b[l].reshape(1, d_model).astype(F32),
                          alpha=alpha, tm=tl["ffn_tm"], tf=tl["ffn_tf"])

        proj_p = proj[:mp].reshape(bsz, seq, N_MAIN)
        p_st.append((proj_p[:, seq - (CONV_W - 1):, COL_A_Q:COL_A_Q + 3 * D_MIX],
                     gdn_s_p, ml_c_p, ml_n_p[:, :H_MIX], ml_m_p[:, :H_MIX, 0],
                     k_rope.reshape(bsz, seq, H_MIX, HEAD_DIM),
                     proj_p[:, :, COL_C_V:COL_C_V + D_MIX].reshape(bsz, seq, H_MIX, HEAD_DIM)))
        s_st.append((jnp.transpose(conv_s, (1, 0, 2)), gdn_s_s, ml_c_s,
                     ml_n_s.reshape(dec_b, H_MIX, HEAD_DIM), ml_m_s,
                     k_s.reshape(dec_b, dec_s, H_MIX, HEAD_DIM), v_s.reshape(dec_b, dec_s, H_MIX, HEAD_DIM),
                     gv_s.reshape(dec_b, dec_s, D_MIX)))

    def stk(sts, i):
        return jnp.stack([s[i] for s in sts], axis=0)

    dt = x_prompt.dtype
    yp = x[:mp].reshape(bsz, seq, d_model).astype(dt)
    ys = x[mp:].reshape(dec_b, dec_s, d_model).astype(dt)
    return (yp, ys) + tuple(stk(p_st, i).astype(dt) for i in range(7)) + tuple(stk(s_st, i).astype(dt) for i in range(8))
```

```python
import functools
import math

import jax
import jax.numpy as jnp
from jax import lax
from jax.experimental import pallas as pl
from jax.experimental.pallas import tpu as pltpu

F32 = jnp.float32
BF16 = jnp.bfloat16
HIGHEST = lax.Precision.HIGHEST

HEAD_DIM = 128
N_MIXERS = 4
CONV_W = 4
GDN_CHUNK = 64
MLSTM_CHUNK = 64
MOBA_BLOCK = 256
MOBA_TOPK = 3
GM_CHUNK = 128
GM_GROUPS = 4
ROPE_THETA = 500000.0
ROT_DIM = HEAD_DIM // 4
LN_EPS = 1e-5
NEG_BIG = -1e30

LANES = 128
SUBLANES = 8
VMEM_LIMIT_BYTES = 56 * 1024 * 1024


def _cparams(*sem):
    return pltpu.CompilerParams(dimension_semantics=sem, vmem_limit_bytes=VMEM_LIMIT_BYTES)


def _dot(a, b):
    return jnp.dot(a.astype(BF16), b.astype(BF16), preferred_element_type=F32)


def _dot_nt(a, b):
    return lax.dot_general(a.astype(BF16), b.astype(BF16), (((1,), (1,)), ((), ())),
                           preferred_element_type=F32)


def _dot_tn(a, b):
    return lax.dot_general(a.astype(BF16), b.astype(BF16), (((0,), (0,)), ((), ())),
                           preferred_element_type=F32)


def _dot_hi(a, b):
    return jnp.dot(a, b, precision=HIGHEST, preferred_element_type=F32)


def _dot_nt_hi(a, b):
    return lax.dot_general(a, b, (((1,), (1,)), ((), ())), precision=HIGHEST,
                           preferred_element_type=F32)


def _split2(x):
    hi = x.astype(BF16)
    return hi, (x - hi.astype(F32)).astype(BF16)


def _dot_split(a_parts, b_parts):
    ah, al = a_parts
    bh, bl = b_parts
    return (jnp.dot(ah, bh, preferred_element_type=F32) + jnp.dot(ah, bl, preferred_element_type=F32)
            + jnp.dot(al, bh, preferred_element_type=F32))


def _cumsum_rows(tril_bf, x):
    x0 = x.astype(BF16)
    r1 = x - x0.astype(F32)
    x1 = r1.astype(BF16)
    x2 = (r1 - x1.astype(F32)).astype(BF16)
    return (jnp.dot(tril_bf, x0, preferred_element_type=F32) + jnp.dot(tril_bf, x1, preferred_element_type=F32)
            + jnp.dot(tril_bf, x2, preferred_element_type=F32))


def _sigmoid(x):
    return 1.0 / (1.0 + jnp.exp(-x))


def _silu(x):
    return x * _sigmoid(x)


def _softplus(x):
    return jnp.maximum(x, 0.0) + jnp.log(1.0 + jnp.exp(-jnp.abs(x)))


def _log_sigmoid(x):
    return -_softplus(-x)


def _gelu_tanh(x):
    return 0.5 * x * (1.0 + jnp.tanh(math.sqrt(2.0 / math.pi) * (x + 0.044715 * (x * x * x))))


def _layer_norm(x, w, b):
    mu = jnp.mean(x, axis=-1, keepdims=True)
    xc = x - mu
    var = jnp.mean(xc * xc, axis=-1, keepdims=True)
    return xc * lax.rsqrt(var + LN_EPS) * w + b


D_MIX = 512
H_MIX = D_MIX // HEAD_DIM
COL_A_Q, COL_A_K, COL_A_V, COL_A_Z = 0, 512, 1024, 1536
COL_B_Q, COL_B_K, COL_B_V, COL_B_O = 2048, 2560, 3072, 3584
COL_C_Q, COL_C_K, COL_C_V = 4096, 4608, 5120
COL_D_U, COL_D_V = 5632, 6144
N_MAIN = 6656
GATE_A_BETA, GATE_A_DEC, GATE_B_I, GATE_B_F = 0, 4, 8, 12


def _tri_masks(n):
    r = lax.broadcasted_iota(jnp.int32, (n, n), 0)
    c = lax.broadcasted_iota(jnp.int32, (n, n), 1)
    return r >= c, r > c


def _proj_kernel(x_ref, w_ref, wg_ref, o_ref, g_ref, xb_ref):
    @pl.when(pl.program_id(1) == 0)
    def _():
        xb_ref[...] = x_ref[...].astype(BF16)
        g_ref[...] = jnp.dot(xb_ref[...], wg_ref[...], preferred_element_type=F32)

    o_ref[...] = jnp.dot(xb_ref[...], w_ref[...], preferred_element_type=F32)


def _proj(x, w_main, w_gate, *, tm, tn):
    m, d = x.shape
    n = w_main.shape[1]
    return pl.pallas_call(
        _proj_kernel,
        grid=(m // tm, n // tn),
        in_specs=[pl.BlockSpec((tm, d), lambda i, j: (i, 0)),
                  pl.BlockSpec((d, tn), lambda i, j: (0, j)),
                  pl.BlockSpec((d, LANES), lambda i, j: (0, 0))],
        out_specs=[pl.BlockSpec((tm, tn), lambda i, j: (i, j)),
                   pl.BlockSpec((tm, LANES), lambda i, j: (i, 0))],
        out_shape=[jax.ShapeDtypeStruct((m, n), F32), jax.ShapeDtypeStruct((m, LANES), F32)],
        scratch_shapes=[pltpu.VMEM((tm, d), BF16)],
        compiler_params=_cparams("parallel", "arbitrary"),
        name="proj",
    )(x, w_main, w_gate)


def _outproj_ln_kernel(alpha, a_ref, b_ref, c_ref, d_ref, w_ref, x_ref, lw_ref, lb_ref, o_ref, ob_ref):
    y = alpha * x_ref[...]
    for i, m_ref in enumerate((a_ref, b_ref, c_ref, d_ref)):
        y = y + jnp.dot(m_ref[...].astype(BF16), w_ref[i * D_MIX:(i + 1) * D_MIX, :],
                        preferred_element_type=F32)
    y = _layer_norm(y, lw_ref[...], lb_ref[...])
    o_ref[...] = y
    ob_ref[...] = y.astype(BF16)


def _outproj_ln(mixes, w_out_bf, x, ln_w, ln_b, *, alpha, tm):
    m, d = x.shape
    mix_spec = pl.BlockSpec((tm, D_MIX), lambda i: (i, 0))
    return pl.pallas_call(
        functools.partial(_outproj_ln_kernel, alpha),
        grid=(m // tm,),
        in_specs=[mix_spec, mix_spec, mix_spec, mix_spec,
                  pl.BlockSpec((d, d), lambda i: (0, 0)),
                  pl.BlockSpec((tm, d), lambda i: (i, 0)),
                  pl.BlockSpec((1, d), lambda i: (0, 0)),
                  pl.BlockSpec((1, d), lambda i: (0, 0))],
        out_specs=[pl.BlockSpec((tm, d), lambda i: (i, 0)),
                   pl.BlockSpec((tm, d), lambda i: (i, 0))],
        out_shape=[jax.ShapeDtypeStruct((m, d), F32), jax.ShapeDtypeStruct((m, d), BF16)],
        compiler_params=_cparams("parallel"),
        name="outproj_ln",
    )(*mixes, w_out_bf, x, ln_w, ln_b)


def _ffn_ln_kernel(alpha, xb_ref, wu_ref, wd_ref, x_ref, lw_ref, lb_ref, o_ref, ob_ref, acc_ref):
    j = pl.program_id(1)

    @pl.when(j == 0)
    def _():
        acc_ref[...] = alpha * x_ref[...]

    h = jnp.dot(xb_ref[...], wu_ref[...], preferred_element_type=F32)
    h = jnp.maximum(h, 0.0)
    h = (h * h).astype(BF16)
    acc_ref[...] += jnp.dot(h, wd_ref[...], preferred_element_type=F32)

    @pl.when(j == pl.num_programs(1) - 1)
    def _():
        y = _layer_norm(acc_ref[...], lw_ref[...], lb_ref[...])
        o_ref[...] = y
        ob_ref[...] = y.astype(BF16)


def _ffn_ln(x_bf, w_up_bf, w_down_bf, x, ln_w, ln_b, *, alpha, tm, tf):
    m, d = x.shape
    f = w_up_bf.shape[1]
    return pl.pallas_call(
        functools.partial(_ffn_ln_kernel, alpha),
        grid=(m // tm, f // tf),
        in_specs=[pl.BlockSpec((tm, d), lambda i, j: (i, 0)),
                  pl.BlockSpec((d, tf), lambda i, j: (0, j)),
                  pl.BlockSpec((tf, d), lambda i, j: (j, 0)),
                  pl.BlockSpec((tm, d), lambda i, j: (i, 0)),
                  pl.BlockSpec((1, d), lambda i, j: (0, 0)),
                  pl.BlockSpec((1, d), lambda i, j: (0, 0))],
        out_specs=[pl.BlockSpec((tm, d), lambda i, j: (i, 0)),
                   pl.BlockSpec((tm, d), lambda i, j: (i, 0))],
        out_shape=[jax.ShapeDtypeStruct((m, d), F32), jax.ShapeDtypeStruct((m, d), BF16)],
        scratch_shapes=[pltpu.VMEM((tm, d), F32)],
        compiler_params=_cparams("parallel", "arbitrary"),
        name="ffn_ln",
    )(x_bf, w_up_bf, w_down_bf, x, ln_w, ln_b)


def _unit_lower_inverse(a):
    n = a.shape[0]
    r = lax.broadcasted_iota(jnp.int32, (n, n), 0)
    c = lax.broadcasted_iota(jnp.int32, (n, n), 1)
    eye = jnp.where(r == c, 1.0, 0.0).astype(F32)
    t = eye - a
    p_parts = _split2(a)
    span = 2
    while span < n:
        p_parts = _split2(_dot_split(p_parts, p_parts))
        t = t + _dot_split(_split2(t), p_parts)
        span *= 2
    return t


def _gdn_prompt_kernel(tb, cl, x_ref, g_ref, cw_ref, alog_ref, dtb_ref, nw_ref,
                       o_ref, s_out_ref, xbuf, s_ref):
    t = pl.program_id(1)
    dqkv = 3 * D_MIX

    @pl.when(t == 0)
    def _():
        xbuf[0:SUBLANES, :] = jnp.zeros((SUBLANES, dqkv), F32)
        s_ref[...] = jnp.zeros_like(s_ref)

    @pl.when(t > 0)
    def _():
        xbuf[0:SUBLANES, :] = xbuf[tb:tb + SUBLANES, :]

    xbuf[SUBLANES:SUBLANES + tb, :] = x_ref[:, 0:dqkv]
    cw = cw_ref[...]
    y = cw[0:1] * xbuf[SUBLANES - 3:SUBLANES - 3 + tb, :]
    for j in range(1, CONV_W):
        y = y + cw[j:j + 1] * xbuf[SUBLANES - 3 + j:SUBLANES - 3 + j + tb, :]
    y = _silu(y)

    gates = g_ref[...]
    beta_all = _sigmoid(gates)
    g_all = -jnp.exp(alog_ref[...]) * _softplus(gates + dtb_ref[...])
    tri, tri_s = _tri_masks(cl)
    tril_bf = jnp.where(tri, 1.0, 0.0).astype(BF16)
    nw = nw_ref[...]
    states = [s_ref[h] for h in range(H_MIX)]

    for c in range(tb // cl):
        r0 = c * cl
        gcum = _cumsum_rows(tril_bf, g_all[r0:r0 + cl, :])
        gcum_t = gcum.T
        for h in range(H_MIX):
            q = y[r0:r0 + cl, COL_A_Q + h * HEAD_DIM:COL_A_Q + (h + 1) * HEAD_DIM]
            k = y[r0:r0 + cl, COL_A_K + h * HEAD_DIM:COL_A_K + (h + 1) * HEAD_DIM]
            v = y[r0:r0 + cl, COL_A_V + h * HEAD_DIM:COL_A_V + (h + 1) * HEAD_DIM]
            z = x_ref[r0:r0 + cl, COL_A_Z + h * HEAD_DIM:COL_A_Z + (h + 1) * HEAD_DIM]
            q = q * lax.rsqrt(jnp.sum(q * q, axis=-1, keepdims=True) + 1e-6) * (HEAD_DIM ** -0.5)
            k = k * lax.rsqrt(jnp.sum(k * k, axis=-1, keepdims=True) + 1e-6)
            beta = beta_all[r0:r0 + cl, GATE_A_BETA + h:GATE_A_BETA + h + 1]
            gc = gcum[:, GATE_A_DEC + h:GATE_A_DEC + h + 1]
            gr = gcum_t[GATE_A_DEC + h:GATE_A_DEC + h + 1, :]
            decay = jnp.exp(jnp.where(tri, gc - gr, NEG_BIG))
            kb = k * beta
            vb = v * beta
            egc = jnp.exp(gc)
            kk = _dot_nt(jnp.concatenate([kb, q], axis=0), k)
            a = jnp.where(tri_s, kk[0:cl] * decay, 0.0)
            attn = jnp.where(tri, kk[cl:2 * cl] * decay, 0.0)
            t_inv = _unit_lower_inverse(a)
            uw = _dot(t_inv, jnp.concatenate([vb, kb * egc], axis=1))
            gc_last = gc[cl - 1:cl, :]
            s = states[h]
            ws_qs = _dot(jnp.concatenate([uw[:, HEAD_DIM:], q * egc], axis=0), s)
            v_new = uw[:, :HEAD_DIM] - ws_qs[0:cl]
            o = ws_qs[cl:2 * cl] + _dot(attn, v_new)
            states[h] = s * jnp.exp(gc_last) + _dot_tn(k * jnp.exp(gc_last - gc), v_new)
            o = o * lax.rsqrt(jnp.mean(o * o, axis=-1, keepdims=True) + LN_EPS) * nw
            o_ref[r0:r0 + cl, h * HEAD_DIM:(h + 1) * HEAD_DIM] = (o * _silu(z)).astype(o_ref.dtype)

    for h in range(H_MIX):
        s_ref[h] = states[h]

    @pl.when(t == pl.num_programs(1) - 1)
    def _():
        s_out_ref[0] = s_ref[...]


def _gdn_prompt(proj, gates, conv_w, alog_row, dtb_row, nw_row, *, bsz, seqlen, tb):
    cl = math.gcd(seqlen, GDN_CHUNK)
    nt = seqlen // tb
    return pl.pallas_call(
        functools.partial(_gdn_prompt_kernel, tb, cl),
        grid=(bsz, nt),
        in_specs=[pl.BlockSpec((tb, 4 * D_MIX), lambda b, t: (b * nt + t, 0)),
                  pl.BlockSpec((tb, LANES), lambda b, t: (b * nt + t, 0)),
                  pl.BlockSpec((CONV_W, 3 * D_MIX), lambda b, t: (0, 0)),
                  pl.BlockSpec((1, LANES), lambda b, t: (0, 0)),
                  pl.BlockSpec((1, LANES), lambda b, t: (0, 0)),
                  pl.BlockSpec((1, HEAD_DIM), lambda b, t: (0, 0))],
        out_specs=[pl.BlockSpec((tb, D_MIX), lambda b, t: (b * nt + t, 0)),
                   pl.BlockSpec((1, H_MIX, HEAD_DIM, HEAD_DIM), lambda b, t: (b, 0, 0, 0))],
        out_shape=[jax.ShapeDtypeStruct((bsz * seqlen, D_MIX), BF16),
                   jax.ShapeDtypeStruct((bsz, H_MIX, HEAD_DIM, HEAD_DIM), F32)],
        scratch_shapes=[pltpu.VMEM((tb + SUBLANES, 3 * D_MIX), F32),
                        pltpu.VMEM((H_MIX, HEAD_DIM, HEAD_DIM), F32)],
        compiler_params=_cparams("parallel", "arbitrary"),
        name="gdn_prompt",
    )(proj, gates, conv_w, alog_row, dtb_row, nw_row)


def _mlstm_prompt_kernel(tb, cl, x_ref, g_ref, gbi_ref, gbf_ref,
                         o_ref, c_out_ref, n_out_ref, m_out_ref, c_ref, n_ref, m_ref):
    t = pl.program_id(1)

    @pl.when(t == 0)
    def _():
        c_ref[...] = jnp.zeros_like(c_ref)
        n_ref[...] = jnp.zeros_like(n_ref)
        m_ref[...] = jnp.zeros_like(m_ref)

    gates = g_ref[...]
    li_all = gates + gbi_ref[...]
    lf_all = _log_sigmoid(gates + gbf_ref[...])
    tri, _ = _tri_masks(cl)
    tril_bf = jnp.where(tri, 1.0, 0.0).astype(BF16)
    cs = [c_ref[h] for h in range(H_MIX)]
    ns = [n_ref[h:h + 1, :] for h in range(H_MIX)]
    ms = [m_ref[h:h + 1, 0:1] for h in range(H_MIX)]

    for c in range(tb // cl):
        r0 = c * cl
        bcum = _cumsum_rows(tril_bf, lf_all[r0:r0 + cl, :])
        bcum_t = bcum.T
        li_t = li_all[r0:r0 + cl, :].T
        for h in range(H_MIX):
            q = x_ref[r0:r0 + cl, h * HEAD_DIM:(h + 1) * HEAD_DIM]
            k = x_ref[r0:r0 + cl, D_MIX + h * HEAD_DIM:D_MIX + (h + 1) * HEAD_DIM] * (HEAD_DIM ** -0.5)
            v = x_ref[r0:r0 + cl, 2 * D_MIX + h * HEAD_DIM:2 * D_MIX + (h + 1) * HEAD_DIM]
            og = x_ref[r0:r0 + cl, 3 * D_MIX + h * HEAD_DIM:3 * D_MIX + (h + 1) * HEAD_DIM]
            b = bcum[:, GATE_B_F + h:GATE_B_F + h + 1]
            br = bcum_t[GATE_B_F + h:GATE_B_F + h + 1, :]
            ic = li_all[r0:r0 + cl, GATE_B_I + h:GATE_B_I + h + 1]
            ir = li_t[GATE_B_I + h:GATE_B_I + h + 1, :]
            m_prev = ms[h]
            dlog = jnp.where(tri, b - br + ir, NEG_BIG)
            inter = b + m_prev
            m = jnp.maximum(jnp.max(dlog, axis=1, keepdims=True), inter)
            s = _dot_nt(q, k) * jnp.exp(dlog - m)
            scale_prev = jnp.exp(inter - m)
            num = _dot(s, v) + scale_prev * _dot(q, cs[h])
            den = jnp.sum(s, axis=1, keepdims=True) + scale_prev * jnp.sum(q * ns[h], axis=1, keepdims=True)
            hh = num / jnp.maximum(jnp.abs(den), jnp.exp(-m))
            b_end = b[cl - 1:cl, :]
            wlog = b_end - b + ic
            m_new = jnp.maximum(b_end + m_prev, jnp.max(wlog, axis=0, keepdims=True))
            wk = jnp.exp(wlog - m_new) * k
            dec = jnp.exp(b_end + m_prev - m_new)
            cs[h] = dec * cs[h] + _dot_tn(wk, v)
            ns[h] = dec * ns[h] + jnp.sum(wk, axis=0, keepdims=True)
            ms[h] = m_new
            o_ref[r0:r0 + cl, h * HEAD_DIM:(h + 1) * HEAD_DIM] = (_sigmoid(og) * hh).astype(o_ref.dtype)

    for h in range(H_MIX):
        c_ref[h] = cs[h]
        n_ref[h:h + 1, :] = ns[h]
        m_ref[h:h + 1, :] = jnp.broadcast_to(ms[h], (1, LANES))

    @pl.when(t == pl.num_programs(1) - 1)
    def _():
        c_out_ref[0] = c_ref[...]
        n_out_ref[0] = n_ref[...]
        m_out_ref[0] = m_ref[...]


def _mlstm_prompt(proj, gates, gbi_row, gbf_row, *, bsz, seqlen, tb):
    cl = math.gcd(seqlen, MLSTM_CHUNK)
    nt = seqlen // tb
    return pl.pallas_call(
        functools.partial(_mlstm_prompt_kernel, tb, cl),
        grid=(bsz, nt),
        in_specs=[pl.BlockSpec((tb, 4 * D_MIX), lambda b, t: (b * nt + t, COL_B_Q // (4 * D_MIX))),
                  pl.BlockSpec((tb, LANES), lambda b, t: (b * nt + t, 0)),
                  pl.BlockSpec((1, LANES), lambda b, t: (0, 0)),
                  pl.BlockSpec((1, LANES), lambda b, t: (0, 0))],
        out_specs=[pl.BlockSpec((tb, D_MIX), lambda b, t: (b * nt + t, 0)),
                   pl.BlockSpec((1, H_MIX, HEAD_DIM, HEAD_DIM), lambda b, t: (b, 0, 0, 0)),
                   pl.BlockSpec((1, SUBLANES, HEAD_DIM), lambda b, t: (b, 0, 0)),
                   pl.BlockSpec((1, SUBLANES, LANES), lambda b, t: (b, 0, 0))],
        out_shape=[jax.ShapeDtypeStruct((bsz * seqlen, D_MIX), BF16),
                   jax.ShapeDtypeStruct((bsz, H_MIX, HEAD_DIM, HEAD_DIM), F32),
                   jax.ShapeDtypeStruct((bsz, SUBLANES, HEAD_DIM), F32),
                   jax.ShapeDtypeStruct((bsz, SUBLANES, LANES), F32)],
        scratch_shapes=[pltpu.VMEM((H_MIX, HEAD_DIM, HEAD_DIM), F32),
                        pltpu.VMEM((SUBLANES, HEAD_DIM), F32),
                        pltpu.VMEM((SUBLANES, LANES), F32)],
        compiler_params=_cparams("parallel", "arbitrary"),
        name="mlstm_prompt",
    )(proj, gates, gbi_row, gbf_row)


def _gmlp_gv(dv, nw):
    gv = _gelu_tanh(dv)
    mu = jnp.mean(gv, axis=-1, keepdims=True)
    gc = gv - mu
    var = jnp.mean(gc * gc, axis=-1, keepdims=True)
    return gc * lax.rsqrt(var + LN_EPS) * nw


def _gmlp_prompt_kernel(tb, u_ref, v_ref, nw_ref, ws_ref, bt_ref, o_ref):
    gu = _gelu_tanh(u_ref[...])
    gv = _gmlp_gv(v_ref[...], nw_ref[...])
    tri, _ = _tri_masks(GM_CHUNK)
    gw = D_MIX // GM_GROUPS
    for g in range(GM_GROUPS):
        wm = jnp.where(tri, ws_ref[g], 0.0)
        bias = bt_ref[:, g:g + 1]
        for c in range(tb // GM_CHUNK):
            r0 = c * GM_CHUNK
            z = _dot(wm, gv[r0:r0 + GM_CHUNK, g * gw:(g + 1) * gw]) + bias
            o_ref[r0:r0 + GM_CHUNK, g * gw:(g + 1) * gw] = (
                gu[r0:r0 + GM_CHUNK, g * gw:(g + 1) * gw] * z).astype(o_ref.dtype)


def _gmlp_prompt(proj, nw_row, ws, b_t, *, rows, tb):
    return pl.pallas_call(
        functools.partial(_gmlp_prompt_kernel, tb),
        grid=(rows // tb,),
        in_specs=[pl.BlockSpec((tb, D_MIX), lambda i: (i, COL_D_U // D_MIX)),
                  pl.BlockSpec((tb, D_MIX), lambda i: (i, COL_D_V // D_MIX)),
                  pl.BlockSpec((1, D_MIX), lambda i: (0, 0)),
                  pl.BlockSpec((GM_GROUPS, GM_CHUNK, GM_CHUNK), lambda i: (0, 0, 0)),
                  pl.BlockSpec((GM_CHUNK, GM_GROUPS), lambda i: (0, 0))],
        out_specs=pl.BlockSpec((tb, D_MIX), lambda i: (i, 0)),
        out_shape=jax.ShapeDtypeStruct((rows, D_MIX), BF16),
        compiler_params=_cparams("parallel"),
        name="gmlp_prompt",
    )(proj, proj, nw_row, ws, b_t)


KMEAN_ROWS = 128


def _rope_tables(pos):
    half = ROT_DIM // 2
    inv_freq = ROPE_THETA ** (-jnp.arange(half, dtype=F32) * (2.0 / ROT_DIM))
    ang = pos.astype(F32)[:, None] * inv_freq[None, :]
    cos, sin = jnp.cos(ang), jnp.sin(ang)
    rest = jnp.ones((pos.shape[0], HEAD_DIM - ROT_DIM), F32)
    return (jnp.concatenate([cos, cos, rest], axis=1),
            jnp.concatenate([-sin, sin, 0.0 * rest], axis=1))


def _rope(x, cos, sin):
    lane = lax.broadcasted_iota(jnp.int32, (x.shape[0], HEAD_DIM), 1)
    first_half = lane < ROT_DIM // 2
    outs = []
    for h in range(H_MIX):
        xh = x[:, h * HEAD_DIM:(h + 1) * HEAD_DIM]
        rot = jnp.where(first_half, pltpu.roll(xh, HEAD_DIM - ROT_DIM // 2, 1), pltpu.roll(xh, ROT_DIM // 2, 1))
        outs.append(xh * cos + rot * sin)
    return jnp.concatenate(outs, axis=1)


def _rope_kernel(q_ref, k_ref, v_ref, cos_ref, sin_ref, qo_ref, ko_ref, kb_ref, vb_ref, km_ref):
    t = pl.program_id(1)
    cos = cos_ref[...]
    sin = sin_ref[...]
    qo_ref[...] = _rope(q_ref[...], cos, sin)
    kr = _rope(k_ref[...], cos, sin)
    ko_ref[...] = kr
    kb_ref[...] = kr.astype(BF16)
    vb_ref[...] = v_ref[...].astype(BF16)

    @pl.when(t == 0)
    def _():
        km_ref[...] = jnp.zeros_like(km_ref)

    km_ref[0, pl.ds(t, 1), :] = jnp.mean(kr, axis=0, keepdims=True)


def _rope_prompt(proj, cos_tab, sin_tab, *, bsz, seqlen):
    nb = seqlen // MOBA_BLOCK
    col = lambda c: pl.BlockSpec((MOBA_BLOCK, D_MIX), lambda b, t: (b * nb + t, c // D_MIX))
    row = pl.BlockSpec((MOBA_BLOCK, D_MIX), lambda b, t: (b * nb + t, 0))
    tab = pl.BlockSpec((MOBA_BLOCK, HEAD_DIM), lambda b, t: (t, 0))
    return pl.pallas_call(
        _rope_kernel,
        grid=(bsz, nb),
        in_specs=[col(COL_C_Q), col(COL_C_K), col(COL_C_V), tab, tab],
        out_specs=[row, row, row, row, pl.BlockSpec((1, KMEAN_ROWS, D_MIX), lambda b, t: (b, 0, 0))],
        out_shape=[jax.ShapeDtypeStruct((bsz * seqlen, D_MIX), F32),
                   jax.ShapeDtypeStruct((bsz * seqlen, D_MIX), F32),
                   jax.ShapeDtypeStruct((bsz * seqlen, D_MIX), BF16),
                   jax.ShapeDtypeStruct((bsz * seqlen, D_MIX), BF16),
                   jax.ShapeDtypeStruct((bsz, KMEAN_ROWS, D_MIX), F32)],
        compiler_params=_cparams("parallel", "arbitrary"),
        name="rope_prompt",
    )(proj, proj, proj, cos_tab, sin_tab)


def _topk_block_mask(gate, n_valid):
    lane = lax.broadcasted_iota(jnp.int32, gate.shape, 1)
    neg_inf = float("-inf")
    g = jnp.where(lane < n_valid, gate, neg_inf)
    sel = jnp.zeros(gate.shape, F32)
    for _ in range(MOBA_TOPK):
        mx = jnp.max(g, axis=1, keepdims=True)
        first = jnp.min(jnp.where(g == mx, lane, KMEAN_ROWS), axis=1, keepdims=True)
        pick = (lane == first) & (mx > neg_inf)
        sel = jnp.where(pick, 1.0, sel)
        g = jnp.where(pick, neg_inf, g)
    return sel


def _moba_prompt_kernel(q_ref, k_ref, v_ref, km_ref, o_ref):
    qt = pl.program_id(1)
    blk = MOBA_BLOCK
    tri, _ = _tri_masks(blk)
    own0 = pl.multiple_of(qt * blk, blk)
    heads = [slice(h * HEAD_DIM, (h + 1) * HEAD_DIM) for h in range(H_MIX)]

    qs, sels, carry = [], [], []
    for hs in heads:
        q = q_ref[:, hs]
        sels.append(_topk_block_mask(_dot_nt_hi(q, km_ref[0][:, hs]), qt).astype(BF16))
        qh = (q * (HEAD_DIM ** -0.5)).astype(BF16)
        qs.append(qh)
        s = jnp.where(tri, _dot_nt(qh, k_ref[pl.ds(own0, blk), hs]), NEG_BIG)
        m = jnp.max(s, axis=1, keepdims=True)
        p = jnp.exp(s - m)
        carry += [m, jnp.sum(p, axis=1, keepdims=True), _dot(p, v_ref[pl.ds(own0, blk), hs])]

    def body(j, carry):
        r0 = pl.multiple_of(j * blk, blk)
        onehot = jnp.where(lax.broadcasted_iota(jnp.int32, (KMEAN_ROWS, blk), 0) == j, 1.0, 0.0).astype(BF16)
        out = []
        for h, hs in enumerate(heads):
            m, l, acc = carry[3 * h:3 * h + 3]
            picked = jnp.dot(sels[h], onehot, preferred_element_type=F32) > 0.5
            s = jnp.where(picked, _dot_nt(qs[h], k_ref[pl.ds(r0, blk), hs]), NEG_BIG)
            m_new = jnp.maximum(m, jnp.max(s, axis=1, keepdims=True))
            alpha = jnp.exp(m - m_new)
            p = jnp.where(picked, jnp.exp(s - m_new), 0.0)
            out += [m_new, alpha * l + jnp.sum(p, axis=1, keepdims=True),
                    alpha * acc + _dot(p, v_ref[pl.ds(r0, blk), hs])]
        return tuple(out)

    carry = lax.fori_loop(0, qt, body, tuple(carry))
    for h, hs in enumerate(heads):
        o_ref[:, hs] = (carry[3 * h + 2] / carry[3 * h + 1]).astype(o_ref.dtype)


def _moba_prompt(q_rope, k_bf, v_bf, kmean, *, bsz, seqlen):
    nb = seqlen // MOBA_BLOCK
    return pl.pallas_call(
        _moba_prompt_kernel,
        grid=(bsz, nb),
        in_specs=[pl.BlockSpec((MOBA_BLOCK, D_MIX), lambda b, t: (b * nb + t, 0)),
                  pl.BlockSpec((seqlen, D_MIX), lambda b, t: (b, 0)),
                  pl.BlockSpec((seqlen, D_MIX), lambda b, t: (b, 0)),
                  pl.BlockSpec((1, KMEAN_ROWS, D_MIX), lambda b, t: (b, 0, 0))],
        out_specs=pl.BlockSpec((MOBA_BLOCK, D_MIX), lambda b, t: (b * nb + t, 0)),
        out_shape=jax.ShapeDtypeStruct((bsz * seqlen, D_MIX), BF16),
        compiler_params=_cparams("parallel", "arbitrary"),
        name="moba_prompt",
    )(q_rope, k_bf, v_bf, kmean)


SAMPLE_GROUP = SUBLANES


def _columns(rows):
    pad = jnp.zeros((HEAD_DIM - SAMPLE_GROUP, HEAD_DIM), F32)
    return jnp.concatenate([rows, pad], axis=0).T


def _sample_state_kernel(p_ref, g_ref, conv_ref, s_ref, c_ref, n_ref, m_ref,
                         cw_ref, alog_ref, dtb_ref, nw_ref, gbi_ref, gbf_ref,
                         gmnw_ref, gmw0_ref, gmb0_ref, cos_ref, sin_ref,
                         oa_ref, ob_ref, od_ref, convo_ref, so_ref, co_ref, no_ref, mo_ref,
                         qr_ref, kr_ref, gv_ref):
    bg = SAMPLE_GROUP
    gates = g_ref[...]

    cw = cw_ref[...]
    xa = p_ref[:, COL_A_Q:COL_A_Q + 3 * D_MIX]
    y = cw[CONV_W - 1:CONV_W] * xa
    for j in range(CONV_W - 1):
        y = y + cw[j:j + 1] * conv_ref[j]
    y = _silu(y)
    for j in range(CONV_W - 2):
        convo_ref[j] = conv_ref[j + 1]
    convo_ref[CONV_W - 2] = xa
    beta_all = _sigmoid(gates)
    eg_all = jnp.exp(-jnp.exp(alog_ref[...]) * _softplus(gates + dtb_ref[...]))
    nw = nw_ref[...]
    for h in range(H_MIX):
        hs = slice(h * HEAD_DIM, (h + 1) * HEAD_DIM)
        q = y[:, COL_A_Q + h * HEAD_DIM:COL_A_Q + (h + 1) * HEAD_DIM]
        k = y[:, COL_A_K + h * HEAD_DIM:COL_A_K + (h + 1) * HEAD_DIM]
        v = y[:, COL_A_V + h * HEAD_DIM:COL_A_V + (h + 1) * HEAD_DIM]
        z = p_ref[:, COL_A_Z + h * HEAD_DIM:COL_A_Z + (h + 1) * HEAD_DIM]
        q = q * lax.rsqrt(jnp.sum(q * q, axis=-1, keepdims=True) + 1e-6) * (HEAD_DIM ** -0.5)
        k = k * lax.rsqrt(jnp.sum(k * k, axis=-1, keepdims=True) + 1e-6)
        beta = beta_all[:, GATE_A_BETA + h:GATE_A_BETA + h + 1]
        eg = eg_all[:, GATE_A_DEC + h:GATE_A_DEC + h + 1]
        qk = jnp.sum(q * k, axis=-1, keepdims=True)
        kt = _columns(k)
        qt = _columns(q)
        for i in range(bg):
            s = s_ref[i, h]
            kcol = kt[:, i:i + 1]
            e_i = eg[i:i + 1, :]
            ks = jnp.sum(kcol * s, axis=0, keepdims=True)
            qs = jnp.sum(qt[:, i:i + 1] * s, axis=0, keepdims=True)
            v_new = beta[i:i + 1, :] * (v[i:i + 1, :] - e_i * ks)
            oa_ref[i:i + 1, hs] = e_i * qs + qk[i:i + 1, :] * v_new
            so_ref[i, h] = e_i * s + kcol * v_new
        o = oa_ref[:, hs]
        o = o * lax.rsqrt(jnp.mean(o * o, axis=-1, keepdims=True) + LN_EPS) * nw
        oa_ref[:, hs] = o * _silu(z)

    li_all = gates + gbi_ref[...]
    lf_all = _log_sigmoid(gates + gbf_ref[...])
    for h in range(H_MIX):
        hs = slice(h * HEAD_DIM, (h + 1) * HEAD_DIM)
        q = p_ref[:, COL_B_Q + h * HEAD_DIM:COL_B_Q + (h + 1) * HEAD_DIM]
        k = p_ref[:, COL_B_K + h * HEAD_DIM:COL_B_K + (h + 1) * HEAD_DIM] * (HEAD_DIM ** -0.5)
        v = p_ref[:, COL_B_V + h * HEAD_DIM:COL_B_V + (h + 1) * HEAD_DIM]
        og = p_ref[:, COL_B_O + h * HEAD_DIM:COL_B_O + (h + 1) * HEAD_DIM]
        li = li_all[:, GATE_B_I + h:GATE_B_I + h + 1]
        lf = lf_all[:, GATE_B_F + h:GATE_B_F + h + 1]
        m_prev = m_ref[:, h:h + 1]
        n_prev = n_ref[:, hs]
        m_new = jnp.maximum(lf + m_prev, li)
        w_in = jnp.exp(li - m_new)
        dec = jnp.exp(lf + m_prev - m_new)
        sc = jnp.sum(q * k, axis=-1, keepdims=True) * w_in
        den = sc + dec * jnp.sum(q * n_prev, axis=-1, keepdims=True)
        denom = jnp.maximum(jnp.abs(den), jnp.exp(-m_new))
        no_ref[:, hs] = dec * n_prev + w_in * k
        mo_ref[:, h:h + 1] = m_new
        kt = _columns(k)
        qt = _columns(q)
        wv = w_in * v
        for i in range(bg):
            c = c_ref[i, h]
            d_i = dec[i:i + 1, :]
            qc = jnp.sum(qt[:, i:i + 1] * c, axis=0, keepdims=True)
            ob_ref[i:i + 1, hs] = sc[i:i + 1, :] * v[i:i + 1, :] + d_i * qc
            co_ref[i, h] = d_i * c + kt[:, i:i + 1] * wv[i:i + 1, :]
        ob_ref[:, hs] = _sigmoid(og) * (ob_ref[:, hs] / denom)

    gu = _gelu_tanh(p_ref[:, COL_D_U:COL_D_U + D_MIX])
    gv = _gmlp_gv(p_ref[:, COL_D_V:COL_D_V + D_MIX], gmnw_ref[...])
    gv_ref[...] = gv
    od_ref[...] = gu * (gmw0_ref[...] * gv + gmb0_ref[...])

    cos = cos_ref[...]
    sin = sin_ref[...]
    qr_ref[...] = _rope(p_ref[:, COL_C_Q:COL_C_Q + D_MIX], cos, sin)
    kr_ref[...] = _rope(p_ref[:, COL_C_K:COL_C_K + D_MIX], cos, sin)


def _sample_state(proj, gates, conv_t, s0, c0, n0, m0, conv_w, alog_row, dtb_row, nw_row, gbi_row, gbf_row,
                  gm_nw_row, gm_w0_row, gm_b0_row, cos_row, sin_row):
    bg = SAMPLE_GROUP
    nrows = proj.shape[0]
    full = lambda shape: pl.BlockSpec(shape, lambda i: (0,) * len(shape))
    row_blk = lambda w: pl.BlockSpec((bg, w), lambda i: (i, 0))
    mat_blk = pl.BlockSpec((bg, H_MIX, HEAD_DIM, HEAD_DIM), lambda i: (i, 0, 0, 0))
    conv_blk = pl.BlockSpec((CONV_W - 1, bg, 3 * D_MIX), lambda i: (0, i, 0))
    f = lambda shape: jax.ShapeDtypeStruct(shape, F32)
    return pl.pallas_call(
        _sample_state_kernel,
        grid=(nrows // bg,),
        in_specs=[row_blk(N_MAIN), row_blk(LANES),
                  conv_blk, mat_blk, mat_blk, row_blk(D_MIX), row_blk(H_MIX),
                  full((CONV_W, 3 * D_MIX)), full((1, LANES)), full((1, LANES)), full((1, HEAD_DIM)),
                  full((1, LANES)), full((1, LANES)),
                  full((1, D_MIX)), full((1, D_MIX)), full((1, D_MIX)),
                  full((1, HEAD_DIM)), full((1, HEAD_DIM))],
        out_specs=[row_blk(D_MIX), row_blk(D_MIX), row_blk(D_MIX), conv_blk, mat_blk, mat_blk,
                   row_blk(D_MIX), row_blk(H_MIX), row_blk(D_MIX), row_blk(D_MIX), row_blk(D_MIX)],
        out_shape=[f((nrows, D_MIX)), f((nrows, D_MIX)), f((nrows, D_MIX)),
                   f((CONV_W - 1, nrows, 3 * D_MIX)),
                   f((nrows, H_MIX, HEAD_DIM, HEAD_DIM)), f((nrows, H_MIX, HEAD_DIM, HEAD_DIM)),
                   f((nrows, D_MIX)), f((nrows, H_MIX)),
                   f((nrows, D_MIX)), f((nrows, D_MIX)), f((nrows, D_MIX))],
        compiler_params=_cparams("parallel"),
        name="sample_state",
    )(proj, gates, conv_t, s0, c0, n0, m0, conv_w, alog_row, dtb_row, nw_row, gbi_row, gbf_row,
      gm_nw_row, gm_w0_row, gm_b0_row, cos_row, sin_row)


def _head_columns(row):
    lane = lax.broadcasted_iota(jnp.int32, (HEAD_DIM, LANES), 1)
    out = jnp.zeros((HEAD_DIM, LANES), F32)
    for h in range(H_MIX):
        rows = jnp.broadcast_to(row[:, h * HEAD_DIM:(h + 1) * HEAD_DIM], (HEAD_DIM, HEAD_DIM))
        out = jnp.where(lane == h, rows.T, out)
    return out


def _head_lanes(vals):
    lane = lax.broadcasted_iota(jnp.int32, (1, LANES), 1)
    out = jnp.zeros((1, LANES), F32)
    for h, v in enumerate(vals):
        out = jnp.where(lane == h, v, out)
    return out


def _moba_decode_kernel(n_pages, page_rows, pt_ref, q_ref, kn_ref, vn_ref, *refs):
    del pt_ref
    k_refs = refs[:n_pages]
    v_refs = refs[n_pages:2 * n_pages]
    o_ref = refs[2 * n_pages]
    page_size = page_rows // H_MIX
    pages_per_block = MOBA_BLOCK // page_size
    n_blocks = n_pages // pages_per_block
    scale = HEAD_DIM ** -0.5
    heads = [slice(h * HEAD_DIM, (h + 1) * HEAD_DIM) for h in range(H_MIX)]

    q = q_ref[0]
    kn = kn_ref[0]
    wq = _head_columns(q).astype(BF16)
    r_id = lax.broadcasted_iota(jnp.int32, (page_rows, LANES), 0)
    lane = lax.broadcasted_iota(jnp.int32, (page_rows, LANES), 1)
    head_mask = (r_id % H_MIX) == lane

    def page_head_sums(kp):
        fold = jnp.sum(kp.reshape(page_rows // SUBLANES, SUBLANES, HEAD_DIM), axis=0)
        return sum(fold[g * H_MIX:(g + 1) * H_MIX] for g in range(SUBLANES // H_MIX))

    gates = []
    for j in range(n_blocks):
        ksum = sum(page_head_sums(k_refs[p][0, 0]) for p in range(j * pages_per_block, (j + 1) * pages_per_block))
        gates.append(_head_lanes([jnp.sum(q[:, hs] * ksum[h:h + 1, :], axis=1, keepdims=True)
                                  for h, hs in enumerate(heads)]) * (1.0 / MOBA_BLOCK))
    sels = []
    for n in range(n_blocks):
        rank = jnp.zeros((1, LANES), F32)
        for m in range(n_blocks):
            if m == n:
                continue
            ahead = gates[m] > gates[n]
            if m < n:
                ahead = ahead | (gates[m] == gates[n])
            rank = rank + jnp.where(ahead, 1.0, 0.0)
        sels.append(rank < float(MOBA_TOPK))

    m_run = _head_lanes([jnp.sum(q[:, hs] * kn[:, hs], axis=1, keepdims=True) for hs in heads]) * scale
    l_run = jnp.ones((1, LANES), F32)
    acc_t = _head_columns(vn_ref[0])
    for p in range(n_pages):
        keep = head_mask & sels[p // pages_per_block]
        s = jnp.dot(k_refs[p][0, 0].astype(BF16), wq, preferred_element_type=F32) * scale
        s = jnp.where(keep, s, NEG_BIG)
        m_new = jnp.maximum(m_run, jnp.max(s, axis=0, keepdims=True))
        alpha = jnp.exp(m_run - m_new)
        prob = jnp.where(keep, jnp.exp(s - m_new), 0.0)
        l_run = alpha * l_run + jnp.sum(prob, axis=0, keepdims=True)
        acc_t = alpha * acc_t + _dot_tn(v_refs[p][0, 0], prob)
        m_run = m_new
    out = (acc_t / l_run).T
    for h, hs in enumerate(heads):
        o_ref[0, :, hs] = out[h:h + 1, :]


def _moba_decode(page_table, q_rope, k_new, v_new, cache_k, cache_v, *, layer):
    bsz, n_pages = page_table.shape
    page_rows = cache_k.shape[2]
    page_size = page_rows // H_MIX
    assert MOBA_BLOCK % page_size == 0 and (n_pages * page_size) % MOBA_BLOCK == 0
    assert (n_pages * page_size) // MOBA_BLOCK >= MOBA_TOPK and SUBLANES % H_MIX == 0
    row = pl.BlockSpec((1, 1, D_MIX), lambda b, pt: (b, 0, 0))
    page_specs = [pl.BlockSpec((1, 1, page_rows, HEAD_DIM), lambda b, pt, p=p: (layer, pt[b, p], 0, 0))
                  for p in range(n_pages)]
    return pl.pallas_call(
        functools.partial(_moba_decode_kernel, n_pages, page_rows),
        grid_spec=pltpu.PrefetchScalarGridSpec(
            num_scalar_prefetch=1,
            grid=(bsz,),
            in_specs=[row, row, row] + page_specs + page_specs,
            out_specs=row),
        out_shape=jax.ShapeDtypeStruct((bsz, 1, D_MIX), F32),
        compiler_params=_cparams("parallel"),
        name="moba_decode",
    )(page_table, q_rope, k_new, v_new, *([cache_k] * n_pages), *([cache_v] * n_pages))


def _largest_divisor(n, candidates):
    for c in candidates:
        if n % c == 0:
            return c
    raise ValueError(f"no tile in {candidates} divides {n}")


def _tiles(rows, seqlen):
    return dict(
        proj_tm=_largest_divisor(rows, (1024, 512, 256, 128)),
        proj_tn=512,
        out_tm=_largest_divisor(rows, (256, 128)),
        ffn_tm=_largest_divisor(rows, (512, 256, 128)),
        ffn_tf=512,
        scan_tb=_largest_divisor(seqlen, (256, 128, 64, 32, 16, 8)),
        gmlp_tb=_largest_divisor(seqlen, (512, 256, 128)),
    )


def _lane_row(vals, offset):
    return jnp.zeros((1, LANES), F32).at[0, offset:offset + vals.shape[0]].set(vals.astype(F32))


def kernel(x_prompt, x_sample, state_gdn_conv, state_gdn_s, state_mlstm_c, state_mlstm_n, state_mlstm_m,
           cache_k, cache_v, page_table, w_in, gdn_conv_w, gdn_a_log, gdn_dt_bias, gdn_norm_w, mlstm_gate_b,
           gmlp_norm_w, gmlp_ws, gmlp_b, w_out, ln1_w, ln1_b, w_up, w_down, ln2_w, ln2_b):
    bsz, seq, d_model = x_prompt.shape
    dec_b, dec_s, _ = x_sample.shape
    depth = w_in.shape[0]
    assert dec_s == 1 and d_model == N_MIXERS * D_MIX and seq >= CONV_W - 1
    assert seq % MOBA_BLOCK == 0 and seq // MOBA_BLOCK <= KMEAN_ROWS and seq % GM_CHUNK == 0
    assert w_in.shape[2] == N_MAIN + 4 * H_MIX and cache_k.shape[3] == H_MIX
    alpha = (2.0 * depth) ** 0.25
    mp = bsz * seq
    past_len = page_table.shape[1] * cache_k.shape[2]
    n_pool, page_size = cache_k.shape[1], cache_k.shape[2]
    tp = _tiles(mp, seq)
    ts = _tiles(dec_b, seq)
    gw = D_MIX // GM_GROUPS

    cos_p, sin_p = _rope_tables(jnp.arange(seq, dtype=jnp.int32))
    cos_s, sin_s = _rope_tables(past_len + jnp.arange(dec_s, dtype=jnp.int32))
    ck = cache_k.reshape(depth, n_pool, page_size * H_MIX, HEAD_DIM)
    cv = cache_v.reshape(depth, n_pool, page_size * H_MIX, HEAD_DIM)

    a_end = 4 * D_MIX
    b0 = a_end + 2 * H_MIX
    b_end = b0 + 4 * D_MIX
    c0 = b_end + 2 * H_MIX

    xp = x_prompt.reshape(mp, d_model)
    xs = x_sample.reshape(dec_b, d_model)
    xp_in, xs_in = xp, xs
    p_st, s_st = [], []
    for l in range(depth):
        w = w_in[l]
        w_main = jnp.concatenate([w[:, :a_end], w[:, b0:b_end], w[:, c0:]], axis=1).astype(BF16)
        w_gate = jnp.concatenate([w[:, a_end:b0], w[:, b_end:c0],
                                  jnp.zeros((d_model, LANES - 4 * H_MIX), F32)], axis=1).astype(BF16)
        alog_row = _lane_row(gdn_a_log[l], GATE_A_DEC)
        dtb_row = _lane_row(gdn_dt_bias[l], GATE_A_DEC)
        nw_row = gdn_norm_w[l].reshape(1, HEAD_DIM).astype(F32)
        gbi_row = _lane_row(mlstm_gate_b[l][:H_MIX], GATE_B_I)
        gbf_row = _lane_row(mlstm_gate_b[l][H_MIX:], GATE_B_F)
        gm_nw_row = gmlp_norm_w[l].reshape(1, D_MIX).astype(F32)
        conv_w = gdn_conv_w[l].astype(F32)
        w_out_bf = w_out[l].astype(BF16)
        w_up_bf = w_up[l].astype(BF16)
        w_down_bf = w_down[l].astype(BF16)
        ln1 = (ln1_w[l].reshape(1, d_model).astype(F32), ln1_b[l].reshape(1, d_model).astype(F32))
        ln2 = (ln2_w[l].reshape(1, d_model).astype(F32), ln2_b[l].reshape(1, d_model).astype(F32))

        proj, gates = _proj(xp_in, w_main, w_gate, tm=tp["proj_tm"], tn=tp["proj_tn"])
        oa, gdn_s_p = _gdn_prompt(proj, gates, conv_w, alog_row, dtb_row, nw_row,
                                  bsz=bsz, seqlen=seq, tb=tp["scan_tb"])
        ob, ml_c_p, ml_n_p, ml_m_p = _mlstm_prompt(proj, gates, gbi_row, gbf_row,
                                                   bsz=bsz, seqlen=seq, tb=tp["scan_tb"])
        q_rope, k_rope, k_bf, v_bf, kmean = _rope_prompt(proj, cos_p, sin_p, bsz=bsz, seqlen=seq)
        oc = _moba_prompt(q_rope, k_bf, v_bf, kmean, bsz=bsz, seqlen=seq)
        od = _gmlp_prompt(proj, gm_nw_row, gmlp_ws[l].astype(F32), gmlp_b[l].astype(F32).T,
                          rows=mp, tb=tp["gmlp_tb"])
        x1, x1_bf = _outproj_ln((oa, ob, oc, od), w_out_bf, xp, *ln1, alpha=alpha, tm=tp["out_tm"])
        xp, xp_in = _ffn_ln(x1_bf, w_up_bf, w_down_bf, x1, *ln2, alpha=alpha, tm=tp["ffn_tm"], tf=tp["ffn_tf"])
        proj_p = proj.reshape(bsz, seq, N_MAIN)
        p_st.append((proj_p[:, seq - (CONV_W - 1):, COL_A_Q:COL_A_Q + 3 * D_MIX],
                     gdn_s_p, ml_c_p, ml_n_p[:, :H_MIX], ml_m_p[:, :H_MIX, 0],
                     k_rope.reshape(bsz, seq, H_MIX, HEAD_DIM),
                     proj_p[:, :, COL_C_V:COL_C_V + D_MIX].reshape(bsz, seq, H_MIX, HEAD_DIM)))

        proj, gates = _proj(xs_in, w_main, w_gate, tm=ts["proj_tm"], tn=ts["proj_tn"])
        (oa, ob, od, conv_s, gdn_s_s, ml_c_s, ml_n_s, ml_m_s, q_s, k_s, gv_s) = _sample_state(
            proj, gates, jnp.transpose(state_gdn_conv[l].astype(F32), (1, 0, 2)),
            state_gdn_s[l].astype(F32), state_mlstm_c[l].astype(F32),
            state_mlstm_n[l].astype(F32).reshape(dec_b, D_MIX), state_mlstm_m[l].astype(F32),
            conv_w, alog_row, dtb_row, nw_row, gbi_row, gbf_row, gm_nw_row,
            jnp.repeat(gmlp_ws[l][:, 0, 0].astype(F32), gw).reshape(1, D_MIX),
            jnp.repeat(gmlp_b[l][:, 0].astype(F32), gw).reshape(1, D_MIX),
            cos_s, sin_s)
        v_s = proj[:, COL_C_V:COL_C_V + D_MIX]
        oc = _moba_decode(page_table, q_s.reshape(dec_b, 1, D_MIX), k_s.reshape(dec_b, 1, D_MIX),
                          v_s.reshape(dec_b, 1, D_MIX), ck, cv, layer=l).reshape(dec_b, D_MIX)
        x1, x1_bf = _outproj_ln((oa, ob, oc, od), w_out_bf, xs, *ln1, alpha=alpha, tm=ts["out_tm"])
        xs, xs_in = _ffn_ln(x1_bf, w_up_bf, w_down_bf, x1, *ln2, alpha=alpha, tm=ts["ffn_tm"], tf=ts["ffn_tf"])
        s_st.append((jnp.transpose(conv_s, (1, 0, 2)), gdn_s_s, ml_c_s,
                     ml_n_s.reshape(dec_b, H_MIX, HEAD_DIM), ml_m_s,
                     k_s.reshape(dec_b, dec_s, H_MIX, HEAD_DIM), v_s.reshape(dec_b, dec_s, H_MIX, HEAD_DIM),
                     gv_s.reshape(dec_b, dec_s, D_MIX)))

    def stk(sts, i):
        return jnp.stack([s[i] for s in sts], axis=0)

    dt = x_prompt.dtype
    yp = xp.reshape(bsz, seq, d_model).astype(dt)
    ys = xs.reshape(dec_b, dec_s, d_model).astype(dt)
    return (yp, ys) + tuple(stk(p_st, i).astype(dt) for i in range(7)) + tuple(stk(s_st, i).astype(dt) for i in range(8))
```

```python
import functools
import math

import jax
import jax.numpy as jnp
from jax import lax
from jax.experimental import pallas as pl
from jax.experimental.pallas import tpu as pltpu

F32 = jnp.float32
BF16 = jnp.bfloat16
HIGHEST = lax.Precision.HIGHEST

HEAD_DIM = 128
N_MIXERS = 4
CONV_W = 4
GDN_CHUNK = 64
MLSTM_CHUNK = 64
MOBA_BLOCK = 256
MOBA_TOPK = 3
GM_CHUNK = 128
GM_GROUPS = 4
ROPE_THETA = 500000.0
ROT_DIM = HEAD_DIM // 4
LN_EPS = 1e-5
NEG_BIG = -1e30

LANES = 128
SUBLANES = 8
VMEM_LIMIT_BYTES = 56 * 1024 * 1024


def _cparams(*sem):
    return pltpu.CompilerParams(dimension_semantics=sem, vmem_limit_bytes=VMEM_LIMIT_BYTES)


def _dot(a, b):
    return jnp.dot(a.astype(BF16), b.astype(BF16), preferred_element_type=F32)


def _dot_nt(a, b):
    return lax.dot_general(a.astype(BF16), b.astype(BF16), (((1,), (1,)), ((), ())),
                           preferred_element_type=F32)


def _dot_tn(a, b):
    return lax.dot_general(a.astype(BF16), b.astype(BF16), (((0,), (0,)), ((), ())),
                           preferred_element_type=F32)


def _dot_hi(a, b):
    return jnp.dot(a, b, precision=HIGHEST, preferred_element_type=F32)


def _dot_nt_hi(a, b):
    return lax.dot_general(a, b, (((1,), (1,)), ((), ())), precision=HIGHEST,
                           preferred_element_type=F32)


def _split2(x):
    hi = x.astype(BF16)
    return hi, (x - hi.astype(F32)).astype(BF16)


def _dot_split(a_parts, b_parts):
    ah, al = a_parts
    bh, bl = b_parts
    return (jnp.dot(ah, bh, preferred_element_type=F32) + jnp.dot(ah, bl, preferred_element_type=F32)
            + jnp.dot(al, bh, preferred_element_type=F32))


def _cumsum_rows(tril_bf, x):
    x0 = x.astype(BF16)
    r1 = x - x0.astype(F32)
    x1 = r1.astype(BF16)
    x2 = (r1 - x1.astype(F32)).astype(BF16)
    return (jnp.dot(tril_bf, x0, preferred_element_type=F32) + jnp.dot(tril_bf, x1, preferred_element_type=F32)
            + jnp.dot(tril_bf, x2, preferred_element_type=F32))


def _sigmoid(x):
    return 1.0 / (1.0 + jnp.exp(-x))


def _silu(x):
    return x * _sigmoid(x)


def _softplus(x):
    return jnp.maximum(x, 0.0) + jnp.log(1.0 + jnp.exp(-jnp.abs(x)))


def _log_sigmoid(x):
    return -_softplus(-x)


def _gelu_tanh(x):
    return 0.5 * x * (1.0 + jnp.tanh(math.sqrt(2.0 / math.pi) * (x + 0.044715 * (x * x * x))))


def _layer_norm(x, w, b):
    mu = jnp.mean(x, axis=-1, keepdims=True)
    xc = x - mu
    var = jnp.mean(xc * xc, axis=-1, keepdims=True)
    return xc * lax.rsqrt(var + LN_EPS) * w + b


D_MIX = 512
H_MIX = D_MIX // HEAD_DIM
COL_A_Q, COL_A_K, COL_A_V, COL_A_Z = 0, 512, 1024, 1536
COL_B_Q, COL_B_K, COL_B_V, COL_B_O = 2048, 2560, 3072, 3584
COL_C_Q, COL_C_K, COL_C_V = 4096, 4608, 5120
COL_D_U, COL_D_V = 5632, 6144
N_MAIN = 6656
GATE_A_BETA, GATE_A_DEC, GATE_B_I, GATE_B_F = 0, 4, 8, 12


def _tri_masks(n):
    r = lax.broadcasted_iota(jnp.int32, (n, n), 0)
    c = lax.broadcasted_iota(jnp.int32, (n, n), 1)
    return r >= c, r > c


def _proj_kernel(x_ref, w_ref, wg_ref, o_ref, g_ref, xb_ref):
    @pl.when(pl.program_id(1) == 0)
    def _():
        xb_ref[...] = x_ref[...].astype(BF16)
        g_ref[...] = jnp.dot(xb_ref[...], wg_ref[...], preferred_element_type=F32)

    o_ref[...] = jnp.dot(xb_ref[...], w_ref[...], preferred_element_type=F32)


def _proj(x, w_main, w_gate, *, tm, tn):
    m, d = x.shape
    n = w_main.shape[1]
    return pl.pallas_call(
        _proj_kernel,
        grid=(m // tm, n // tn),
        in_specs=[pl.BlockSpec((tm, d), lambda i, j: (i, 0)),
                  pl.BlockSpec((d, tn), lambda i, j: (0, j)),
                  pl.BlockSpec((d, LANES), lambda i, j: (0, 0))],
        out_specs=[pl.BlockSpec((tm, tn), lambda i, j: (i, j)),
                   pl.BlockSpec((tm, LANES), lambda i, j: (i, 0))],
        out_shape=[jax.ShapeDtypeStruct((m, n), F32), jax.ShapeDtypeStruct((m, LANES), F32)],
        scratch_shapes=[pltpu.VMEM((tm, d), BF16)],
        compiler_params=_cparams("parallel", "arbitrary"),
        name="proj",
    )(x, w_main, w_gate)


def _outproj_ln_kernel(alpha, a_ref, b_ref, c_ref, d_ref, w_ref, x_ref, lw_ref, lb_ref, o_ref, ob_ref):
    y = alpha * x_ref[...]
    for i, m_ref in enumerate((a_ref, b_ref, c_ref, d_ref)):
        y = y + jnp.dot(m_ref[...].astype(BF16), w_ref[i * D_MIX:(i + 1) * D_MIX, :],
                        preferred_element_type=F32)
    y = _layer_norm(y, lw_ref[...], lb_ref[...])
    o_ref[...] = y
    ob_ref[...] = y.astype(BF16)


def _outproj_ln(mixes, w_out_bf, x, ln_w, ln_b, *, alpha, tm):
    m, d = x.shape
    mix_spec = pl.BlockSpec((tm, D_MIX), lambda i: (i, 0))
    return pl.pallas_call(
        functools.partial(_outproj_ln_kernel, alpha),
        grid=(m // tm,),
        in_specs=[mix_spec, mix_spec, mix_spec, mix_spec,
                  pl.BlockSpec((d, d), lambda i: (0, 0)),
                  pl.BlockSpec((tm, d), lambda i: (i, 0)),
                  pl.BlockSpec((1, d), lambda i: (0, 0)),
                  pl.BlockSpec((1, d), lambda i: (0, 0))],
        out_specs=[pl.BlockSpec((tm, d), lambda i: (i, 0)),
                   pl.BlockSpec((tm, d), lambda i: (i, 0))],
        out_shape=[jax.ShapeDtypeStruct((m, d), F32), jax.ShapeDtypeStruct((m, d), BF16)],
        compiler_params=_cparams("parallel"),
        name="outproj_ln",
    )(*mixes, w_out_bf, x, ln_w, ln_b)


def _ffn_ln_kernel(alpha, xb_ref, wu_ref, wd_ref, x_ref, lw_ref, lb_ref, o_ref, ob_ref, acc_ref):
    j = pl.program_id(1)

    @pl.when(j == 0)
    def _():
        acc_ref[...] = alpha * x_ref[...]

    h = jnp.dot(xb_ref[...], wu_ref[...], preferred_element_type=F32)
    h = jnp.maximum(h, 0.0)
    h = (h * h).astype(BF16)
    acc_ref[...] += jnp.dot(h, wd_ref[...], preferred_element_type=F32)

    @pl.when(j == pl.num_programs(1) - 1)
    def _():
        y = _layer_norm(acc_ref[...], lw_ref[...], lb_ref[...])
        o_ref[...] = y
        ob_ref[...] = y.astype(BF16)


def _ffn_ln(x_bf, w_up_bf, w_down_bf, x, ln_w, ln_b, *, alpha, tm, tf):
    m, d = x.shape
    f = w_up_bf.shape[1]
    return pl.pallas_call(
        functools.partial(_ffn_ln_kernel, alpha),
        grid=(m // tm, f // tf),
        in_specs=[pl.BlockSpec((tm, d), lambda i, j: (i, 0)),
                  pl.BlockSpec((d, tf), lambda i, j: (0, j)),
                  pl.BlockSpec((tf, d), lambda i, j: (j, 0)),
                  pl.BlockSpec((tm, d), lambda i, j: (i, 0)),
                  pl.BlockSpec((1, d), lambda i, j: (0, 0)),
                  pl.BlockSpec((1, d), lambda i, j: (0, 0))],
        out_specs=[pl.BlockSpec((tm, d), lambda i, j: (i, 0)),
                   pl.BlockSpec((tm, d), lambda i, j: (i, 0))],
        out_shape=[jax.ShapeDtypeStruct((m, d), F32), jax.ShapeDtypeStruct((m, d), BF16)],
        scratch_shapes=[pltpu.VMEM((tm, d), F32)],
        compiler_params=_cparams("parallel", "arbitrary"),
        name="ffn_ln",
    )(x_bf, w_up_bf, w_down_bf, x, ln_w, ln_b)


def _unit_lower_inverses(mats):
    n = mats[0].shape[0]
    r = lax.broadcasted_iota(jnp.int32, (n, n), 0)
    c = lax.broadcasted_iota(jnp.int32, (n, n), 1)
    eye = jnp.where(r == c, 1.0, 0.0).astype(F32)
    ts = [eye - a for a in mats]
    ps = [_split2(a) for a in mats]
    span = 2
    while span < n:
        ps = [_split2(_dot_split(p, p)) for p in ps]
        ts = [t + _dot_split(_split2(t), p) for t, p in zip(ts, ps)]
        span *= 2
    return ts


def _gdn_prompt_kernel(tb, cl, x_ref, g_ref, cw_ref, alog_ref, dtb_ref, nw_ref,
                       o_ref, s_out_ref, xbuf, s_ref):
    t = pl.program_id(1)
    dqkv = 3 * D_MIX

    @pl.when(t == 0)
    def _():
        xbuf[0:SUBLANES, :] = jnp.zeros((SUBLANES, dqkv), F32)
        s_ref[...] = jnp.zeros_like(s_ref)

    @pl.when(t > 0)
    def _():
        xbuf[0:SUBLANES, :] = xbuf[tb:tb + SUBLANES, :]

    xbuf[SUBLANES:SUBLANES + tb, :] = x_ref[:, 0:dqkv]
    cw = cw_ref[...]
    y = cw[0:1] * xbuf[SUBLANES - 3:SUBLANES - 3 + tb, :]
    for j in range(1, CONV_W):
        y = y + cw[j:j + 1] * xbuf[SUBLANES - 3 + j:SUBLANES - 3 + j + tb, :]
    y = _silu(y)

    gates = g_ref[...]
    beta_all = _sigmoid(gates)
    g_all = -jnp.exp(alog_ref[...]) * _softplus(gates + dtb_ref[...])
    tri, tri_s = _tri_masks(cl)
    tril_bf = jnp.where(tri, 1.0, 0.0).astype(BF16)
    nw = nw_ref[...]
    states = [s_ref[h] for h in range(H_MIX)]

    a_mats, pre = [], []
    for c in range(tb // cl):
        r0 = c * cl
        gcum = _cumsum_rows(tril_bf, g_all[r0:r0 + cl, :])
        gcum_t = gcum.T
        for h in range(H_MIX):
            q = y[r0:r0 + cl, COL_A_Q + h * HEAD_DIM:COL_A_Q + (h + 1) * HEAD_DIM]
            k = y[r0:r0 + cl, COL_A_K + h * HEAD_DIM:COL_A_K + (h + 1) * HEAD_DIM]
            v = y[r0:r0 + cl, COL_A_V + h * HEAD_DIM:COL_A_V + (h + 1) * HEAD_DIM]
            q = q * lax.rsqrt(jnp.sum(q * q, axis=-1, keepdims=True) + 1e-6) * (HEAD_DIM ** -0.5)
            k = k * lax.rsqrt(jnp.sum(k * k, axis=-1, keepdims=True) + 1e-6)
            beta = beta_all[r0:r0 + cl, GATE_A_BETA + h:GATE_A_BETA + h + 1]
            gc = gcum[:, GATE_A_DEC + h:GATE_A_DEC + h + 1]
            gr = gcum_t[GATE_A_DEC + h:GATE_A_DEC + h + 1, :]
            decay = jnp.exp(jnp.where(tri, gc - gr, NEG_BIG))
            kb = k * beta
            egc = jnp.exp(gc)
            gc_last = gc[cl - 1:cl, :]
            kk = _dot_nt(jnp.concatenate([kb, q], axis=0), k)
            a_mats.append(jnp.where(tri_s, kk[0:cl] * decay, 0.0))
            pre.append(dict(attn=jnp.where(tri, kk[cl:2 * cl] * decay, 0.0),
                            rhs=jnp.concatenate([v * beta, kb * egc], axis=1),
                            q_dec=q * egc, k_tail=k * jnp.exp(gc_last - gc), g_tot=jnp.exp(gc_last)))
    t_invs = _unit_lower_inverses(a_mats)
    uws = [_dot(t_inv, p["rhs"]) for t_inv, p in zip(t_invs, pre)]

    for c in range(tb // cl):
        r0 = c * cl
        for h in range(H_MIX):
            p, uw = pre[c * H_MIX + h], uws[c * H_MIX + h]
            z = x_ref[r0:r0 + cl, COL_A_Z + h * HEAD_DIM:COL_A_Z + (h + 1) * HEAD_DIM]
            s = states[h]
            ws_qs = _dot(jnp.concatenate([uw[:, HEAD_DIM:], p["q_dec"]], axis=0), s)
            v_new = uw[:, :HEAD_DIM] - ws_qs[0:cl]
            o = ws_qs[cl:2 * cl] + _dot(p["attn"], v_new)
            states[h] = s * p["g_tot"] + _dot_tn(p["k_tail"], v_new)
            o = o * lax.rsqrt(jnp.mean(o * o, axis=-1, keepdims=True) + LN_EPS) * nw
            o_ref[r0:r0 + cl, h * HEAD_DIM:(h + 1) * HEAD_DIM] = (o * _silu(z)).astype(o_ref.dtype)

    for h in range(H_MIX):
        s_ref[h] = states[h]

    @pl.when(t == pl.num_programs(1) - 1)
    def _():
        s_out_ref[0] = s_ref[...]


def _gdn_prompt(proj, gates, conv_w, alog_row, dtb_row, nw_row, *, bsz, seqlen, tb):
    cl = math.gcd(seqlen, GDN_CHUNK)
    nt = seqlen // tb
    return pl.pallas_call(
        functools.partial(_gdn_prompt_kernel, tb, cl),
        grid=(bsz, nt),
        in_specs=[pl.BlockSpec((tb, 4 * D_MIX), lambda b, t: (b * nt + t, 0)),
                  pl.BlockSpec((tb, LANES), lambda b, t: (b * nt + t, 0)),
                  pl.BlockSpec((CONV_W, 3 * D_MIX), lambda b, t: (0, 0)),
                  pl.BlockSpec((1, LANES), lambda b, t: (0, 0)),
                  pl.BlockSpec((1, LANES), lambda b, t: (0, 0)),
                  pl.BlockSpec((1, HEAD_DIM), lambda b, t: (0, 0))],
        out_specs=[pl.BlockSpec((tb, D_MIX), lambda b, t: (b * nt + t, 0)),
                   pl.BlockSpec((1, H_MIX, HEAD_DIM, HEAD_DIM), lambda b, t: (b, 0, 0, 0))],
        out_shape=[jax.ShapeDtypeStruct((bsz * seqlen, D_MIX), BF16),
                   jax.ShapeDtypeStruct((bsz, H_MIX, HEAD_DIM, HEAD_DIM), F32)],
        scratch_shapes=[pltpu.VMEM((tb + SUBLANES, 3 * D_MIX), F32),
                        pltpu.VMEM((H_MIX, HEAD_DIM, HEAD_DIM), F32)],
        compiler_params=_cparams("parallel", "arbitrary"),
        name="gdn_prompt",
    )(proj, gates, conv_w, alog_row, dtb_row, nw_row)


def _mlstm_prompt_kernel(tb, cl, x_ref, g_ref, gbi_ref, gbf_ref,
                         o_ref, c_out_ref, n_out_ref, m_out_ref, c_ref, n_ref, m_ref):
    t = pl.program_id(1)

    @pl.when(t == 0)
    def _():
        c_ref[...] = jnp.zeros_like(c_ref)
        n_ref[...] = jnp.zeros_like(n_ref)
        m_ref[...] = jnp.zeros_like(m_ref)

    gates = g_ref[...]
    li_all = gates + gbi_ref[...]
    lf_all = _log_sigmoid(gates + gbf_ref[...])
    tri, _ = _tri_masks(cl)
    tril_bf = jnp.where(tri, 1.0, 0.0).astype(BF16)
    cs = [c_ref[h] for h in range(H_MIX)]
    ns = [n_ref[h:h + 1, :] for h in range(H_MIX)]
    ms = [m_ref[h:h + 1, 0:1] for h in range(H_MIX)]

    for c in range(tb // cl):
        r0 = c * cl
        bcum = _cumsum_rows(tril_bf, lf_all[r0:r0 + cl, :])
        bcum_t = bcum.T
        li_t = li_all[r0:r0 + cl, :].T
        for h in range(H_MIX):
            q = x_ref[r0:r0 + cl, h * HEAD_DIM:(h + 1) * HEAD_DIM]
            k = x_ref[r0:r0 + cl, D_MIX + h * HEAD_DIM:D_MIX + (h + 1) * HEAD_DIM] * (HEAD_DIM ** -0.5)
            v = x_ref[r0:r0 + cl, 2 * D_MIX + h * HEAD_DIM:2 * D_MIX + (h + 1) * HEAD_DIM]
            og = x_ref[r0:r0 + cl, 3 * D_MIX + h * HEAD_DIM:3 * D_MIX + (h + 1) * HEAD_DIM]
            b = bcum[:, GATE_B_F + h:GATE_B_F + h + 1]
            br = bcum_t[GATE_B_F + h:GATE_B_F + h + 1, :]
            ic = li_all[r0:r0 + cl, GATE_B_I + h:GATE_B_I + h + 1]
            ir = li_t[GATE_B_I + h:GATE_B_I + h + 1, :]
            m_prev = ms[h]
            dlog = jnp.where(tri, b - br + ir, NEG_BIG)
            inter = b + m_prev
            m = jnp.maximum(jnp.max(dlog, axis=1, keepdims=True), inter)
            s = _dot_nt(q, k) * jnp.exp(dlog - m)
            scale_prev = jnp.exp(inter - m)
            num = _dot(s, v) + scale_prev * _dot(q, cs[h])
            den = jnp.sum(s, axis=1, keepdims=True) + scale_prev * jnp.sum(q * ns[h], axis=1, keepdims=True)
            hh = num / jnp.maximum(jnp.abs(den), jnp.exp(-m))
            b_end = b[cl - 1:cl, :]
            wlog = b_end - b + ic
            m_new = jnp.maximum(b_end + m_prev, jnp.max(wlog, axis=0, keepdims=True))
            wk = jnp.exp(wlog - m_new) * k
            dec = jnp.exp(b_end + m_prev - m_new)
            cs[h] = dec * cs[h] + _dot_tn(wk, v)
            ns[h] = dec * ns[h] + jnp.sum(wk, axis=0, keepdims=True)
            ms[h] = m_new
            o_ref[r0:r0 + cl, h * HEAD_DIM:(h + 1) * HEAD_DIM] = (_sigmoid(og) * hh).astype(o_ref.dtype)

    for h in range(H_MIX):
        c_ref[h] = cs[h]
        n_ref[h:h + 1, :] = ns[h]
        m_ref[h:h + 1, :] = jnp.broadcast_to(ms[h], (1, LANES))

    @pl.when(t == pl.num_programs(1) - 1)
    def _():
        c_out_ref[0] = c_ref[...]
        n_out_ref[0] = n_ref[...]
        m_out_ref[0] = m_ref[...]


def _mlstm_prompt(proj, gates, gbi_row, gbf_row, *, bsz, seqlen, tb):
    cl = math.gcd(seqlen, MLSTM_CHUNK)
    nt = seqlen // tb
    return pl.pallas_call(
        functools.partial(_mlstm_prompt_kernel, tb, cl),
        grid=(bsz, nt),
        in_specs=[pl.BlockSpec((tb, 4 * D_MIX), lambda b, t: (b * nt + t, COL_B_Q // (4 * D_MIX))),
                  pl.BlockSpec((tb, LANES), lambda b, t: (b * nt + t, 0)),
                  pl.BlockSpec((1, LANES), lambda b, t: (0, 0)),
                  pl.BlockSpec((1, LANES), lambda b, t: (0, 0))],
        out_specs=[pl.BlockSpec((tb, D_MIX), lambda b, t: (b * nt + t, 0)),
                   pl.BlockSpec((1, H_MIX, HEAD_DIM, HEAD_DIM), lambda b, t: (b, 0, 0, 0)),
                   pl.BlockSpec((1, SUBLANES, HEAD_DIM), lambda b, t: (b, 0, 0)),
                   pl.BlockSpec((1, SUBLANES, LANES), lambda b, t: (b, 0, 0))],
        out_shape=[jax.ShapeDtypeStruct((bsz * seqlen, D_MIX), BF16),
                   jax.ShapeDtypeStruct((bsz, H_MIX, HEAD_DIM, HEAD_DIM), F32),
                   jax.ShapeDtypeStruct((bsz, SUBLANES, HEAD_DIM), F32),
                   jax.ShapeDtypeStruct((bsz, SUBLANES, LANES), F32)],
        scratch_shapes=[pltpu.VMEM((H_MIX, HEAD_DIM, HEAD_DIM), F32),
                        pltpu.VMEM((SUBLANES, HEAD_DIM), F32),
                        pltpu.VMEM((SUBLANES, LANES), F32)],
        compiler_params=_cparams("parallel", "arbitrary"),
        name="mlstm_prompt",
    )(proj, gates, gbi_row, gbf_row)


def _gmlp_gv(dv, nw):
    gv = _gelu_tanh(dv)
    mu = jnp.mean(gv, axis=-1, keepdims=True)
    gc = gv - mu
    var = jnp.mean(gc * gc, axis=-1, keepdims=True)
    return gc * lax.rsqrt(var + LN_EPS) * nw


def _gmlp_prompt_kernel(tb, u_ref, v_ref, nw_ref, ws_ref, bt_ref, o_ref):
    gu = _gelu_tanh(u_ref[...])
    gv = _gmlp_gv(v_ref[...], nw_ref[...])
    tri, _ = _tri_masks(GM_CHUNK)
    gw = D_MIX // GM_GROUPS
    for g in range(GM_GROUPS):
        wm = jnp.where(tri, ws_ref[g], 0.0)
        bias = bt_ref[:, g:g + 1]
        for c in range(tb // GM_CHUNK):
            r0 = c * GM_CHUNK
            z = _dot(wm, gv[r0:r0 + GM_CHUNK, g * gw:(g + 1) * gw]) + bias
            o_ref[r0:r0 + GM_CHUNK, g * gw:(g + 1) * gw] = (
                gu[r0:r0 + GM_CHUNK, g * gw:(g + 1) * gw] * z).astype(o_ref.dtype)


def _gmlp_prompt(proj, nw_row, ws, b_t, *, rows, tb):
    return pl.pallas_call(
        functools.partial(_gmlp_prompt_kernel, tb),
        grid=(rows // tb,),
        in_specs=[pl.BlockSpec((tb, D_MIX), lambda i: (i, COL_D_U // D_MIX)),
                  pl.BlockSpec((tb, D_MIX), lambda i: (i, COL_D_V // D_MIX)),
                  pl.BlockSpec((1, D_MIX), lambda i: (0, 0)),
                  pl.BlockSpec((GM_GROUPS, GM_CHUNK, GM_CHUNK), lambda i: (0, 0, 0)),
                  pl.BlockSpec((GM_CHUNK, GM_GROUPS), lambda i: (0, 0))],
        out_specs=pl.BlockSpec((tb, D_MIX), lambda i: (i, 0)),
        out_shape=jax.ShapeDtypeStruct((rows, D_MIX), BF16),
        compiler_params=_cparams("parallel"),
        name="gmlp_prompt",
    )(proj, proj, nw_row, ws, b_t)


KMEAN_ROWS = 128


def _rope_tables(pos):
    half = ROT_DIM // 2
    inv_freq = ROPE_THETA ** (-jnp.arange(half, dtype=F32) * (2.0 / ROT_DIM))
    ang = pos.astype(F32)[:, None] * inv_freq[None, :]
    cos, sin = jnp.cos(ang), jnp.sin(ang)
    rest = jnp.ones((pos.shape[0], HEAD_DIM - ROT_DIM), F32)
    return (jnp.concatenate([cos, cos, rest], axis=1),
            jnp.concatenate([-sin, sin, 0.0 * rest], axis=1))


def _rope(x, cos, sin):
    lane = lax.broadcasted_iota(jnp.int32, (x.shape[0], HEAD_DIM), 1)
    first_half = lane < ROT_DIM // 2
    outs = []
    for h in range(H_MIX):
        xh = x[:, h * HEAD_DIM:(h + 1) * HEAD_DIM]
        rot = jnp.where(first_half, pltpu.roll(xh, HEAD_DIM - ROT_DIM // 2, 1), pltpu.roll(xh, ROT_DIM // 2, 1))
        outs.append(xh * cos + rot * sin)
    return jnp.concatenate(outs, axis=1)


def _rope_kernel(q_ref, k_ref, v_ref, cos_ref, sin_ref, qo_ref, ko_ref, kb_ref, vb_ref, km_ref):
    t = pl.program_id(1)
    cos = cos_ref[...]
    sin = sin_ref[...]
    qo_ref[...] = _rope(q_ref[...], cos, sin)
    kr = _rope(k_ref[...], cos, sin)
    ko_ref[...] = kr
    kb_ref[...] = kr.astype(BF16)
    vb_ref[...] = v_ref[...].astype(BF16)

    @pl.when(t == 0)
    def _():
        km_ref[...] = jnp.zeros_like(km_ref)

    km_ref[0, pl.ds(t, 1), :] = jnp.mean(kr, axis=0, keepdims=True)


def _rope_prompt(proj, cos_tab, sin_tab, *, bsz, seqlen):
    nb = seqlen // MOBA_BLOCK
    col = lambda c: pl.BlockSpec((MOBA_BLOCK, D_MIX), lambda b, t: (b * nb + t, c // D_MIX))
    row = pl.BlockSpec((MOBA_BLOCK, D_MIX), lambda b, t: (b * nb + t, 0))
    tab = pl.BlockSpec((MOBA_BLOCK, HEAD_DIM), lambda b, t: (t, 0))
    return pl.pallas_call(
        _rope_kernel,
        grid=(bsz, nb),
        in_specs=[col(COL_C_Q), col(COL_C_K), col(COL_C_V), tab, tab],
        out_specs=[row, row, row, row, pl.BlockSpec((1, KMEAN_ROWS, D_MIX), lambda b, t: (b, 0, 0))],
        out_shape=[jax.ShapeDtypeStruct((bsz * seqlen, D_MIX), F32),
                   jax.ShapeDtypeStruct((bsz * seqlen, D_MIX), F32),
                   jax.ShapeDtypeStruct((bsz * seqlen, D_MIX), BF16),
                   jax.ShapeDtypeStruct((bsz * seqlen, D_MIX), BF16),
                   jax.ShapeDtypeStruct((bsz, KMEAN_ROWS, D_MIX), F32)],
        compiler_params=_cparams("parallel", "arbitrary"),
        name="rope_prompt",
    )(proj, proj, proj, cos_tab, sin_tab)


def _topk_block_rows(gate_t, n_valid):
    row = lax.broadcasted_iota(jnp.int32, gate_t.shape, 0)
    neg_inf = float("-inf")
    g = jnp.where(row < n_valid, gate_t, neg_inf)
    sel = jnp.zeros(gate_t.shape, F32)
    for _ in range(MOBA_TOPK):
        mx = jnp.max(g, axis=0, keepdims=True)
        first = jnp.min(jnp.where(g == mx, row, KMEAN_ROWS), axis=0, keepdims=True)
        pick = (row == first) & (mx > neg_inf)
        sel = jnp.where(pick, 1.0, sel)
        g = jnp.where(pick, neg_inf, g)
    return sel


def _moba_prompt_kernel(nb_pad, q_ref, k_ref, v_ref, km_ref, o_ref):
    qt = pl.program_id(1)
    blk = MOBA_BLOCK
    span = 2 * blk
    tri, _ = _tri_masks(blk)
    own0 = pl.multiple_of(qt * blk, blk)
    heads = [slice(h * HEAD_DIM, (h + 1) * HEAD_DIM) for h in range(H_MIX)]

    qs, sels, carry = [], [], []
    for hs in heads:
        q = q_ref[:, hs]
        sel_t = _topk_block_rows(_dot_nt_hi(km_ref[0][0:nb_pad, hs], q), qt)
        sel_t = jnp.concatenate([sel_t, jnp.zeros((KMEAN_ROWS - nb_pad, blk), F32)], axis=0)
        sels.append(sel_t.T.astype(BF16))
        qh = (q * (HEAD_DIM ** -0.5)).astype(BF16)
        qs.append(qh)
        s = jnp.where(tri, _dot_nt(qh, k_ref[pl.ds(own0, blk), hs]), NEG_BIG)
        m = jnp.max(s, axis=1, keepdims=True)
        p = jnp.exp(s - m)
        carry += [m, jnp.sum(p, axis=1, keepdims=True), _dot(p, v_ref[pl.ds(own0, blk), hs])]

    def body(j, carry):
        r0 = pl.multiple_of(j * span, span)
        key_block = 2 * j + lax.broadcasted_iota(jnp.int32, (KMEAN_ROWS, span), 1) // blk
        onehot = jnp.where(lax.broadcasted_iota(jnp.int32, (KMEAN_ROWS, span), 0) == key_block, 1.0, 0.0).astype(BF16)
        out = []
        for h, hs in enumerate(heads):
            m, l, acc = carry[3 * h:3 * h + 3]
            picked = jnp.dot(sels[h], onehot, preferred_element_type=F32) > 0.5
            s = jnp.where(picked, _dot_nt(qs[h], k_ref[pl.ds(r0, span), hs]), NEG_BIG)
            m_new = jnp.maximum(m, jnp.max(s, axis=1, keepdims=True))
            alpha = jnp.exp(m - m_new)
            p = jnp.where(picked, jnp.exp(s - m_new), 0.0)
            out += [m_new, alpha * l + jnp.sum(p, axis=1, keepdims=True),
                    alpha * acc + _dot(p, v_ref[pl.ds(r0, span), hs])]
        return tuple(out)

    carry = lax.fori_loop(0, (qt + 1) // 2, body, tuple(carry))
    for h, hs in enumerate(heads):
        o_ref[:, hs] = (carry[3 * h + 2] / carry[3 * h + 1]).astype(o_ref.dtype)


def _moba_prompt(q_rope, k_bf, v_bf, kmean, *, bsz, seqlen):
    nb = seqlen // MOBA_BLOCK
    nb_pad = -(-nb // SUBLANES) * SUBLANES
    assert nb_pad <= KMEAN_ROWS
    return pl.pallas_call(
        functools.partial(_moba_prompt_kernel, nb_pad),
        grid=(bsz, nb),
        in_specs=[pl.BlockSpec((MOBA_BLOCK, D_MIX), lambda b, t: (b * nb + t, 0)),
                  pl.BlockSpec((seqlen, D_MIX), lambda b, t: (b, 0)),
                  pl.BlockSpec((seqlen, D_MIX), lambda b, t: (b, 0)),
                  pl.BlockSpec((1, KMEAN_ROWS, D_MIX), lambda b, t: (b, 0, 0))],
        out_specs=pl.BlockSpec((MOBA_BLOCK, D_MIX), lambda b, t: (b * nb + t, 0)),
        out_shape=jax.ShapeDtypeStruct((bsz * seqlen, D_MIX), BF16),
        compiler_params=_cparams("parallel", "arbitrary"),
        name="moba_prompt",
    )(q_rope, k_bf, v_bf, kmean)


SAMPLE_GROUP = SUBLANES


def _columns(rows):
    pad = jnp.zeros((HEAD_DIM - SAMPLE_GROUP, HEAD_DIM), F32)
    return jnp.concatenate([rows, pad], axis=0).T


def _sample_state_kernel(p_ref, g_ref, conv_ref, s_ref, c_ref, n_ref, m_ref,
                         cw_ref, alog_ref, dtb_ref, nw_ref, gbi_ref, gbf_ref,
                         gmnw_ref, gmw0_ref, gmb0_ref, cos_ref, sin_ref,
                         oa_ref, ob_ref, od_ref, convo_ref, so_ref, co_ref, no_ref, mo_ref,
                         qr_ref, kr_ref, gv_ref):
    bg = SAMPLE_GROUP
    gates = g_ref[...]

    cw = cw_ref[...]
    xa = p_ref[:, COL_A_Q:COL_A_Q + 3 * D_MIX]
    y = cw[CONV_W - 1:CONV_W] * xa
    for j in range(CONV_W - 1):
        y = y + cw[j:j + 1] * conv_ref[j]
    y = _silu(y)
    for j in range(CONV_W - 2):
        convo_ref[j] = conv_ref[j + 1]
    convo_ref[CONV_W - 2] = xa
    beta_all = _sigmoid(gates)
    eg_all = jnp.exp(-jnp.exp(alog_ref[...]) * _softplus(gates + dtb_ref[...]))
    nw = nw_ref[...]
    for h in range(H_MIX):
        hs = slice(h * HEAD_DIM, (h + 1) * HEAD_DIM)
        q = y[:, COL_A_Q + h * HEAD_DIM:COL_A_Q + (h + 1) * HEAD_DIM]
        k = y[:, COL_A_K + h * HEAD_DIM:COL_A_K + (h + 1) * HEAD_DIM]
        v = y[:, COL_A_V + h * HEAD_DIM:COL_A_V + (h + 1) * HEAD_DIM]
        z = p_ref[:, COL_A_Z + h * HEAD_DIM:COL_A_Z + (h + 1) * HEAD_DIM]
        q = q * lax.rsqrt(jnp.sum(q * q, axis=-1, keepdims=True) + 1e-6) * (HEAD_DIM ** -0.5)
        k = k * lax.rsqrt(jnp.sum(k * k, axis=-1, keepdims=True) + 1e-6)
        beta = beta_all[:, GATE_A_BETA + h:GATE_A_BETA + h + 1]
        eg = eg_all[:, GATE_A_DEC + h:GATE_A_DEC + h + 1]
        qk = jnp.sum(q * k, axis=-1, keepdims=True)
        kt = _columns(k)
        qt = _columns(q)
        for i in range(bg):
            s = s_ref[i, h]
            kcol = kt[:, i:i + 1]
            e_i = eg[i:i + 1, :]
            ks = jnp.sum(kcol * s, axis=0, keepdims=True)
            qs = jnp.sum(qt[:, i:i + 1] * s, axis=0, keepdims=True)
            v_new = beta[i:i + 1, :] * (v[i:i + 1, :] - e_i * ks)
            oa_ref[i:i + 1, hs] = e_i * qs + qk[i:i + 1, :] * v_new
            so_ref[i, h] = e_i * s + kcol * v_new
        o = oa_ref[:, hs]
        o = o * lax.rsqrt(jnp.mean(o * o, axis=-1, keepdims=True) + LN_EPS) * nw
        oa_ref[:, hs] = o * _silu(z)

    li_all = gates + gbi_ref[...]
    lf_all = _log_sigmoid(gates + gbf_ref[...])
    for h in range(H_MIX):
        hs = slice(h * HEAD_DIM, (h + 1) * HEAD_DIM)
        q = p_ref[:, COL_B_Q + h * HEAD_DIM:COL_B_Q + (h + 1) * HEAD_DIM]
        k = p_ref[:, COL_B_K + h * HEAD_DIM:COL_B_K + (h + 1) * HEAD_DIM] * (HEAD_DIM ** -0.5)
        v = p_ref[:, COL_B_V + h * HEAD_DIM:COL_B_V + (h + 1) * HEAD_DIM]
        og = p_ref[:, COL_B_O + h * HEAD_DIM:COL_B_O + (h + 1) * HEAD_DIM]
        li = li_all[:, GATE_B_I + h:GATE_B_I + h + 1]
        lf = lf_all[:, GATE_B_F + h:GATE_B_F + h + 1]
        m_prev = m_ref[:, h:h + 1]
        n_prev = n_ref[:, hs]
        m_new = jnp.maximum(lf + m_prev, li)
        w_in = jnp.exp(li - m_new)
        dec = jnp.exp(lf + m_prev - m_new)
        sc = jnp.sum(q * k, axis=-1, keepdims=True) * w_in
        den = sc + dec * jnp.sum(q * n_prev, axis=-1, keepdims=True)
        denom = jnp.maximum(jnp.abs(den), jnp.exp(-m_new))
        no_ref[:, hs] = dec * n_prev + w_in * k
        mo_ref[:, h:h + 1] = m_new
        kt = _columns(k)
        qt = _columns(q)
        wv = w_in * v
        for i in range(bg):
            c = c_ref[i, h]
            d_i = dec[i:i + 1, :]
            qc = jnp.sum(qt[:, i:i + 1] * c, axis=0, keepdims=True)
            ob_ref[i:i + 1, hs] = sc[i:i + 1, :] * v[i:i + 1, :] + d_i * qc
            co_ref[i, h] = d_i * c + kt[:, i:i + 1] * wv[i:i + 1, :]
        ob_ref[:, hs] = _sigmoid(og) * (ob_ref[:, hs] / denom)

    gu = _gelu_tanh(p_ref[:, COL_D_U:COL_D_U + D_MIX])
    gv = _gmlp_gv(p_ref[:, COL_D_V:COL_D_V + D_MIX], gmnw_ref[...])
    gv_ref[...] = gv
    od_ref[...] = gu * (gmw0_ref[...] * gv + gmb0_ref[...])

    cos = cos_ref[...]
    sin = sin_ref[...]
    qr_ref[...] = _rope(p_ref[:, COL_C_Q:COL_C_Q + D_MIX], cos, sin)
    kr_ref[...] = _rope(p_ref[:, COL_C_K:COL_C_K + D_MIX], cos, sin)


def _sample_state(proj, gates, conv_t, s0, c0, n0, m0, conv_w, alog_row, dtb_row, nw_row, gbi_row, gbf_row,
                  gm_nw_row, gm_w0_row, gm_b0_row, cos_row, sin_row):
    bg = SAMPLE_GROUP
    nrows = proj.shape[0]
    full = lambda shape: pl.BlockSpec(shape, lambda i: (0,) * len(shape))
    row_blk = lambda w: pl.BlockSpec((bg, w), lambda i: (i, 0))
    mat_blk = pl.BlockSpec((bg, H_MIX, HEAD_DIM, HEAD_DIM), lambda i: (i, 0, 0, 0))
    conv_blk = pl.BlockSpec((CONV_W - 1, bg, 3 * D_MIX), lambda i: (0, i, 0))
    f = lambda shape: jax.ShapeDtypeStruct(shape, F32)
    return pl.pallas_call(
        _sample_state_kernel,
        grid=(nrows // bg,),
        in_specs=[row_blk(N_MAIN), row_blk(LANES),
                  conv_blk, mat_blk, mat_blk, row_blk(D_MIX), row_blk(H_MIX),
                  full((CONV_W, 3 * D_MIX)), full((1, LANES)), full((1, LANES)), full((1, HEAD_DIM)),
                  full((1, LANES)), full((1, LANES)),
                  full((1, D_MIX)), full((1, D_MIX)), full((1, D_MIX)),
                  full((1, HEAD_DIM)), full((1, HEAD_DIM))],
        out_specs=[row_blk(D_MIX), row_blk(D_MIX), row_blk(D_MIX), conv_blk, mat_blk, mat_blk,
                   row_blk(D_MIX), row_blk(H_MIX), row_blk(D_MIX), row_blk(D_MIX), row_blk(D_MIX)],
        out_shape=[f((nrows, D_MIX)), f((nrows, D_MIX)), f((nrows, D_MIX)),
                   f((CONV_W - 1, nrows, 3 * D_MIX)),
                   f((nrows, H_MIX, HEAD_DIM, HEAD_DIM)), f((nrows, H_MIX, HEAD_DIM, HEAD_DIM)),
                   f((nrows, D_MIX)), f((nrows, H_MIX)),
                   f((nrows, D_MIX)), f((nrows, D_MIX)), f((nrows, D_MIX))],
        compiler_params=_cparams("parallel"),
        name="sample_state",
    )(proj, gates, conv_t, s0, c0, n0, m0, conv_w, alog_row, dtb_row, nw_row, gbi_row, gbf_row,
      gm_nw_row, gm_w0_row, gm_b0_row, cos_row, sin_row)


def _head_rows(row):
    sub = lax.broadcasted_iota(jnp.int32, (SUBLANES, HEAD_DIM), 0)
    out = jnp.zeros((SUBLANES, HEAD_DIM), F32)
    for h in range(H_MIX):
        out = jnp.where(sub == h, jnp.broadcast_to(row[:, h * HEAD_DIM:(h + 1) * HEAD_DIM], (SUBLANES, HEAD_DIM)), out)
    return out


def _moba_decode_kernel(n_pages, page_rows, pt_ref, q_ref, kn_ref, vn_ref, *refs):
    del pt_ref
    k_refs = refs[:n_pages]
    v_refs = refs[n_pages:2 * n_pages]
    o_ref = refs[2 * n_pages]
    page_size = page_rows // H_MIX
    pages_per_block = MOBA_BLOCK // page_size
    n_blocks = n_pages // pages_per_block
    scale = HEAD_DIM ** -0.5

    q8 = _head_rows(q_ref[0])
    q8_bf = q8.astype(BF16)
    sub = lax.broadcasted_iota(jnp.int32, (SUBLANES, page_rows), 0)
    col = lax.broadcasted_iota(jnp.int32, (SUBLANES, page_rows), 1)
    head_mask = (col % H_MIX) == sub
    sub_d = lax.broadcasted_iota(jnp.int32, (SUBLANES, HEAD_DIM), 0)

    gates = []
    for j in range(n_blocks):
        fold = jnp.zeros((SUBLANES, HEAD_DIM), F32)
        for p in range(j * pages_per_block, (j + 1) * pages_per_block):
            fold = fold + jnp.sum(k_refs[p][0, 0].reshape(page_rows // SUBLANES, SUBLANES, HEAD_DIM), axis=0)
        ksum = fold
        for g in range(1, SUBLANES // H_MIX):
            ksum = ksum + pltpu.roll(fold, SUBLANES - g * H_MIX, 0)
        ksum = jnp.where(sub_d < H_MIX, ksum, 0.0)
        gates.append(jnp.sum(q8 * ksum, axis=1, keepdims=True) * (1.0 / MOBA_BLOCK))
    sels = []
    for n in range(n_blocks):
        rank = jnp.zeros((SUBLANES, 1), F32)
        for m in range(n_blocks):
            if m == n:
                continue
            ahead = gates[m] > gates[n]
            if m < n:
                ahead = ahead | (gates[m] == gates[n])
            rank = rank + jnp.where(ahead, 1.0, 0.0)
        sels.append(rank < float(MOBA_TOPK))

    keeps = [head_mask & sels[p // pages_per_block] for p in range(n_pages)]
    scores = [jnp.where(keeps[p],
                        lax.dot_general(q8_bf, k_refs[p][0, 0].astype(BF16), (((1,), (1,)), ((), ())),
                                        preferred_element_type=F32) * scale, NEG_BIG)
              for p in range(n_pages)]
    s_own = jnp.sum(q8 * _head_rows(kn_ref[0]), axis=1, keepdims=True) * scale
    m_all = s_own
    for s in scores:
        m_all = jnp.maximum(m_all, jnp.max(s, axis=1, keepdims=True))
    probs = [jnp.where(keeps[p], jnp.exp(scores[p] - m_all), 0.0) for p in range(n_pages)]
    e_own = jnp.exp(s_own - m_all)
    l_all = e_own
    acc = e_own * _head_rows(vn_ref[0])
    for p in range(n_pages):
        l_all = l_all + jnp.sum(probs[p], axis=1, keepdims=True)
        acc = acc + _dot(probs[p], v_refs[p][0, 0])
    out = acc / l_all
    for h in range(H_MIX):
        o_ref[0, :, h * HEAD_DIM:(h + 1) * HEAD_DIM] = out[h:h + 1, :]


def _moba_decode(page_table, q_rope, k_new, v_new, cache_k, cache_v, *, layer):
    bsz, n_pages = page_table.shape
    page_rows = cache_k.shape[2]
    page_size = page_rows // H_MIX
    assert MOBA_BLOCK % page_size == 0 and (n_pages * page_size) % MOBA_BLOCK == 0
    assert (n_pages * page_size) // MOBA_BLOCK >= MOBA_TOPK and SUBLANES % H_MIX == 0
    row = pl.BlockSpec((1, 1, D_MIX), lambda b, pt: (b, 0, 0))
    page_specs = [pl.BlockSpec((1, 1, page_rows, HEAD_DIM), lambda b, pt, p=p: (layer, pt[b, p], 0, 0))
                  for p in range(n_pages)]
    return pl.pallas_call(
        functools.partial(_moba_decode_kernel, n_pages, page_rows),
        grid_spec=pltpu.PrefetchScalarGridSpec(
            num_scalar_prefetch=1,
            grid=(bsz,),
            in_specs=[row, row, row] + page_specs + page_specs,
            out_specs=row),
        out_shape=jax.ShapeDtypeStruct((bsz, 1, D_MIX), F32),
        compiler_params=_cparams("parallel"),
        name="moba_decode",
    )(page_table, q_rope, k_new, v_new, *([cache_k] * n_pages), *([cache_v] * n_pages))


def _largest_divisor(n, candidates):
    for c in candidates:
        if n % c == 0:
            return c
    raise ValueError(f"no tile in {candidates} divides {n}")


def _tiles(rows, seqlen):
    return dict(
        proj_tm=_largest_divisor(rows, (1024, 512, 256, 128)),
        proj_tn=512,
        out_tm=_largest_divisor(rows, (256, 128)),
        ffn_tm=_largest_divisor(rows, (512, 256, 128)),
        ffn_tf=512,
        scan_tb=_largest_divisor(seqlen, (256, 128, 64, 32, 16, 8)),
        gmlp_tb=_largest_divisor(seqlen, (512, 256, 128)),
    )


def _lane_row(vals, offset):
    return jnp.zeros((1, LANES), F32).at[0, offset:offset + vals.shape[0]].set(vals.astype(F32))


def kernel(x_prompt, x_sample, state_gdn_conv, state_gdn_s, state_mlstm_c, state_mlstm_n, state_mlstm_m,
           cache_k, cache_v, page_table, w_in, gdn_conv_w, gdn_a_log, gdn_dt_bias, gdn_norm_w, mlstm_gate_b,
           gmlp_norm_w, gmlp_ws, gmlp_b, w_out, ln1_w, ln1_b, w_up, w_down, ln2_w, ln2_b):
    bsz, seq, d_model = x_prompt.shape
    dec_b, dec_s, _ = x_sample.shape
    depth = w_in.shape[0]
    assert dec_s == 1 and d_model == N_MIXERS * D_MIX and seq >= CONV_W - 1
    assert seq % MOBA_BLOCK == 0 and seq // MOBA_BLOCK <= KMEAN_ROWS and seq % GM_CHUNK == 0
    assert w_in.shape[2] == N_MAIN + 4 * H_MIX and cache_k.shape[3] == H_MIX
    alpha = (2.0 * depth) ** 0.25
    mp = bsz * seq
    past_len = page_table.shape[1] * cache_k.shape[2]
    n_pool, page_size = cache_k.shape[1], cache_k.shape[2]
    tp = _tiles(mp, seq)
    ts = _tiles(dec_b, seq)
    gw = D_MIX // GM_GROUPS

    cos_p, sin_p = _rope_tables(jnp.arange(seq, dtype=jnp.int32))
    cos_s, sin_s = _rope_tables(past_len + jnp.arange(dec_s, dtype=jnp.int32))
    ck = cache_k.reshape(depth, n_pool, page_size * H_MIX, HEAD_DIM)
    cv = cache_v.reshape(depth, n_pool, page_size * H_MIX, HEAD_DIM)

    a_end = 4 * D_MIX
    b0 = a_end + 2 * H_MIX
    b_end = b0 + 4 * D_MIX
    c0 = b_end + 2 * H_MIX

    xp = x_prompt.reshape(mp, d_model)
    xs = x_sample.reshape(dec_b, d_model)
    xp_in, xs_in = xp, xs
    p_st, s_st = [], []
    for l in range(depth):
        w = w_in[l]
        w_main = jnp.concatenate([w[:, :a_end], w[:, b0:b_end], w[:, c0:]], axis=1).astype(BF16)
        w_gate = jnp.concatenate([w[:, a_end:b0], w[:, b_end:c0],
                                  jnp.zeros((d_model, LANES - 4 * H_MIX), F32)], axis=1).astype(BF16)
        alog_row = _lane_row(gdn_a_log[l], GATE_A_DEC)
        dtb_row = _lane_row(gdn_dt_bias[l], GATE_A_DEC)
        nw_row = gdn_norm_w[l].reshape(1, HEAD_DIM).astype(F32)
        gbi_row = _lane_row(mlstm_gate_b[l][:H_MIX], GATE_B_I)
        gbf_row = _lane_row(mlstm_gate_b[l][H_MIX:], GATE_B_F)
        gm_nw_row = gmlp_norm_w[l].reshape(1, D_MIX).astype(F32)
        conv_w = gdn_conv_w[l].astype(F32)
        w_out_bf = w_out[l].astype(BF16)
        w_up_bf = w_up[l].astype(BF16)
        w_down_bf = w_down[l].astype(BF16)
        ln1 = (ln1_w[l].reshape(1, d_model).astype(F32), ln1_b[l].reshape(1, d_model).astype(F32))
        ln2 = (ln2_w[l].reshape(1, d_model).astype(F32), ln2_b[l].reshape(1, d_model).astype(F32))

        proj, gates = _proj(xp_in, w_main, w_gate, tm=tp["proj_tm"], tn=tp["proj_tn"])
        oa, gdn_s_p = _gdn_prompt(proj, gates, conv_w, alog_row, dtb_row, nw_row,
                                  bsz=bsz, seqlen=seq, tb=tp["scan_tb"])
        ob, ml_c_p, ml_n_p, ml_m_p = _mlstm_prompt(proj, gates, gbi_row, gbf_row,
                                                   bsz=bsz, seqlen=seq, tb=tp["scan_tb"])
        q_rope, k_rope, k_bf, v_bf, kmean = _rope_prompt(proj, cos_p, sin_p, bsz=bsz, seqlen=seq)
        oc = _moba_prompt(q_rope, k_bf, v_bf, kmean, bsz=bsz, seqlen=seq)
        od = _gmlp_prompt(proj, gm_nw_row, gmlp_ws[l].astype(F32), gmlp_b[l].astype(F32).T,
                          rows=mp, tb=tp["gmlp_tb"])
        x1, x1_bf = _outproj_ln((oa, ob, oc, od), w_out_bf, xp, *ln1, alpha=alpha, tm=tp["out_tm"])
        xp, xp_in = _ffn_ln(x1_bf, w_up_bf, w_down_bf, x1, *ln2, alpha=alpha, tm=tp["ffn_tm"], tf=tp["ffn_tf"])
        proj_p = proj.reshape(bsz, seq, N_MAIN)
        p_st.append((proj_p[:, seq - (CONV_W - 1):, COL_A_Q:COL_A_Q + 3 * D_MIX],
                     gdn_s_p, ml_c_p, ml_n_p[:, :H_MIX], ml_m_p[:, :H_MIX, 0],
                     k_rope.reshape(bsz, seq, H_MIX, HEAD_DIM),
                     proj_p[:, :, COL_C_V:COL_C_V + D_MIX].reshape(bsz, seq, H_MIX, HEAD_DIM)))

        proj, gates = _proj(xs_in, w_main, w_gate, tm=ts["proj_tm"], tn=ts["proj_tn"])
        (oa, ob, od, conv_s, gdn_s_s, ml_c_s, ml_n_s, ml_m_s, q_s, k_s, gv_s) = _sample_state(
            proj, gates, jnp.transpose(state_gdn_conv[l].astype(F32), (1, 0, 2)),
            state_gdn_s[l].astype(F32), state_mlstm_c[l].astype(F32),
            state_mlstm_n[l].astype(F32).reshape(dec_b, D_MIX), state_mlstm_m[l].astype(F32),
            conv_w, alog_row, dtb_row, nw_row, gbi_row, gbf_row, gm_nw_row,
            jnp.repeat(gmlp_ws[l][:, 0, 0].astype(F32), gw).reshape(1, D_MIX),
            jnp.repeat(gmlp_b[l][:, 0].astype(F32), gw).reshape(1, D_MIX),
            cos_s, sin_s)
        v_s = proj[:, COL_C_V:COL_C_V + D_MIX]
        oc = _moba_decode(page_table, q_s.reshape(dec_b, 1, D_MIX), k_s.reshape(dec_b, 1, D_MIX),
                          v_s.reshape(dec_b, 1, D_MIX), ck, cv, layer=l).reshape(dec_b, D_MIX)
        x1, x1_bf = _outproj_ln((oa, ob, oc, od), w_out_bf, xs, *ln1, alpha=alpha, tm=ts["out_tm"])
        xs, xs_in = _ffn_ln(x1_bf, w_up_bf, w_down_bf, x1, *ln2, alpha=alpha, tm=ts["ffn_tm"], tf=ts["ffn_tf"])
        s_st.append((jnp.transpose(conv_s, (1, 0, 2)), gdn_s_s, ml_c_s,
                     ml_n_s.reshape(dec_b, H_MIX, HEAD_DIM), ml_m_s,
                     k_s.reshape(dec_b, dec_s, H_MIX, HEAD_DIM), v_s.reshape(dec_b, dec_s, H_MIX, HEAD_DIM),
                     gv_s.reshape(dec_b, dec_s, D_MIX)))

    def stk(sts, i):
        return jnp.stack([s[i] for s in sts], axis=0)

    dt = x_prompt.dtype
    yp = xp.reshape(bsz, seq, d_model).astype(dt)
    ys = xs.reshape(dec_b, dec_s, d_model).astype(dt)
    return (yp, ys) + tuple(stk(p_st, i).astype(dt) for i in range(7)) + tuple(stk(s_st, i).astype(dt) for i in range(8))
```

```python
import functools
import math

import jax
import jax.numpy as jnp
from jax import lax
from jax.experimental import pallas as pl
from jax.experimental.pallas import tpu as pltpu

F32 = jnp.float32
BF16 = jnp.bfloat16
HIGHEST = lax.Precision.HIGHEST

HEAD_DIM = 128
N_MIXERS = 4
CONV_W = 4
GDN_CHUNK = 64
MLSTM_CHUNK = 64
MOBA_BLOCK = 256
MOBA_TOPK = 3
GM_CHUNK = 128
GM_GROUPS = 4
ROPE_THETA = 500000.0
ROT_DIM = HEAD_DIM // 4
LN_EPS = 1e-5
NEG_BIG = -1e30

LANES = 128
SUBLANES = 8
VMEM_LIMIT_BYTES = 56 * 1024 * 1024


def _cparams(*sem):
    return pltpu.CompilerParams(dimension_semantics=sem, vmem_limit_bytes=VMEM_LIMIT_BYTES)


def _dot(a, b):
    return jnp.dot(a.astype(BF16), b.astype(BF16), preferred_element_type=F32)


def _dot_nt(a, b):
    return lax.dot_general(a.astype(BF16), b.astype(BF16), (((1,), (1,)), ((), ())),
                           preferred_element_type=F32)


def _dot_tn(a, b):
    return lax.dot_general(a.astype(BF16), b.astype(BF16), (((0,), (0,)), ((), ())),
                           preferred_element_type=F32)


def _dot_hi(a, b):
    return jnp.dot(a, b, precision=HIGHEST, preferred_element_type=F32)


def _dot_nt_hi(a, b):
    return lax.dot_general(a, b, (((1,), (1,)), ((), ())), precision=HIGHEST,
                           preferred_element_type=F32)


def _split2(x):
    hi = x.astype(BF16)
    return hi, (x - hi.astype(F32)).astype(BF16)


def _dot_split(a_parts, b_parts):
    ah, al = a_parts
    bh, bl = b_parts
    return (jnp.dot(ah, bh, preferred_element_type=F32) + jnp.dot(ah, bl, preferred_element_type=F32)
            + jnp.dot(al, bh, preferred_element_type=F32))


def _cumsum_rows(tril_bf, x):
    x0 = x.astype(BF16)
    r1 = x - x0.astype(F32)
    x1 = r1.astype(BF16)
    x2 = (r1 - x1.astype(F32)).astype(BF16)
    return (jnp.dot(tril_bf, x0, preferred_element_type=F32) + jnp.dot(tril_bf, x1, preferred_element_type=F32)
            + jnp.dot(tril_bf, x2, preferred_element_type=F32))


def _sigmoid(x):
    return 1.0 / (1.0 + jnp.exp(-x))


def _silu(x):
    return x * _sigmoid(x)


def _softplus(x):
    return jnp.maximum(x, 0.0) + jnp.log(1.0 + jnp.exp(-jnp.abs(x)))


def _log_sigmoid(x):
    return -_softplus(-x)


def _gelu_tanh(x):
    return 0.5 * x * (1.0 + jnp.tanh(math.sqrt(2.0 / math.pi) * (x + 0.044715 * (x * x * x))))


def _layer_norm(x, w, b):
    mu = jnp.mean(x, axis=-1, keepdims=True)
    xc = x - mu
    var = jnp.mean(xc * xc, axis=-1, keepdims=True)
    return xc * lax.rsqrt(var + LN_EPS) * w + b


D_MIX = 512
H_MIX = D_MIX // HEAD_DIM
COL_A_Q, COL_A_K, COL_A_V, COL_A_Z = 0, 512, 1024, 1536
COL_B_Q, COL_B_K, COL_B_V, COL_B_O = 2048, 2560, 3072, 3584
COL_C_Q, COL_C_K, COL_C_V = 4096, 4608, 5120
COL_D_U, COL_D_V = 5632, 6144
N_MAIN = 6656
GATE_A_BETA, GATE_A_DEC, GATE_B_I, GATE_B_F = 0, 4, 8, 12


def _tri_masks(n):
    r = lax.broadcasted_iota(jnp.int32, (n, n), 0)
    c = lax.broadcasted_iota(jnp.int32, (n, n), 1)
    return r >= c, r > c


def _proj_kernel(x_ref, w_ref, wg_ref, o_ref, g_ref, xb_ref):
    @pl.when(pl.program_id(1) == 0)
    def _():
        xb_ref[...] = x_ref[...].astype(BF16)
        g_ref[...] = jnp.dot(xb_ref[...], wg_ref[0], preferred_element_type=F32)

    o_ref[...] = jnp.dot(xb_ref[...], w_ref[0], preferred_element_type=F32)


def _proj(x, w_main, w_gate, *, layer, tm, tn):
    m, d = x.shape
    n = w_main.shape[2]
    return pl.pallas_call(
        _proj_kernel,
        grid=(m // tm, n // tn),
        in_specs=[pl.BlockSpec((tm, d), lambda i, j: (i, 0)),
                  pl.BlockSpec((1, d, tn), lambda i, j: (layer, 0, j)),
                  pl.BlockSpec((1, d, LANES), lambda i, j: (layer, 0, 0))],
        out_specs=[pl.BlockSpec((tm, tn), lambda i, j: (i, j)),
                   pl.BlockSpec((tm, LANES), lambda i, j: (i, 0))],
        out_shape=[jax.ShapeDtypeStruct((m, n), F32), jax.ShapeDtypeStruct((m, LANES), F32)],
        scratch_shapes=[pltpu.VMEM((tm, d), BF16)],
        compiler_params=_cparams("parallel", "arbitrary"),
        name="proj",
    )(x, w_main, w_gate)


def _outproj_ln_kernel(alpha, a_ref, b_ref, c_ref, d_ref, w_ref, x_ref, lw_ref, lb_ref, o_ref, ob_ref):
    y = alpha * x_ref[...]
    for i, m_ref in enumerate((a_ref, b_ref, c_ref, d_ref)):
        y = y + jnp.dot(m_ref[...].astype(BF16), w_ref[0, i * D_MIX:(i + 1) * D_MIX, :],
                        preferred_element_type=F32)
    y = _layer_norm(y, lw_ref[...], lb_ref[...])
    o_ref[...] = y
    ob_ref[...] = y.astype(BF16)


def _outproj_ln(mixes, w_out_bf, x, ln_w, ln_b, *, layer, alpha, tm):
    m, d = x.shape
    mix_spec = pl.BlockSpec((tm, D_MIX), lambda i: (i, 0))
    return pl.pallas_call(
        functools.partial(_outproj_ln_kernel, alpha),
        grid=(m // tm,),
        in_specs=[mix_spec, mix_spec, mix_spec, mix_spec,
                  pl.BlockSpec((1, d, d), lambda i: (layer, 0, 0)),
                  pl.BlockSpec((tm, d), lambda i: (i, 0)),
                  pl.BlockSpec((1, d), lambda i: (0, 0)),
                  pl.BlockSpec((1, d), lambda i: (0, 0))],
        out_specs=[pl.BlockSpec((tm, d), lambda i: (i, 0)),
                   pl.BlockSpec((tm, d), lambda i: (i, 0))],
        out_shape=[jax.ShapeDtypeStruct((m, d), F32), jax.ShapeDtypeStruct((m, d), BF16)],
        compiler_params=_cparams("parallel"),
        name="outproj_ln",
    )(*mixes, w_out_bf, x, ln_w, ln_b)


def _ffn_ln_kernel(alpha, xb_ref, wu_ref, wd_ref, x_ref, lw_ref, lb_ref, o_ref, ob_ref, acc_ref):
    j = pl.program_id(1)

    @pl.when(j == 0)
    def _():
        acc_ref[...] = alpha * x_ref[...]

    h = jnp.dot(xb_ref[...], wu_ref[0], preferred_element_type=F32)
    h = jnp.maximum(h, 0.0)
    h = (h * h).astype(BF16)
    acc_ref[...] += jnp.dot(h, wd_ref[0], preferred_element_type=F32)

    @pl.when(j == pl.num_programs(1) - 1)
    def _():
        y = _layer_norm(acc_ref[...], lw_ref[...], lb_ref[...])
        o_ref[...] = y
        ob_ref[...] = y.astype(BF16)


def _ffn_ln(x_bf, w_up_bf, w_down_bf, x, ln_w, ln_b, *, layer, alpha, tm, tf):
    m, d = x.shape
    f = w_up_bf.shape[2]
    return pl.pallas_call(
        functools.partial(_ffn_ln_kernel, alpha),
        grid=(m // tm, f // tf),
        in_specs=[pl.BlockSpec((tm, d), lambda i, j: (i, 0)),
                  pl.BlockSpec((1, d, tf), lambda i, j: (layer, 0, j)),
                  pl.BlockSpec((1, tf, d), lambda i, j: (layer, j, 0)),
                  pl.BlockSpec((tm, d), lambda i, j: (i, 0)),
                  pl.BlockSpec((1, d), lambda i, j: (0, 0)),
                  pl.BlockSpec((1, d), lambda i, j: (0, 0))],
        out_specs=[pl.BlockSpec((tm, d), lambda i, j: (i, 0)),
                   pl.BlockSpec((tm, d), lambda i, j: (i, 0))],
        out_shape=[jax.ShapeDtypeStruct((m, d), F32), jax.ShapeDtypeStruct((m, d), BF16)],
        scratch_shapes=[pltpu.VMEM((tm, d), F32)],
        compiler_params=_cparams("parallel", "arbitrary"),
        name="ffn_ln",
    )(x_bf, w_up_bf, w_down_bf, x, ln_w, ln_b)


def _unit_lower_inverses(mats):
    n = mats[0].shape[0]
    r = lax.broadcasted_iota(jnp.int32, (n, n), 0)
    c = lax.broadcasted_iota(jnp.int32, (n, n), 1)
    eye = jnp.where(r == c, 1.0, 0.0).astype(F32)
    ts = [eye - a for a in mats]
    ps = [_split2(a) for a in mats]
    span = 2
    while span < n:
        ps = [_split2(_dot_split(p, p)) for p in ps]
        ts = [t + _dot_split(_split2(t), p) for t, p in zip(ts, ps)]
        span *= 2
    return ts


def _gdn_prompt_kernel(tb, cl, x_ref, g_ref, cw_ref, alog_ref, dtb_ref, nw_ref,
                       o_ref, s_out_ref, xbuf, s_ref):
    t = pl.program_id(1)
    dqkv = 3 * D_MIX

    @pl.when(t == 0)
    def _():
        xbuf[0:SUBLANES, :] = jnp.zeros((SUBLANES, dqkv), F32)
        s_ref[...] = jnp.zeros_like(s_ref)

    @pl.when(t > 0)
    def _():
        xbuf[0:SUBLANES, :] = xbuf[tb:tb + SUBLANES, :]

    xbuf[SUBLANES:SUBLANES + tb, :] = x_ref[:, 0:dqkv]
    cw = cw_ref[...]
    y = cw[0:1] * xbuf[SUBLANES - 3:SUBLANES - 3 + tb, :]
    for j in range(1, CONV_W):
        y = y + cw[j:j + 1] * xbuf[SUBLANES - 3 + j:SUBLANES - 3 + j + tb, :]
    y = _silu(y)

    gates = g_ref[...]
    beta_all = _sigmoid(gates)
    g_all = -jnp.exp(alog_ref[...]) * _softplus(gates + dtb_ref[...])
    tri, tri_s = _tri_masks(cl)
    tril_bf = jnp.where(tri, 1.0, 0.0).astype(BF16)
    nw = nw_ref[...]
    states = [s_ref[h] for h in range(H_MIX)]

    a_mats, pre = [], []
    for c in range(tb // cl):
        r0 = c * cl
        gcum = _cumsum_rows(tril_bf, g_all[r0:r0 + cl, :])
        gcum_t = gcum.T
        for h in range(H_MIX):
            q = y[r0:r0 + cl, COL_A_Q + h * HEAD_DIM:COL_A_Q + (h + 1) * HEAD_DIM]
            k = y[r0:r0 + cl, COL_A_K + h * HEAD_DIM:COL_A_K + (h + 1) * HEAD_DIM]
            v = y[r0:r0 + cl, COL_A_V + h * HEAD_DIM:COL_A_V + (h + 1) * HEAD_DIM]
            q = q * lax.rsqrt(jnp.sum(q * q, axis=-1, keepdims=True) + 1e-6) * (HEAD_DIM ** -0.5)
            k = k * lax.rsqrt(jnp.sum(k * k, axis=-1, keepdims=True) + 1e-6)
            beta = beta_all[r0:r0 + cl, GATE_A_BETA + h:GATE_A_BETA + h + 1]
            gc = gcum[:, GATE_A_DEC + h:GATE_A_DEC + h + 1]
            gr = gcum_t[GATE_A_DEC + h:GATE_A_DEC + h + 1, :]
            decay = jnp.exp(jnp.where(tri, gc - gr, NEG_BIG))
            kb = k * beta
            egc = jnp.exp(gc)
            gc_last = gc[cl - 1:cl, :]
            kk = _dot_nt(jnp.concatenate([kb, q], axis=0), k)
            a_mats.append(jnp.where(tri_s, kk[0:cl] * decay, 0.0))
            pre.append(dict(attn=jnp.where(tri, kk[cl:2 * cl] * decay, 0.0),
                            rhs=jnp.concatenate([v * beta, kb * egc], axis=1),
                            q_dec=q * egc, k_tail=k * jnp.exp(gc_last - gc), g_tot=jnp.exp(gc_last)))
    t_invs = _unit_lower_inverses(a_mats)
    uws = [_dot(t_inv, p["rhs"]) for t_inv, p in zip(t_invs, pre)]

    for c in range(tb // cl):
        r0 = c * cl
        for h in range(H_MIX):
            p, uw = pre[c * H_MIX + h], uws[c * H_MIX + h]
            z = x_ref[r0:r0 + cl, COL_A_Z + h * HEAD_DIM:COL_A_Z + (h + 1) * HEAD_DIM]
            s = states[h]
            ws_qs = _dot(jnp.concatenate([uw[:, HEAD_DIM:], p["q_dec"]], axis=0), s)
            v_new = uw[:, :HEAD_DIM] - ws_qs[0:cl]
            o = ws_qs[cl:2 * cl] + _dot(p["attn"], v_new)
            states[h] = s * p["g_tot"] + _dot_tn(p["k_tail"], v_new)
            o = o * lax.rsqrt(jnp.mean(o * o, axis=-1, keepdims=True) + LN_EPS) * nw
            o_ref[r0:r0 + cl, h * HEAD_DIM:(h + 1) * HEAD_DIM] = (o * _silu(z)).astype(o_ref.dtype)

    for h in range(H_MIX):
        s_ref[h] = states[h]

    @pl.when(t == pl.num_programs(1) - 1)
    def _():
        s_out_ref[0] = s_ref[...]


def _gdn_prompt(proj, gates, conv_w, alog_row, dtb_row, nw_row, *, bsz, seqlen, tb):
    cl = math.gcd(seqlen, GDN_CHUNK)
    nt = seqlen // tb
    return pl.pallas_call(
        functools.partial(_gdn_prompt_kernel, tb, cl),
        grid=(bsz, nt),
        in_specs=[pl.BlockSpec((tb, 4 * D_MIX), lambda b, t: (b * nt + t, 0)),
                  pl.BlockSpec((tb, LANES), lambda b, t: (b * nt + t, 0)),
                  pl.BlockSpec((CONV_W, 3 * D_MIX), lambda b, t: (0, 0)),
                  pl.BlockSpec((1, LANES), lambda b, t: (0, 0)),
                  pl.BlockSpec((1, LANES), lambda b, t: (0, 0)),
                  pl.BlockSpec((1, HEAD_DIM), lambda b, t: (0, 0))],
        out_specs=[pl.BlockSpec((tb, D_MIX), lambda b, t: (b * nt + t, 0)),
                   pl.BlockSpec((1, H_MIX, HEAD_DIM, HEAD_DIM), lambda b, t: (b, 0, 0, 0))],
        out_shape=[jax.ShapeDtypeStruct((bsz * seqlen, D_MIX), BF16),
                   jax.ShapeDtypeStruct((bsz, H_MIX, HEAD_DIM, HEAD_DIM), F32)],
        scratch_shapes=[pltpu.VMEM((tb + SUBLANES, 3 * D_MIX), F32),
                        pltpu.VMEM((H_MIX, HEAD_DIM, HEAD_DIM), F32)],
        compiler_params=_cparams("parallel", "arbitrary"),
        name="gdn_prompt",
    )(proj, gates, conv_w, alog_row, dtb_row, nw_row)


def _mlstm_prompt_kernel(tb, cl, x_ref, g_ref, gbi_ref, gbf_ref,
                         o_ref, c_out_ref, n_out_ref, m_out_ref, c_ref, n_ref, m_ref):
    t = pl.program_id(1)

    @pl.when(t == 0)
    def _():
        c_ref[...] = jnp.zeros_like(c_ref)
        n_ref[...] = jnp.zeros_like(n_ref)
        m_ref[...] = jnp.zeros_like(m_ref)

    gates = g_ref[...]
    li_all = gates + gbi_ref[...]
    lf_all = _log_sigmoid(gates + gbf_ref[...])
    tri, _ = _tri_masks(cl)
    tril_bf = jnp.where(tri, 1.0, 0.0).astype(BF16)
    cs = [c_ref[h] for h in range(H_MIX)]
    ns = [n_ref[h:h + 1, :] for h in range(H_MIX)]
    ms = [m_ref[h:h + 1, 0:1] for h in range(H_MIX)]

    for c in range(tb // cl):
        r0 = c * cl
        bcum = _cumsum_rows(tril_bf, lf_all[r0:r0 + cl, :])
        bcum_t = bcum.T
        li_t = li_all[r0:r0 + cl, :].T
        for h in range(H_MIX):
            q = x_ref[r0:r0 + cl, h * HEAD_DIM:(h + 1) * HEAD_DIM]
            k = x_ref[r0:r0 + cl, D_MIX + h * HEAD_DIM:D_MIX + (h + 1) * HEAD_DIM] * (HEAD_DIM ** -0.5)
            v = x_ref[r0:r0 + cl, 2 * D_MIX + h * HEAD_DIM:2 * D_MIX + (h + 1) * HEAD_DIM]
            og = x_ref[r0:r0 + cl, 3 * D_MIX + h * HEAD_DIM:3 * D_MIX + (h + 1) * HEAD_DIM]
            b = bcum[:, GATE_B_F + h:GATE_B_F + h + 1]
            br = bcum_t[GATE_B_F + h:GATE_B_F + h + 1, :]
            ic = li_all[r0:r0 + cl, GATE_B_I + h:GATE_B_I + h + 1]
            ir = li_t[GATE_B_I + h:GATE_B_I + h + 1, :]
            m_prev = ms[h]
            dlog = jnp.where(tri, b - br + ir, NEG_BIG)
            inter = b + m_prev
            m = jnp.maximum(jnp.max(dlog, axis=1, keepdims=True), inter)
            s = _dot_nt(q, k) * jnp.exp(dlog - m)
            scale_prev = jnp.exp(inter - m)
            num = _dot(s, v) + scale_prev * _dot(q, cs[h])
            den = jnp.sum(s, axis=1, keepdims=True) + scale_prev * jnp.sum(q * ns[h], axis=1, keepdims=True)
            hh = num / jnp.maximum(jnp.abs(den), jnp.exp(-m))
            b_end = b[cl - 1:cl, :]
            wlog = b_end - b + ic
            m_new = jnp.maximum(b_end + m_prev, jnp.max(wlog, axis=0, keepdims=True))
            wk = jnp.exp(wlog - m_new) * k
            dec = jnp.exp(b_end + m_prev - m_new)
            cs[h] = dec * cs[h] + _dot_tn(wk, v)
            ns[h] = dec * ns[h] + jnp.sum(wk, axis=0, keepdims=True)
            ms[h] = m_new
            o_ref[r0:r0 + cl, h * HEAD_DIM:(h + 1) * HEAD_DIM] = (_sigmoid(og) * hh).astype(o_ref.dtype)

    for h in range(H_MIX):
        c_ref[h] = cs[h]
        n_ref[h:h + 1, :] = ns[h]
        m_ref[h:h + 1, :] = jnp.broadcast_to(ms[h], (1, LANES))

    @pl.when(t == pl.num_programs(1) - 1)
    def _():
        c_out_ref[0] = c_ref[...]
        n_out_ref[0] = n_ref[...]
        m_out_ref[0] = m_ref[...]


def _mlstm_prompt(proj, gates, gbi_row, gbf_row, *, bsz, seqlen, tb):
    cl = math.gcd(seqlen, MLSTM_CHUNK)
    nt = seqlen // tb
    return pl.pallas_call(
        functools.partial(_mlstm_prompt_kernel, tb, cl),
        grid=(bsz, nt),
        in_specs=[pl.BlockSpec((tb, 4 * D_MIX), lambda b, t: (b * nt + t, COL_B_Q // (4 * D_MIX))),
                  pl.BlockSpec((tb, LANES), lambda b, t: (b * nt + t, 0)),
                  pl.BlockSpec((1, LANES), lambda b, t: (0, 0)),
                  pl.BlockSpec((1, LANES), lambda b, t: (0, 0))],
        out_specs=[pl.BlockSpec((tb, D_MIX), lambda b, t: (b * nt + t, 0)),
                   pl.BlockSpec((1, H_MIX, HEAD_DIM, HEAD_DIM), lambda b, t: (b, 0, 0, 0)),
                   pl.BlockSpec((1, SUBLANES, HEAD_DIM), lambda b, t: (b, 0, 0)),
                   pl.BlockSpec((1, SUBLANES, LANES), lambda b, t: (b, 0, 0))],
        out_shape=[jax.ShapeDtypeStruct((bsz * seqlen, D_MIX), BF16),
                   jax.ShapeDtypeStruct((bsz, H_MIX, HEAD_DIM, HEAD_DIM), F32),
                   jax.ShapeDtypeStruct((bsz, SUBLANES, HEAD_DIM), F32),
                   jax.ShapeDtypeStruct((bsz, SUBLANES, LANES), F32)],
        scratch_shapes=[pltpu.VMEM((H_MIX, HEAD_DIM, HEAD_DIM), F32),
                        pltpu.VMEM((SUBLANES, HEAD_DIM), F32),
                        pltpu.VMEM((SUBLANES, LANES), F32)],
        compiler_params=_cparams("parallel", "arbitrary"),
        name="mlstm_prompt",
    )(proj, gates, gbi_row, gbf_row)


def _gmlp_gv(dv, nw):
    gv = _gelu_tanh(dv)
    mu = jnp.mean(gv, axis=-1, keepdims=True)
    gc = gv - mu
    var = jnp.mean(gc * gc, axis=-1, keepdims=True)
    return gc * lax.rsqrt(var + LN_EPS) * nw


def _gmlp_prompt_kernel(tb, u_ref, v_ref, nw_ref, ws_ref, bt_ref, o_ref):
    gu = _gelu_tanh(u_ref[...])
    gv = _gmlp_gv(v_ref[...], nw_ref[...])
    tri, _ = _tri_masks(GM_CHUNK)
    gw = D_MIX // GM_GROUPS
    for g in range(GM_GROUPS):
        wm = jnp.where(tri, ws_ref[g], 0.0)
        bias = bt_ref[:, g:g + 1]
        for c in range(tb // GM_CHUNK):
            r0 = c * GM_CHUNK
            z = _dot(wm, gv[r0:r0 + GM_CHUNK, g * gw:(g + 1) * gw]) + bias
            o_ref[r0:r0 + GM_CHUNK, g * gw:(g + 1) * gw] = (
                gu[r0:r0 + GM_CHUNK, g * gw:(g + 1) * gw] * z).astype(o_ref.dtype)


def _gmlp_prompt(proj, nw_row, ws, b_t, *, rows, tb):
    return pl.pallas_call(
        functools.partial(_gmlp_prompt_kernel, tb),
        grid=(rows // tb,),
        in_specs=[pl.BlockSpec((tb, D_MIX), lambda i: (i, COL_D_U // D_MIX)),
                  pl.BlockSpec((tb, D_MIX), lambda i: (i, COL_D_V // D_MIX)),
                  pl.BlockSpec((1, D_MIX), lambda i: (0, 0)),
                  pl.BlockSpec((GM_GROUPS, GM_CHUNK, GM_CHUNK), lambda i: (0, 0, 0)),
                  pl.BlockSpec((GM_CHUNK, GM_GROUPS), lambda i: (0, 0))],
        out_specs=pl.BlockSpec((tb, D_MIX), lambda i: (i, 0)),
        out_shape=jax.ShapeDtypeStruct((rows, D_MIX), BF16),
        compiler_params=_cparams("parallel"),
        name="gmlp_prompt",
    )(proj, proj, nw_row, ws, b_t)


KMEAN_ROWS = 128


def _rope_tables(pos):
    half = ROT_DIM // 2
    inv_freq = ROPE_THETA ** (-jnp.arange(half, dtype=F32) * (2.0 / ROT_DIM))
    ang = pos.astype(F32)[:, None] * inv_freq[None, :]
    cos, sin = jnp.cos(ang), jnp.sin(ang)
    rest = jnp.ones((pos.shape[0], HEAD_DIM - ROT_DIM), F32)
    return (jnp.concatenate([cos, cos, rest], axis=1),
            jnp.concatenate([-sin, sin, 0.0 * rest], axis=1))


def _rope(x, cos, sin):
    lane = lax.broadcasted_iota(jnp.int32, (x.shape[0], HEAD_DIM), 1)
    first_half = lane < ROT_DIM // 2
    outs = []
    for h in range(H_MIX):
        xh = x[:, h * HEAD_DIM:(h + 1) * HEAD_DIM]
        rot = jnp.where(first_half, pltpu.roll(xh, HEAD_DIM - ROT_DIM // 2, 1), pltpu.roll(xh, ROT_DIM // 2, 1))
        outs.append(xh * cos + rot * sin)
    return jnp.concatenate(outs, axis=1)


def _rope_kernel(q_ref, k_ref, v_ref, cos_ref, sin_ref, qo_ref, ko_ref, vo_ref, kb_ref, vb_ref, km_ref):
    t = pl.program_id(1)
    cos = cos_ref[...]
    sin = sin_ref[...]
    qo_ref[...] = _rope(q_ref[...], cos, sin)
    kr = _rope(k_ref[...], cos, sin)
    v = v_ref[...]
    for h in range(H_MIX):
        ko_ref[pl.ds(h, MOBA_BLOCK, stride=H_MIX), :] = kr[:, h * HEAD_DIM:(h + 1) * HEAD_DIM]
        vo_ref[pl.ds(h, MOBA_BLOCK, stride=H_MIX), :] = v[:, h * HEAD_DIM:(h + 1) * HEAD_DIM]
    kb_ref[...] = kr.astype(BF16)
    vb_ref[...] = v.astype(BF16)

    @pl.when(t == 0)
    def _():
        km_ref[...] = jnp.zeros_like(km_ref)

    km_ref[0, pl.ds(t, 1), :] = jnp.mean(kr, axis=0, keepdims=True)


def _rope_prompt(proj, cos_tab, sin_tab, *, bsz, seqlen):
    nb = seqlen // MOBA_BLOCK
    col = lambda c: pl.BlockSpec((MOBA_BLOCK, D_MIX), lambda b, t: (b * nb + t, c // D_MIX))
    row = pl.BlockSpec((MOBA_BLOCK, D_MIX), lambda b, t: (b * nb + t, 0))
    head_rows = pl.BlockSpec((MOBA_BLOCK * H_MIX, HEAD_DIM), lambda b, t: (b * nb + t, 0))
    tab = pl.BlockSpec((MOBA_BLOCK, HEAD_DIM), lambda b, t: (t, 0))
    return pl.pallas_call(
        _rope_kernel,
        grid=(bsz, nb),
        in_specs=[col(COL_C_Q), col(COL_C_K), col(COL_C_V), tab, tab],
        out_specs=[row, head_rows, head_rows, row, row,
                   pl.BlockSpec((1, KMEAN_ROWS, D_MIX), lambda b, t: (b, 0, 0))],
        out_shape=[jax.ShapeDtypeStruct((bsz * seqlen, D_MIX), F32),
                   jax.ShapeDtypeStruct((bsz * seqlen * H_MIX, HEAD_DIM), F32),
                   jax.ShapeDtypeStruct((bsz * seqlen * H_MIX, HEAD_DIM), F32),
                   jax.ShapeDtypeStruct((bsz * seqlen, D_MIX), BF16),
                   jax.ShapeDtypeStruct((bsz * seqlen, D_MIX), BF16),
                   jax.ShapeDtypeStruct((bsz, KMEAN_ROWS, D_MIX), F32)],
        compiler_params=_cparams("parallel", "arbitrary"),
        name="rope_prompt",
    )(proj, proj, proj, cos_tab, sin_tab)


def _topk_block_rows(gate_t, n_valid):
    row = lax.broadcasted_iota(jnp.int32, gate_t.shape, 0)
    neg_inf = float("-inf")
    g = jnp.where(row < n_valid, gate_t, neg_inf)
    sel = jnp.zeros(gate_t.shape, F32)
    for _ in range(MOBA_TOPK):
        mx = jnp.max(g, axis=0, keepdims=True)
        first = jnp.min(jnp.where(g == mx, row, KMEAN_ROWS), axis=0, keepdims=True)
        pick = (row == first) & (mx > neg_inf)
        sel = jnp.where(pick, 1.0, sel)
        g = jnp.where(pick, neg_inf, g)
    return sel


def _moba_prompt_kernel(nb_pad, q_ref, k_ref, v_ref, km_ref, o_ref):
    qt = pl.program_id(1)
    blk = MOBA_BLOCK
    span = 2 * blk
    tri, _ = _tri_masks(blk)
    own0 = pl.multiple_of(qt * blk, blk)
    heads = [slice(h * HEAD_DIM, (h + 1) * HEAD_DIM) for h in range(H_MIX)]

    qs, sels, carry = [], [], []
    for hs in heads:
        q = q_ref[:, hs]
        sel_t = _topk_block_rows(_dot_nt_hi(km_ref[0][0:nb_pad, hs], q), qt)
        sel_t = jnp.concatenate([sel_t, jnp.zeros((KMEAN_ROWS - nb_pad, blk), F32)], axis=0)
        sels.append(sel_t.T.astype(BF16))
        qh = (q * (HEAD_DIM ** -0.5)).astype(BF16)
        qs.append(qh)
        s = jnp.where(tri, _dot_nt(qh, k_ref[pl.ds(own0, blk), hs]), NEG_BIG)
        m = jnp.max(s, axis=1, keepdims=True)
        p = jnp.exp(s - m)
        carry += [m, jnp.sum(p, axis=1, keepdims=True), _dot(p, v_ref[pl.ds(own0, blk), hs])]

    def body(j, carry):
        r0 = pl.multiple_of(j * span, span)
        key_block = 2 * j + lax.broadcasted_iota(jnp.int32, (KMEAN_ROWS, span), 1) // blk
        onehot = jnp.where(lax.broadcasted_iota(jnp.int32, (KMEAN_ROWS, span), 0) == key_block, 1.0, 0.0).astype(BF16)
        out = []
        for h, hs in enumerate(heads):
            m, l, acc = carry[3 * h:3 * h + 3]
            picked = jnp.dot(sels[h], onehot, preferred_element_type=F32) > 0.5
            s = jnp.where(picked, _dot_nt(qs[h], k_ref[pl.ds(r0, span), hs]), NEG_BIG)
            m_new = jnp.maximum(m, jnp.max(s, axis=1, keepdims=True))
            alpha = jnp.exp(m - m_new)
            p = jnp.where(picked, jnp.exp(s - m_new), 0.0)
            out += [m_new, alpha * l + jnp.sum(p, axis=1, keepdims=True),
                    alpha * acc + _dot(p, v_ref[pl.ds(r0, span), hs])]
        return tuple(out)

    carry = lax.fori_loop(0, (qt + 1) // 2, body, tuple(carry))
    for h, hs in enumerate(heads):
        o_ref[:, hs] = (carry[3 * h + 2] / carry[3 * h + 1]).astype(o_ref.dtype)


def _moba_prompt(q_rope, k_bf, v_bf, kmean, *, bsz, seqlen):
    nb = seqlen // MOBA_BLOCK
    nb_pad = -(-nb // SUBLANES) * SUBLANES
    assert nb_pad <= KMEAN_ROWS
    return pl.pallas_call(
        functools.partial(_moba_prompt_kernel, nb_pad),
        grid=(bsz, nb),
        in_specs=[pl.BlockSpec((MOBA_BLOCK, D_MIX), lambda b, t: (b * nb + t, 0)),
                  pl.BlockSpec((seqlen, D_MIX), lambda b, t: (b, 0)),
                  pl.BlockSpec((seqlen, D_MIX), lambda b, t: (b, 0)),
                  pl.BlockSpec((1, KMEAN_ROWS, D_MIX), lambda b, t: (b, 0, 0))],
        out_specs=pl.BlockSpec((MOBA_BLOCK, D_MIX), lambda b, t: (b * nb + t, 0)),
        out_shape=jax.ShapeDtypeStruct((bsz * seqlen, D_MIX), BF16),
        compiler_params=_cparams("parallel", "arbitrary"),
        name="moba_prompt",
    )(q_rope, k_bf, v_bf, kmean)


SAMPLE_GROUP = SUBLANES


def _columns(rows):
    pad = jnp.zeros((HEAD_DIM - SAMPLE_GROUP, HEAD_DIM), F32)
    return jnp.concatenate([rows, pad], axis=0).T


def _sample_state_kernel(p_ref, g_ref, conv_ref, s_ref, c_ref, n_ref, m_ref,
                         cw_ref, alog_ref, dtb_ref, nw_ref, gbi_ref, gbf_ref,
                         gmnw_ref, gmw0_ref, gmb0_ref, cos_ref, sin_ref,
                         oa_ref, ob_ref, od_ref, convo_ref, so_ref, co_ref, no_ref, mo_ref,
                         qr_ref, kr_ref, gv_ref):
    bg = SAMPLE_GROUP
    gates = g_ref[...]

    cw = cw_ref[...]
    xa = p_ref[:, COL_A_Q:COL_A_Q + 3 * D_MIX]
    y = cw[CONV_W - 1:CONV_W] * xa
    for j in range(CONV_W - 1):
        y = y + cw[j:j + 1] * conv_ref[j]
    y = _silu(y)
    for j in range(CONV_W - 2):
        convo_ref[j] = conv_ref[j + 1]
    convo_ref[CONV_W - 2] = xa
    beta_all = _sigmoid(gates)
    eg_all = jnp.exp(-jnp.exp(alog_ref[...]) * _softplus(gates + dtb_ref[...]))
    nw = nw_ref[...]
    for h in range(H_MIX):
        hs = slice(h * HEAD_DIM, (h + 1) * HEAD_DIM)
        q = y[:, COL_A_Q + h * HEAD_DIM:COL_A_Q + (h + 1) * HEAD_DIM]
        k = y[:, COL_A_K + h * HEAD_DIM:COL_A_K + (h + 1) * HEAD_DIM]
        v = y[:, COL_A_V + h * HEAD_DIM:COL_A_V + (h + 1) * HEAD_DIM]
        z = p_ref[:, COL_A_Z + h * HEAD_DIM:COL_A_Z + (h + 1) * HEAD_DIM]
        q = q * lax.rsqrt(jnp.sum(q * q, axis=-1, keepdims=True) + 1e-6) * (HEAD_DIM ** -0.5)
        k = k * lax.rsqrt(jnp.sum(k * k, axis=-1, keepdims=True) + 1e-6)
        beta = beta_all[:, GATE_A_BETA + h:GATE_A_BETA + h + 1]
        eg = eg_all[:, GATE_A_DEC + h:GATE_A_DEC + h + 1]
        qk = jnp.sum(q * k, axis=-1, keepdims=True)
        kt = _columns(k)
        qt = _columns(q)
        for i in range(bg):
            s = s_ref[i, h]
            kcol = kt[:, i:i + 1]
            e_i = eg[i:i + 1, :]
            ks = jnp.sum(kcol * s, axis=0, keepdims=True)
            qs = jnp.sum(qt[:, i:i + 1] * s, axis=0, keepdims=True)
            v_new = beta[i:i + 1, :] * (v[i:i + 1, :] - e_i * ks)
            oa_ref[i:i + 1, hs] = e_i * qs + qk[i:i + 1, :] * v_new
            so_ref[i, h] = e_i * s + kcol * v_new
        o = oa_ref[:, hs]
        o = o * lax.rsqrt(jnp.mean(o * o, axis=-1, keepdims=True) + LN_EPS) * nw
        oa_ref[:, hs] = o * _silu(z)

    li_all = gates + gbi_ref[...]
    lf_all = _log_sigmoid(gates + gbf_ref[...])
    for h in range(H_MIX):
        hs = slice(h * HEAD_DIM, (h + 1) * HEAD_DIM)
        q = p_ref[:, COL_B_Q + h * HEAD_DIM:COL_B_Q + (h + 1) * HEAD_DIM]
        k = p_ref[:, COL_B_K + h * HEAD_DIM:COL_B_K + (h + 1) * HEAD_DIM] * (HEAD_DIM ** -0.5)
        v = p_ref[:, COL_B_V + h * HEAD_DIM:COL_B_V + (h + 1) * HEAD_DIM]
        og = p_ref[:, COL_B_O + h * HEAD_DIM:COL_B_O + (h + 1) * HEAD_DIM]
        li = li_all[:, GATE_B_I + h:GATE_B_I + h + 1]
        lf = lf_all[:, GATE_B_F + h:GATE_B_F + h + 1]
        m_prev = m_ref[:, h:h + 1]
        n_prev = n_ref[:, hs]
        m_new = jnp.maximum(lf + m_prev, li)
        w_in = jnp.exp(li - m_new)
        dec = jnp.exp(lf + m_prev - m_new)
        sc = jnp.sum(q * k, axis=-1, keepdims=True) * w_in
        den = sc + dec * jnp.sum(q * n_prev, axis=-1, keepdims=True)
        denom = jnp.maximum(jnp.abs(den), jnp.exp(-m_new))
        no_ref[:, hs] = dec * n_prev + w_in * k
        mo_ref[:, h:h + 1] = m_new
        kt = _columns(k)
        qt = _columns(q)
        wv = w_in * v
        for i in range(bg):
            c = c_ref[i, h]
            d_i = dec[i:i + 1, :]
            qc = jnp.sum(qt[:, i:i + 1] * c, axis=0, keepdims=True)
            ob_ref[i:i + 1, hs] = sc[i:i + 1, :] * v[i:i + 1, :] + d_i * qc
            co_ref[i, h] = d_i * c + kt[:, i:i + 1] * wv[i:i + 1, :]
        ob_ref[:, hs] = _sigmoid(og) * (ob_ref[:, hs] / denom)

    gu = _gelu_tanh(p_ref[:, COL_D_U:COL_D_U + D_MIX])
    gv = _gmlp_gv(p_ref[:, COL_D_V:COL_D_V + D_MIX], gmnw_ref[...])
    gv_ref[...] = gv
    od_ref[...] = gu * (gmw0_ref[...] * gv + gmb0_ref[...])

    cos = cos_ref[...]
    sin = sin_ref[...]
    qr_ref[...] = _rope(p_ref[:, COL_C_Q:COL_C_Q + D_MIX], cos, sin)
    kr_ref[...] = _rope(p_ref[:, COL_C_K:COL_C_K + D_MIX], cos, sin)


def _sample_state(proj, gates, conv_t, s0, c0, n0, m0, conv_w, alog_row, dtb_row, nw_row, gbi_row, gbf_row,
                  gm_nw_row, gm_w0_row, gm_b0_row, cos_row, sin_row, *, layer):
    bg = SAMPLE_GROUP
    nrows = proj.shape[0]
    full = lambda shape: pl.BlockSpec(shape, lambda i: (0,) * len(shape))
    row_blk = lambda w: pl.BlockSpec((bg, w), lambda i: (i, 0))
    mat_blk = pl.BlockSpec((bg, H_MIX, HEAD_DIM, HEAD_DIM), lambda i: (i, 0, 0, 0))
    mat_in = pl.BlockSpec((None, bg, H_MIX, HEAD_DIM, HEAD_DIM), lambda i: (layer, i, 0, 0, 0))
    conv_blk = pl.BlockSpec((CONV_W - 1, bg, 3 * D_MIX), lambda i: (0, i, 0))
    f = lambda shape: jax.ShapeDtypeStruct(shape, F32)
    return pl.pallas_call(
        _sample_state_kernel,
        grid=(nrows // bg,),
        in_specs=[row_blk(N_MAIN), row_blk(LANES),
                  conv_blk, mat_in, mat_in, row_blk(D_MIX), row_blk(H_MIX),
                  full((CONV_W, 3 * D_MIX)), full((1, LANES)), full((1, LANES)), full((1, HEAD_DIM)),
                  full((1, LANES)), full((1, LANES)),
                  full((1, D_MIX)), full((1, D_MIX)), full((1, D_MIX)),
                  full((1, HEAD_DIM)), full((1, HEAD_DIM))],
        out_specs=[row_blk(D_MIX), row_blk(D_MIX), row_blk(D_MIX), conv_blk, mat_blk, mat_blk,
                   row_blk(D_MIX), row_blk(H_MIX), row_blk(D_MIX), row_blk(D_MIX), row_blk(D_MIX)],
        out_shape=[f((nrows, D_MIX)), f((nrows, D_MIX)), f((nrows, D_MIX)),
                   f((CONV_W - 1, nrows, 3 * D_MIX)),
                   f((nrows, H_MIX, HEAD_DIM, HEAD_DIM)), f((nrows, H_MIX, HEAD_DIM, HEAD_DIM)),
                   f((nrows, D_MIX)), f((nrows, H_MIX)),
                   f((nrows, D_MIX)), f((nrows, D_MIX)), f((nrows, D_MIX))],
        compiler_params=_cparams("parallel"),
        name="sample_state",
    )(proj, gates, conv_t, s0, c0, n0, m0, conv_w, alog_row, dtb_row, nw_row, gbi_row, gbf_row,
      gm_nw_row, gm_w0_row, gm_b0_row, cos_row, sin_row)


def _head_rows(row):
    sub = lax.broadcasted_iota(jnp.int32, (SUBLANES, HEAD_DIM), 0)
    out = jnp.zeros((SUBLANES, HEAD_DIM), F32)
    for h in range(H_MIX):
        out = jnp.where(sub == h, jnp.broadcast_to(row[:, h * HEAD_DIM:(h + 1) * HEAD_DIM], (SUBLANES, HEAD_DIM)), out)
    return out


def _moba_decode_kernel(n_pages, page_rows, pt_ref, q_ref, kn_ref, vn_ref, *refs):
    del pt_ref
    k_refs = refs[:n_pages]
    v_refs = refs[n_pages:2 * n_pages]
    o_ref = refs[2 * n_pages]
    page_size = page_rows // H_MIX
    pages_per_block = MOBA_BLOCK // page_size
    n_blocks = n_pages // pages_per_block
    scale = HEAD_DIM ** -0.5

    q8 = _head_rows(q_ref[0])
    q8_bf = q8.astype(BF16)
    sub = lax.broadcasted_iota(jnp.int32, (SUBLANES, page_rows), 0)
    col = lax.broadcasted_iota(jnp.int32, (SUBLANES, page_rows), 1)
    head_mask = (col % H_MIX) == sub
    sub_d = lax.broadcasted_iota(jnp.int32, (SUBLANES, HEAD_DIM), 0)

    gates = []
    for j in range(n_blocks):
        fold = jnp.zeros((SUBLANES, HEAD_DIM), F32)
        for p in range(j * pages_per_block, (j + 1) * pages_per_block):
            fold = fold + jnp.sum(k_refs[p][0, 0].reshape(page_rows // SUBLANES, SUBLANES, HEAD_DIM), axis=0)
        ksum = fold
        for g in range(1, SUBLANES // H_MIX):
            ksum = ksum + pltpu.roll(fold, SUBLANES - g * H_MIX, 0)
        ksum = jnp.where(sub_d < H_MIX, ksum, 0.0)
        gates.append(jnp.sum(q8 * ksum, axis=1, keepdims=True) * (1.0 / MOBA_BLOCK))
    sels = []
    for n in range(n_blocks):
        rank = jnp.zeros((SUBLANES, 1), F32)
        for m in range(n_blocks):
            if m == n:
                continue
            ahead = gates[m] > gates[n]
            if m < n:
                ahead = ahead | (gates[m] == gates[n])
            rank = rank + jnp.where(ahead, 1.0, 0.0)
        sels.append(rank < float(MOBA_TOPK))

    keeps = [head_mask & sels[p // pages_per_block] for p in range(n_pages)]
    scores = [jnp.where(keeps[p],
                        lax.dot_general(q8_bf, k_refs[p][0, 0].astype(BF16), (((1,), (1,)), ((), ())),
                                        preferred_element_type=F32) * scale, NEG_BIG)
              for p in range(n_pages)]
    s_own = jnp.sum(q8 * _head_rows(kn_ref[0]), axis=1, keepdims=True) * scale
    m_all = s_own
    for s in scores:
        m_all = jnp.maximum(m_all, jnp.max(s, axis=1, keepdims=True))
    probs = [jnp.where(keeps[p], jnp.exp(scores[p] - m_all), 0.0) for p in range(n_pages)]
    e_own = jnp.exp(s_own - m_all)
    l_all = e_own
    acc = e_own * _head_rows(vn_ref[0])
    for p in range(n_pages):
        l_all = l_all + jnp.sum(probs[p], axis=1, keepdims=True)
        acc = acc + _dot(probs[p], v_refs[p][0, 0])
    out = acc / l_all
    for h in range(H_MIX):
        o_ref[0, :, h * HEAD_DIM:(h + 1) * HEAD_DIM] = out[h:h + 1, :]


def _moba_decode(page_table, q_rope, k_new, v_new, cache_k, cache_v, *, layer):
    bsz, n_pages = page_table.shape
    page_rows = cache_k.shape[2]
    page_size = page_rows // H_MIX
    assert MOBA_BLOCK % page_size == 0 and (n_pages * page_size) % MOBA_BLOCK == 0
    assert (n_pages * page_size) // MOBA_BLOCK >= MOBA_TOPK and SUBLANES % H_MIX == 0
    row = pl.BlockSpec((1, 1, D_MIX), lambda b, pt: (b, 0, 0))
    page_specs = [pl.BlockSpec((1, 1, page_rows, HEAD_DIM), lambda b, pt, p=p: (layer, pt[b, p], 0, 0))
                  for p in range(n_pages)]
    return pl.pallas_call(
        functools.partial(_moba_decode_kernel, n_pages, page_rows),
        grid_spec=pltpu.PrefetchScalarGridSpec(
            num_scalar_prefetch=1,
            grid=(bsz,),
            in_specs=[row, row, row] + page_specs + page_specs,
            out_specs=row),
        out_shape=jax.ShapeDtypeStruct((bsz, 1, D_MIX), F32),
        compiler_params=_cparams("parallel"),
        name="moba_decode",
    )(page_table, q_rope, k_new, v_new, *([cache_k] * n_pages), *([cache_v] * n_pages))


def _largest_divisor(n, candidates):
    for c in candidates:
        if n % c == 0:
            return c
    raise ValueError(f"no tile in {candidates} divides {n}")


def _tiles(rows, seqlen):
    return dict(
        proj_tm=_largest_divisor(rows, (1024, 512, 256, 128)),
        proj_tn=512,
        out_tm=_largest_divisor(rows, (512, 256, 128)),
        ffn_tm=_largest_divisor(rows, (512, 256, 128)),
        ffn_tf=1024,
        scan_tb=_largest_divisor(seqlen, (256, 128, 64, 32, 16, 8)),
        gmlp_tb=_largest_divisor(seqlen, (512, 256, 128)),
    )


def _lane_row(vals, offset):
    return jnp.zeros((1, LANES), F32).at[0, offset:offset + vals.shape[0]].set(vals.astype(F32))


def kernel(x_prompt, x_sample, state_gdn_conv, state_gdn_s, state_mlstm_c, state_mlstm_n, state_mlstm_m,
           cache_k, cache_v, page_table, w_in, gdn_conv_w, gdn_a_log, gdn_dt_bias, gdn_norm_w, mlstm_gate_b,
           gmlp_norm_w, gmlp_ws, gmlp_b, w_out, ln1_w, ln1_b, w_up, w_down, ln2_w, ln2_b):
    bsz, seq, d_model = x_prompt.shape
    dec_b, dec_s, _ = x_sample.shape
    depth = w_in.shape[0]
    assert dec_s == 1 and d_model == N_MIXERS * D_MIX and seq >= CONV_W - 1
    assert seq % MOBA_BLOCK == 0 and seq // MOBA_BLOCK <= KMEAN_ROWS and seq % GM_CHUNK == 0
    assert w_in.shape[2] == N_MAIN + 4 * H_MIX and cache_k.shape[3] == H_MIX
    alpha = (2.0 * depth) ** 0.25
    mp = bsz * seq
    past_len = page_table.shape[1] * cache_k.shape[2]
    n_pool, page_size = cache_k.shape[1], cache_k.shape[2]
    tp = _tiles(mp, seq)
    ts = _tiles(dec_b, seq)
    gw = D_MIX // GM_GROUPS

    cos_p, sin_p = _rope_tables(jnp.arange(seq, dtype=jnp.int32))
    cos_s, sin_s = _rope_tables(past_len + jnp.arange(dec_s, dtype=jnp.int32))
    ck = cache_k.reshape(depth, n_pool, page_size * H_MIX, HEAD_DIM)
    cv = cache_v.reshape(depth, n_pool, page_size * H_MIX, HEAD_DIM)

    a_end = 4 * D_MIX
    b0 = a_end + 2 * H_MIX
    b_end = b0 + 4 * D_MIX
    c0 = b_end + 2 * H_MIX

    w_main = jnp.concatenate([w_in[:, :, :a_end], w_in[:, :, b0:b_end], w_in[:, :, c0:]], axis=2).astype(BF16)
    w_gate = jnp.concatenate([w_in[:, :, a_end:b0], w_in[:, :, b_end:c0],
                              jnp.zeros((depth, d_model, LANES - 4 * H_MIX), w_in.dtype)], axis=2).astype(BF16)
    w_out_bf = w_out.astype(BF16)
    w_up_bf = w_up.astype(BF16)
    w_down_bf = w_down.astype(BF16)
    gdn_s_in = state_gdn_s.astype(F32)
    ml_c_in = state_mlstm_c.astype(F32)

    xp = x_prompt.reshape(mp, d_model)
    xs = x_sample.reshape(dec_b, d_model)
    xp_in, xs_in = xp, xs
    p_st, s_st = [], []
    for l in range(depth):
        alog_row = _lane_row(gdn_a_log[l], GATE_A_DEC)
        dtb_row = _lane_row(gdn_dt_bias[l], GATE_A_DEC)
        nw_row = gdn_norm_w[l].reshape(1, HEAD_DIM).astype(F32)
        gbi_row = _lane_row(mlstm_gate_b[l][:H_MIX], GATE_B_I)
        gbf_row = _lane_row(mlstm_gate_b[l][H_MIX:], GATE_B_F)
        gm_nw_row = gmlp_norm_w[l].reshape(1, D_MIX).astype(F32)
        conv_w = gdn_conv_w[l].astype(F32)
        ln1 = (ln1_w[l].reshape(1, d_model).astype(F32), ln1_b[l].reshape(1, d_model).astype(F32))
        ln2 = (ln2_w[l].reshape(1, d_model).astype(F32), ln2_b[l].reshape(1, d_model).astype(F32))

        proj, gates = _proj(xp_in, w_main, w_gate, layer=l, tm=tp["proj_tm"], tn=tp["proj_tn"])
        oa, gdn_s_p = _gdn_prompt(proj, gates, conv_w, alog_row, dtb_row, nw_row,
                                  bsz=bsz, seqlen=seq, tb=tp["scan_tb"])
        ob, ml_c_p, ml_n_p, ml_m_p = _mlstm_prompt(proj, gates, gbi_row, gbf_row,
                                                   bsz=bsz, seqlen=seq, tb=tp["scan_tb"])
        q_rope, k_rows, v_rows, k_bf, v_bf, kmean = _rope_prompt(proj, cos_p, sin_p, bsz=bsz, seqlen=seq)
        oc = _moba_prompt(q_rope, k_bf, v_bf, kmean, bsz=bsz, seqlen=seq)
        od = _gmlp_prompt(proj, gm_nw_row, gmlp_ws[l].astype(F32), gmlp_b[l].astype(F32).T,
                          rows=mp, tb=tp["gmlp_tb"])
        x1, x1_bf = _outproj_ln((oa, ob, oc, od), w_out_bf, xp, *ln1, layer=l, alpha=alpha, tm=tp["out_tm"])
        xp, xp_in = _ffn_ln(x1_bf, w_up_bf, w_down_bf, x1, *ln2, layer=l, alpha=alpha,
                            tm=tp["ffn_tm"], tf=tp["ffn_tf"])
        proj_p = proj.reshape(bsz, seq, N_MAIN)
        p_st.append((proj_p[:, seq - (CONV_W - 1):, COL_A_Q:COL_A_Q + 3 * D_MIX],
                     gdn_s_p, ml_c_p, ml_n_p[:, :H_MIX], ml_m_p[:, :H_MIX, 0],
                     k_rows.reshape(bsz, seq, H_MIX, HEAD_DIM), v_rows.reshape(bsz, seq, H_MIX, HEAD_DIM)))

        proj, gates = _proj(xs_in, w_main, w_gate, layer=l, tm=ts["proj_tm"], tn=ts["proj_tn"])
        (oa, ob, od, conv_s, gdn_s_s, ml_c_s, ml_n_s, ml_m_s, q_s, k_s, gv_s) = _sample_state(
            proj, gates, jnp.transpose(state_gdn_conv[l].astype(F32), (1, 0, 2)),
            gdn_s_in, ml_c_in,
            state_mlstm_n[l].astype(F32).reshape(dec_b, D_MIX), state_mlstm_m[l].astype(F32),
            conv_w, alog_row, dtb_row, nw_row, gbi_row, gbf_row, gm_nw_row,
            jnp.repeat(gmlp_ws[l][:, 0, 0].astype(F32), gw).reshape(1, D_MIX),
            jnp.repeat(gmlp_b[l][:, 0].astype(F32), gw).reshape(1, D_MIX),
            cos_s, sin_s, layer=l)
        v_s = proj[:, COL_C_V:COL_C_V + D_MIX]
        oc = _moba_decode(page_table, q_s.reshape(dec_b, 1, D_MIX), k_s.reshape(dec_b, 1, D_MIX),
                          v_s.reshape(dec_b, 1, D_MIX), ck, cv, layer=l).reshape(dec_b, D_MIX)
        x1, x1_bf = _outproj_ln((oa, ob, oc, od), w_out_bf, xs, *ln1, layer=l, alpha=alpha, tm=ts["out_tm"])
        xs, xs_in = _ffn_ln(x1_bf, w_up_bf, w_down_bf, x1, *ln2, layer=l, alpha=alpha,
                            tm=ts["ffn_tm"], tf=ts["ffn_tf"])
        s_st.append((jnp.transpose(conv_s, (1, 0, 2)), gdn_s_s, ml_c_s,
                     ml_n_s.reshape(dec_b, H_MIX, HEAD_DIM), ml_m_s,
                     k_s.reshape(dec_b, dec_s, H_MIX, HEAD_DIM), v_s.reshape(dec_b, dec_s, H_MIX, HEAD_DIM),
                     gv_s.reshape(dec_b, dec_s, D_MIX)))

    def stk(sts, i):
        return jnp.stack([s[i] for s in sts], axis=0)

    dt = x_prompt.dtype
    yp = xp.reshape(bsz, seq, d_model).astype(dt)
    ys = xs.reshape(dec_b, dec_s, d_model).astype(dt)
    return (yp, ys) + tuple(stk(p_st, i).astype(dt) for i in range(7)) + tuple(stk(s_st, i).astype(dt) for i in range(8))
```

```python
import functools
import math

import jax
import jax.numpy as jnp
from jax import lax
from jax.experimental import pallas as pl
from jax.experimental.pallas import tpu as pltpu

F32 = jnp.float32
BF16 = jnp.bfloat16
HIGHEST = lax.Precision.HIGHEST

HEAD_DIM = 128
N_MIXERS = 4
CONV_W = 4
GDN_CHUNK = 64
MLSTM_CHUNK = 64
MOBA_BLOCK = 256
MOBA_TOPK = 3
GM_CHUNK = 128
GM_GROUPS = 4
ROPE_THETA = 500000.0
ROT_DIM = HEAD_DIM // 4
LN_EPS = 1e-5
NEG_BIG = -1e30

LANES = 128
SUBLANES = 8
VMEM_LIMIT_BYTES = 56 * 1024 * 1024


def _cparams(*sem):
    return pltpu.CompilerParams(dimension_semantics=sem, vmem_limit_bytes=VMEM_LIMIT_BYTES)


def _dot(a, b):
    return jnp.dot(a.astype(BF16), b.astype(BF16), preferred_element_type=F32)


def _dot_nt(a, b):
    return lax.dot_general(a.astype(BF16), b.astype(BF16), (((1,), (1,)), ((), ())),
                           preferred_element_type=F32)


def _dot_tn(a, b):
    return lax.dot_general(a.astype(BF16), b.astype(BF16), (((0,), (0,)), ((), ())),
                           preferred_element_type=F32)


def _dot_hi(a, b):
    return jnp.dot(a, b, precision=HIGHEST, preferred_element_type=F32)


def _dot_nt_hi(a, b):
    return lax.dot_general(a, b, (((1,), (1,)), ((), ())), precision=HIGHEST,
                           preferred_element_type=F32)


def _split2(x):
    hi = x.astype(BF16)
    return hi, (x - hi.astype(F32)).astype(BF16)


def _dot_split(a_parts, b_parts):
    ah, al = a_parts
    bh, bl = b_parts
    return (jnp.dot(ah, bh, preferred_element_type=F32) + jnp.dot(ah, bl, preferred_element_type=F32)
            + jnp.dot(al, bh, preferred_element_type=F32))


def _cumsum_rows(tril_bf, x):
    x0 = x.astype(BF16)
    r1 = x - x0.astype(F32)
    x1 = r1.astype(BF16)
    x2 = (r1 - x1.astype(F32)).astype(BF16)
    return (jnp.dot(tril_bf, x0, preferred_element_type=F32) + jnp.dot(tril_bf, x1, preferred_element_type=F32)
            + jnp.dot(tril_bf, x2, preferred_element_type=F32))


def _sigmoid(x):
    return 1.0 / (1.0 + jnp.exp(-x))


def _silu(x):
    return x * _sigmoid(x)


def _softplus(x):
    return jnp.maximum(x, 0.0) + jnp.log(1.0 + jnp.exp(-jnp.abs(x)))


def _log_sigmoid(x):
    return -_softplus(-x)


def _gelu_tanh(x):
    return 0.5 * x * (1.0 + jnp.tanh(math.sqrt(2.0 / math.pi) * (x + 0.044715 * (x * x * x))))


def _layer_norm(x, w, b):
    mu = jnp.mean(x, axis=-1, keepdims=True)
    xc = x - mu
    var = jnp.mean(xc * xc, axis=-1, keepdims=True)
    return xc * lax.rsqrt(var + LN_EPS) * w + b


D_MIX = 512
H_MIX = D_MIX // HEAD_DIM
COL_A_Q, COL_A_K, COL_A_V, COL_A_Z = 0, 512, 1024, 1536
COL_B_Q, COL_B_K, COL_B_V, COL_B_O = 2048, 2560, 3072, 3584
COL_C_Q, COL_C_K, COL_C_V = 4096, 4608, 5120
COL_D_U, COL_D_V = 5632, 6144
N_MAIN = 6656
GATE_A_BETA, GATE_A_DEC, GATE_B_I, GATE_B_F = 0, 4, 8, 12


def _tri_masks(n):
    r = lax.broadcasted_iota(jnp.int32, (n, n), 0)
    c = lax.broadcasted_iota(jnp.int32, (n, n), 1)
    return r >= c, r > c


def _pack_w_in_kernel(w_ref, main_ref, gate_ref):
    a_end = 4 * D_MIX
    b0 = a_end + 2 * H_MIX
    b_end = b0 + 4 * D_MIX
    c0 = b_end + 2 * H_MIX
    main_ref[0, :, 0:a_end] = w_ref[0, :, 0:a_end].astype(BF16)
    main_ref[0, :, a_end:2 * a_end] = w_ref[0, :, b0:b_end].astype(BF16)
    main_ref[0, :, 2 * a_end:N_MAIN] = w_ref[0, :, c0:c0 + N_MAIN - 2 * a_end].astype(BF16)
    rows = w_ref.shape[1]
    gate_ref[0] = jnp.concatenate([w_ref[0, :, a_end:b0], w_ref[0, :, b_end:c0],
                                   jnp.zeros((rows, LANES - 4 * H_MIX), F32)], axis=1).astype(BF16)


def _pack_w_in(w_in, *, tr):
    depth, d, n_in = w_in.shape
    return pl.pallas_call(
        _pack_w_in_kernel,
        grid=(depth, d // tr),
        in_specs=[pl.BlockSpec((1, tr, n_in), lambda l, i: (l, i, 0))],
        out_specs=[pl.BlockSpec((1, tr, N_MAIN), lambda l, i: (l, i, 0)),
                   pl.BlockSpec((1, tr, LANES), lambda l, i: (l, i, 0))],
        out_shape=[jax.ShapeDtypeStruct((depth, d, N_MAIN), BF16),
                   jax.ShapeDtypeStruct((depth, d, LANES), BF16)],
        compiler_params=_cparams("parallel", "parallel"),
        name="pack_w_in",
    )(w_in)


def _proj_kernel(x_ref, w_ref, wg_ref, o_ref, g_ref, xb_ref):
    @pl.when(pl.program_id(1) == 0)
    def _():
        xb_ref[...] = x_ref[...].astype(BF16)
        g_ref[...] = jnp.dot(xb_ref[...], wg_ref[0], preferred_element_type=F32)

    o_ref[...] = jnp.dot(xb_ref[...], w_ref[0], preferred_element_type=F32)


def _proj(x, w_main, w_gate, *, layer, tm, tn):
    m, d = x.shape
    n = w_main.shape[2]
    return pl.pallas_call(
        _proj_kernel,
        grid=(m // tm, n // tn),
        in_specs=[pl.BlockSpec((tm, d), lambda i, j: (i, 0)),
                  pl.BlockSpec((1, d, tn), lambda i, j: (layer, 0, j)),
                  pl.BlockSpec((1, d, LANES), lambda i, j: (layer, 0, 0))],
        out_specs=[pl.BlockSpec((tm, tn), lambda i, j: (i, j)),
                   pl.BlockSpec((tm, LANES), lambda i, j: (i, 0))],
        out_shape=[jax.ShapeDtypeStruct((m, n), F32), jax.ShapeDtypeStruct((m, LANES), F32)],
        scratch_shapes=[pltpu.VMEM((tm, d), BF16)],
        compiler_params=_cparams("parallel", "arbitrary"),
        name="proj",
    )(x, w_main, w_gate)


def _outproj_ln_kernel(alpha, a_ref, b_ref, c_ref, d_ref, w_ref, x_ref, lw_ref, lb_ref, o_ref, ob_ref):
    y = alpha * x_ref[...]
    for i, m_ref in enumerate((a_ref, b_ref, c_ref, d_ref)):
        y = y + jnp.dot(m_ref[...].astype(BF16), w_ref[0, i * D_MIX:(i + 1) * D_MIX, :],
                        preferred_element_type=F32)
    y = _layer_norm(y, lw_ref[...], lb_ref[...])
    o_ref[...] = y
    ob_ref[...] = y.astype(BF16)


def _outproj_ln(mixes, w_out_bf, x, ln_w, ln_b, *, layer, alpha, tm):
    m, d = x.shape
    mix_spec = pl.BlockSpec((tm, D_MIX), lambda i: (i, 0))
    return pl.pallas_call(
        functools.partial(_outproj_ln_kernel, alpha),
        grid=(m // tm,),
        in_specs=[mix_spec, mix_spec, mix_spec, mix_spec,
                  pl.BlockSpec((1, d, d), lambda i: (layer, 0, 0)),
                  pl.BlockSpec((tm, d), lambda i: (i, 0)),
                  pl.BlockSpec((1, d), lambda i: (0, 0)),
                  pl.BlockSpec((1, d), lambda i: (0, 0))],
        out_specs=[pl.BlockSpec((tm, d), lambda i: (i, 0)),
                   pl.BlockSpec((tm, d), lambda i: (i, 0))],
        out_shape=[jax.ShapeDtypeStruct((m, d), F32), jax.ShapeDtypeStruct((m, d), BF16)],
        compiler_params=_cparams("parallel"),
        name="outproj_ln",
    )(*mixes, w_out_bf, x, ln_w, ln_b)


def _ffn_ln_kernel(alpha, xb_ref, wu_ref, wd_ref, x_ref, lw_ref, lb_ref, o_ref, ob_ref, acc_ref):
    j = pl.program_id(1)

    @pl.when(j == 0)
    def _():
        acc_ref[...] = alpha * x_ref[...]

    h = jnp.dot(xb_ref[...], wu_ref[0], preferred_element_type=F32)
    h = jnp.maximum(h, 0.0)
    h = (h * h).astype(BF16)
    acc_ref[...] += jnp.dot(h, wd_ref[0], preferred_element_type=F32)

    @pl.when(j == pl.num_programs(1) - 1)
    def _():
        y = _layer_norm(acc_ref[...], lw_ref[...], lb_ref[...])
        o_ref[...] = y
        ob_ref[...] = y.astype(BF16)


def _ffn_ln(x_bf, w_up_bf, w_down_bf, x, ln_w, ln_b, *, layer, alpha, tm, tf):
    m, d = x.shape
    f = w_up_bf.shape[2]
    return pl.pallas_call(
        functools.partial(_ffn_ln_kernel, alpha),
        grid=(m // tm, f // tf),
        in_specs=[pl.BlockSpec((tm, d), lambda i, j: (i, 0)),
                  pl.BlockSpec((1, d, tf), lambda i, j: (layer, 0, j)),
                  pl.BlockSpec((1, tf, d), lambda i, j: (layer, j, 0)),
                  pl.BlockSpec((tm, d), lambda i, j: (i, 0)),
                  pl.BlockSpec((1, d), lambda i, j: (0, 0)),
                  pl.BlockSpec((1, d), lambda i, j: (0, 0))],
        out_specs=[pl.BlockSpec((tm, d), lambda i, j: (i, 0)),
                   pl.BlockSpec((tm, d), lambda i, j: (i, 0))],
        out_shape=[jax.ShapeDtypeStruct((m, d), F32), jax.ShapeDtypeStruct((m, d), BF16)],
        scratch_shapes=[pltpu.VMEM((tm, d), F32)],
        compiler_params=_cparams("parallel", "arbitrary"),
        name="ffn_ln",
    )(x_bf, w_up_bf, w_down_bf, x, ln_w, ln_b)


def _unit_lower_inverses(mats):
    n = mats[0].shape[0]
    r = lax.broadcasted_iota(jnp.int32, (n, n), 0)
    c = lax.broadcasted_iota(jnp.int32, (n, n), 1)
    eye = jnp.where(r == c, 1.0, 0.0).astype(F32)
    ts = [eye - a for a in mats]
    ps = [_split2(a) for a in mats]
    span = 2
    while span < n:
        ps = [_split2(_dot_split(p, p)) for p in ps]
        ts = [t + _dot_split(_split2(t), p) for t, p in zip(ts, ps)]
        span *= 2
    return ts


def _gdn_prompt_kernel(tb, cl, x_ref, g_ref, cw_ref, alog_ref, dtb_ref, nw_ref,
                       o_ref, s_out_ref, xbuf, s_ref):
    t = pl.program_id(1)
    dqkv = 3 * D_MIX

    @pl.when(t == 0)
    def _():
        xbuf[0:SUBLANES, :] = jnp.zeros((SUBLANES, dqkv), F32)
        s_ref[...] = jnp.zeros_like(s_ref)

    @pl.when(t > 0)
    def _():
        xbuf[0:SUBLANES, :] = xbuf[tb:tb + SUBLANES, :]

    xbuf[SUBLANES:SUBLANES + tb, :] = x_ref[:, 0:dqkv]
    cw = cw_ref[...]
    y = cw[0:1] * xbuf[SUBLANES - 3:SUBLANES - 3 + tb, :]
    for j in range(1, CONV_W):
        y = y + cw[j:j + 1] * xbuf[SUBLANES - 3 + j:SUBLANES - 3 + j + tb, :]
    y = _silu(y)

    gates = g_ref[...]
    beta_all = _sigmoid(gates)
    g_all = -jnp.exp(alog_ref[...]) * _softplus(gates + dtb_ref[...])
    tri, tri_s = _tri_masks(cl)
    tril_bf = jnp.where(tri, 1.0, 0.0).astype(BF16)
    nw = nw_ref[...]
    states = [s_ref[h] for h in range(H_MIX)]

    a_mats, pre = [], []
    for c in range(tb // cl):
        r0 = c * cl
        gcum = _cumsum_rows(tril_bf, g_all[r0:r0 + cl, :])
        gcum_t = gcum.T
        for h in range(H_MIX):
            q = y[r0:r0 + cl, COL_A_Q + h * HEAD_DIM:COL_A_Q + (h + 1) * HEAD_DIM]
            k = y[r0:r0 + cl, COL_A_K + h * HEAD_DIM:COL_A_K + (h + 1) * HEAD_DIM]
            v = y[r0:r0 + cl, COL_A_V + h * HEAD_DIM:COL_A_V + (h + 1) * HEAD_DIM]
            q = q * lax.rsqrt(jnp.sum(q * q, axis=-1, keepdims=True) + 1e-6) * (HEAD_DIM ** -0.5)
            k = k * lax.rsqrt(jnp.sum(k * k, axis=-1, keepdims=True) + 1e-6)
            beta = beta_all[r0:r0 + cl, GATE_A_BETA + h:GATE_A_BETA + h + 1]
            gc = gcum[:, GATE_A_DEC + h:GATE_A_DEC + h + 1]
            gr = gcum_t[GATE_A_DEC + h:GATE_A_DEC + h + 1, :]
            decay = jnp.exp(jnp.where(tri, gc - gr, NEG_BIG))
            kb = k * beta
            egc = jnp.exp(gc)
            gc_last = gc[cl - 1:cl, :]
            kk = _dot_nt(jnp.concatenate([kb, q], axis=0), k)
            a_mats.append(jnp.where(tri_s, kk[0:cl] * decay, 0.0))
            pre.append(dict(attn=jnp.where(tri, kk[cl:2 * cl] * decay, 0.0),
                            rhs=jnp.concatenate([v * beta, kb * egc], axis=1),
                            q_dec=q * egc, k_tail_t=(k * jnp.exp(gc_last - gc)).T, g_tot=jnp.exp(gc_last)))
    t_invs = _unit_lower_inverses(a_mats)
    uws = [_dot(t_inv, p["rhs"]) for t_inv, p in zip(t_invs, pre)]

    for c in range(tb // cl):
        r0 = c * cl
        for h in range(H_MIX):
            p, uw = pre[c * H_MIX + h], uws[c * H_MIX + h]
            z = x_ref[r0:r0 + cl, COL_A_Z + h * HEAD_DIM:COL_A_Z + (h + 1) * HEAD_DIM]
            s = states[h]
            ws_qs = _dot(jnp.concatenate([uw[:, HEAD_DIM:], p["q_dec"]], axis=0), s)
            v_new = uw[:, :HEAD_DIM] - ws_qs[0:cl]
            o = ws_qs[cl:2 * cl] + _dot(p["attn"], v_new)
            states[h] = s * p["g_tot"] + _dot(p["k_tail_t"], v_new)
            o = o * lax.rsqrt(jnp.mean(o * o, axis=-1, keepdims=True) + LN_EPS) * nw
            o_ref[r0:r0 + cl, h * HEAD_DIM:(h + 1) * HEAD_DIM] = (o * _silu(z)).astype(o_ref.dtype)

    for h in range(H_MIX):
        s_ref[h] = states[h]

    @pl.when(t == pl.num_programs(1) - 1)
    def _():
        s_out_ref[0] = s_ref[...]


def _gdn_prompt(proj, gates, conv_w, alog_row, dtb_row, nw_row, *, bsz, seqlen, tb):
    cl = math.gcd(seqlen, GDN_CHUNK)
    nt = seqlen // tb
    return pl.pallas_call(
        functools.partial(_gdn_prompt_kernel, tb, cl),
        grid=(bsz, nt),
        in_specs=[pl.BlockSpec((tb, 4 * D_MIX), lambda b, t: (b * nt + t, 0)),
                  pl.BlockSpec((tb, LANES), lambda b, t: (b * nt + t, 0)),
                  pl.BlockSpec((CONV_W, 3 * D_MIX), lambda b, t: (0, 0)),
                  pl.BlockSpec((1, LANES), lambda b, t: (0, 0)),
                  pl.BlockSpec((1, LANES), lambda b, t: (0, 0)),
                  pl.BlockSpec((1, HEAD_DIM), lambda b, t: (0, 0))],
        out_specs=[pl.BlockSpec((tb, D_MIX), lambda b, t: (b * nt + t, 0)),
                   pl.BlockSpec((1, H_MIX, HEAD_DIM, HEAD_DIM), lambda b, t: (b, 0, 0, 0))],
        out_shape=[jax.ShapeDtypeStruct((bsz * seqlen, D_MIX), BF16),
                   jax.ShapeDtypeStruct((bsz, H_MIX, HEAD_DIM, HEAD_DIM), F32)],
        scratch_shapes=[pltpu.VMEM((tb + SUBLANES, 3 * D_MIX), F32),
                        pltpu.VMEM((H_MIX, HEAD_DIM, HEAD_DIM), F32)],
        compiler_params=_cparams("parallel", "arbitrary"),
        name="gdn_prompt",
    )(proj, gates, conv_w, alog_row, dtb_row, nw_row)


def _mlstm_prompt_kernel(tb, cl, x_ref, g_ref, gbi_ref, gbf_ref,
                         o_ref, c_out_ref, n_out_ref, m_out_ref, c_ref, n_ref, m_ref):
    t = pl.program_id(1)

    @pl.when(t == 0)
    def _():
        c_ref[...] = jnp.zeros_like(c_ref)
        n_ref[...] = jnp.zeros_like(n_ref)
        m_ref[...] = jnp.zeros_like(m_ref)

    gates = g_ref[...]
    li_all = gates + gbi_ref[...]
    lf_all = _log_sigmoid(gates + gbf_ref[...])
    tri, _ = _tri_masks(cl)
    tril_bf = jnp.where(tri, 1.0, 0.0).astype(BF16)
    cs = [c_ref[h] for h in range(H_MIX)]
    ns = [n_ref[h:h + 1, :] for h in range(H_MIX)]
    ms = [m_ref[h:h + 1, 0:1] for h in range(H_MIX)]

    for c in range(tb // cl):
        r0 = c * cl
        bcum = _cumsum_rows(tril_bf, lf_all[r0:r0 + cl, :])
        bcum_t = bcum.T
        li_t = li_all[r0:r0 + cl, :].T
        for h in range(H_MIX):
            q = x_ref[r0:r0 + cl, h * HEAD_DIM:(h + 1) * HEAD_DIM]
            k = x_ref[r0:r0 + cl, D_MIX + h * HEAD_DIM:D_MIX + (h + 1) * HEAD_DIM] * (HEAD_DIM ** -0.5)
            v = x_ref[r0:r0 + cl, 2 * D_MIX + h * HEAD_DIM:2 * D_MIX + (h + 1) * HEAD_DIM]
            og = x_ref[r0:r0 + cl, 3 * D_MIX + h * HEAD_DIM:3 * D_MIX + (h + 1) * HEAD_DIM]
            b = bcum[:, GATE_B_F + h:GATE_B_F + h + 1]
            br = bcum_t[GATE_B_F + h:GATE_B_F + h + 1, :]
            ic = li_all[r0:r0 + cl, GATE_B_I + h:GATE_B_I + h + 1]
            ir = li_t[GATE_B_I + h:GATE_B_I + h + 1, :]
            m_prev = ms[h]
            dlog = jnp.where(tri, b - br + ir, NEG_BIG)
            inter = b + m_prev
            m = jnp.maximum(jnp.max(dlog, axis=1, keepdims=True), inter)
            s = _dot_nt(q, k) * jnp.exp(dlog - m)
            scale_prev = jnp.exp(inter - m)
            num = _dot(s, v) + scale_prev * _dot(q, cs[h])
            den = jnp.sum(s, axis=1, keepdims=True) + scale_prev * jnp.sum(q * ns[h], axis=1, keepdims=True)
            hh = num / jnp.maximum(jnp.abs(den), jnp.exp(-m))
            b_end = b[cl - 1:cl, :]
            wlog = b_end - b + ic
            m_new = jnp.maximum(b_end + m_prev, jnp.max(wlog, axis=0, keepdims=True))
            wk = jnp.exp(wlog - m_new) * k
            dec = jnp.exp(b_end + m_prev - m_new)
            cs[h] = dec * cs[h] + _dot_tn(wk, v)
            ns[h] = dec * ns[h] + jnp.sum(wk, axis=0, keepdims=True)
            ms[h] = m_new
            o_ref[r0:r0 + cl, h * HEAD_DIM:(h + 1) * HEAD_DIM] = (_sigmoid(og) * hh).astype(o_ref.dtype)

    for h in range(H_MIX):
        c_ref[h] = cs[h]
        n_ref[h:h + 1, :] = ns[h]
        m_ref[h:h + 1, :] = jnp.broadcast_to(ms[h], (1, LANES))

    @pl.when(t == pl.num_programs(1) - 1)
    def _():
        c_out_ref[0] = c_ref[...]
        n_out_ref[0] = n_ref[...]
        m_out_ref[0] = m_ref[...]


def _mlstm_prompt(proj, gates, gbi_row, gbf_row, *, bsz, seqlen, tb):
    cl = math.gcd(seqlen, MLSTM_CHUNK)
    nt = seqlen // tb
    return pl.pallas_call(
        functools.partial(_mlstm_prompt_kernel, tb, cl),
        grid=(bsz, nt),
        in_specs=[pl.BlockSpec((tb, 4 * D_MIX), lambda b, t: (b * nt + t, COL_B_Q // (4 * D_MIX))),
                  pl.BlockSpec((tb, LANES), lambda b, t: (b * nt + t, 0)),
                  pl.BlockSpec((1, LANES), lambda b, t: (0, 0)),
                  pl.BlockSpec((1, LANES), lambda b, t: (0, 0))],
        out_specs=[pl.BlockSpec((tb, D_MIX), lambda b, t: (b * nt + t, 0)),
                   pl.BlockSpec((1, H_MIX, HEAD_DIM, HEAD_DIM), lambda b, t: (b, 0, 0, 0)),
                   pl.BlockSpec((1, SUBLANES, HEAD_DIM), lambda b, t: (b, 0, 0)),
                   pl.BlockSpec((1, SUBLANES, LANES), lambda b, t: (b, 0, 0))],
        out_shape=[jax.ShapeDtypeStruct((bsz * seqlen, D_MIX), BF16),
                   jax.ShapeDtypeStruct((bsz, H_MIX, HEAD_DIM, HEAD_DIM), F32),
                   jax.ShapeDtypeStruct((bsz, SUBLANES, HEAD_DIM), F32),
                   jax.ShapeDtypeStruct((bsz, SUBLANES, LANES), F32)],
        scratch_shapes=[pltpu.VMEM((H_MIX, HEAD_DIM, HEAD_DIM), F32),
                        pltpu.VMEM((SUBLANES, HEAD_DIM), F32),
                        pltpu.VMEM((SUBLANES, LANES), F32)],
        compiler_params=_cparams("parallel", "arbitrary"),
        name="mlstm_prompt",
    )(proj, gates, gbi_row, gbf_row)


def _gmlp_gv(dv, nw):
    gv = _gelu_tanh(dv)
    mu = jnp.mean(gv, axis=-1, keepdims=True)
    gc = gv - mu
    var = jnp.mean(gc * gc, axis=-1, keepdims=True)
    return gc * lax.rsqrt(var + LN_EPS) * nw


def _gmlp_prompt_kernel(tb, u_ref, v_ref, nw_ref, ws_ref, bt_ref, o_ref):
    gu = _gelu_tanh(u_ref[...])
    gv = _gmlp_gv(v_ref[...], nw_ref[...])
    tri, _ = _tri_masks(GM_CHUNK)
    gw = D_MIX // GM_GROUPS
    for g in range(GM_GROUPS):
        wm = jnp.where(tri, ws_ref[g], 0.0)
        bias = bt_ref[:, g:g + 1]
        for c in range(tb // GM_CHUNK):
            r0 = c * GM_CHUNK
            z = _dot(wm, gv[r0:r0 + GM_CHUNK, g * gw:(g + 1) * gw]) + bias
            o_ref[r0:r0 + GM_CHUNK, g * gw:(g + 1) * gw] = (
                gu[r0:r0 + GM_CHUNK, g * gw:(g + 1) * gw] * z).astype(o_ref.dtype)


def _gmlp_prompt(proj, nw_row, ws, b_t, *, rows, tb):
    return pl.pallas_call(
        functools.partial(_gmlp_prompt_kernel, tb),
        grid=(rows // tb,),
        in_specs=[pl.BlockSpec((tb, D_MIX), lambda i: (i, COL_D_U // D_MIX)),
                  pl.BlockSpec((tb, D_MIX), lambda i: (i, COL_D_V // D_MIX)),
                  pl.BlockSpec((1, D_MIX), lambda i: (0, 0)),
                  pl.BlockSpec((GM_GROUPS, GM_CHUNK, GM_CHUNK), lambda i: (0, 0, 0)),
                  pl.BlockSpec((GM_CHUNK, GM_GROUPS), lambda i: (0, 0))],
        out_specs=pl.BlockSpec((tb, D_MIX), lambda i: (i, 0)),
        out_shape=jax.ShapeDtypeStruct((rows, D_MIX), BF16),
        compiler_params=_cparams("parallel"),
        name="gmlp_prompt",
    )(proj, proj, nw_row, ws, b_t)


KMEAN_ROWS = 128


def _rope_tables(pos):
    half = ROT_DIM // 2
    inv_freq = ROPE_THETA ** (-jnp.arange(half, dtype=F32) * (2.0 / ROT_DIM))
    ang = pos.astype(F32)[:, None] * inv_freq[None, :]
    cos, sin = jnp.cos(ang), jnp.sin(ang)
    rest = jnp.ones((pos.shape[0], HEAD_DIM - ROT_DIM), F32)
    return (jnp.concatenate([cos, cos, rest], axis=1),
            jnp.concatenate([-sin, sin, 0.0 * rest], axis=1))


def _rope(x, cos, sin):
    lane = lax.broadcasted_iota(jnp.int32, (x.shape[0], HEAD_DIM), 1)
    first_half = lane < ROT_DIM // 2
    outs = []
    for h in range(H_MIX):
        xh = x[:, h * HEAD_DIM:(h + 1) * HEAD_DIM]
        rot = jnp.where(first_half, pltpu.roll(xh, HEAD_DIM - ROT_DIM // 2, 1), pltpu.roll(xh, ROT_DIM // 2, 1))
        outs.append(xh * cos + rot * sin)
    return jnp.concatenate(outs, axis=1)


def _rope_kernel(q_ref, k_ref, v_ref, cos_ref, sin_ref, qo_ref, ko_ref, vo_ref, kb_ref, vb_ref, km_ref):
    t = pl.program_id(1)
    cos = cos_ref[...]
    sin = sin_ref[...]
    qo_ref[...] = _rope(q_ref[...], cos, sin)
    kr = _rope(k_ref[...], cos, sin)
    v = v_ref[...]
    for h in range(H_MIX):
        ko_ref[pl.ds(h, MOBA_BLOCK, stride=H_MIX), :] = kr[:, h * HEAD_DIM:(h + 1) * HEAD_DIM]
        vo_ref[pl.ds(h, MOBA_BLOCK, stride=H_MIX), :] = v[:, h * HEAD_DIM:(h + 1) * HEAD_DIM]
    kb_ref[...] = kr.astype(BF16)
    vb_ref[...] = v.astype(BF16)

    @pl.when(t == 0)
    def _():
        km_ref[...] = jnp.zeros_like(km_ref)

    km_ref[0, pl.ds(t, 1), :] = jnp.mean(kr, axis=0, keepdims=True)


def _rope_prompt(proj, cos_tab, sin_tab, *, bsz, seqlen):
    nb = seqlen // MOBA_BLOCK
    col = lambda c: pl.BlockSpec((MOBA_BLOCK, D_MIX), lambda b, t: (b * nb + t, c // D_MIX))
    row = pl.BlockSpec((MOBA_BLOCK, D_MIX), lambda b, t: (b * nb + t, 0))
    head_rows = pl.BlockSpec((MOBA_BLOCK * H_MIX, HEAD_DIM), lambda b, t: (b * nb + t, 0))
    tab = pl.BlockSpec((MOBA_BLOCK, HEAD_DIM), lambda b, t: (t, 0))
    return pl.pallas_call(
        _rope_kernel,
        grid=(bsz, nb),
        in_specs=[col(COL_C_Q), col(COL_C_K), col(COL_C_V), tab, tab],
        out_specs=[row, head_rows, head_rows, row, row,
                   pl.BlockSpec((1, KMEAN_ROWS, D_MIX), lambda b, t: (b, 0, 0))],
        out_shape=[jax.ShapeDtypeStruct((bsz * seqlen, D_MIX), F32),
                   jax.ShapeDtypeStruct((bsz * seqlen * H_MIX, HEAD_DIM), F32),
                   jax.ShapeDtypeStruct((bsz * seqlen * H_MIX, HEAD_DIM), F32),
                   jax.ShapeDtypeStruct((bsz * seqlen, D_MIX), BF16),
                   jax.ShapeDtypeStruct((bsz * seqlen, D_MIX), BF16),
                   jax.ShapeDtypeStruct((bsz, KMEAN_ROWS, D_MIX), F32)],
        compiler_params=_cparams("parallel", "arbitrary"),
        name="rope_prompt",
    )(proj, proj, proj, cos_tab, sin_tab)


def _topk_block_rows(gate_t, n_valid):
    row = lax.broadcasted_iota(jnp.int32, gate_t.shape, 0)
    neg_inf = float("-inf")
    g = jnp.where(row < n_valid, gate_t, neg_inf)
    sel = jnp.zeros(gate_t.shape, F32)
    for _ in range(MOBA_TOPK):
        mx = jnp.max(g, axis=0, keepdims=True)
        first = jnp.min(jnp.where(g == mx, row, KMEAN_ROWS), axis=0, keepdims=True)
        pick = (row == first) & (mx > neg_inf)
        sel = jnp.where(pick, 1.0, sel)
        g = jnp.where(pick, neg_inf, g)
    return sel


def _moba_prompt_kernel(nb_pad, q_ref, k_ref, v_ref, km_ref, o_ref):
    qt = pl.program_id(1)
    blk = MOBA_BLOCK
    span = 2 * blk
    tri, _ = _tri_masks(blk)
    own0 = pl.multiple_of(qt * blk, blk)
    heads = [slice(h * HEAD_DIM, (h + 1) * HEAD_DIM) for h in range(H_MIX)]

    gates_t = [_dot_nt_hi(km_ref[0][0:nb_pad, hs], q_ref[:, hs]) for hs in heads]
    qhs = [(q_ref[:, hs] * (HEAD_DIM ** -0.5)).astype(BF16) for hs in heads]
    own_s = [jnp.where(tri, _dot_nt(qhs[h], k_ref[pl.ds(own0, blk), hs]), NEG_BIG) for h, hs in enumerate(heads)]
    own_m = [jnp.max(s, axis=1, keepdims=True) for s in own_s]
    own_p = [jnp.exp(s - m) for s, m in zip(own_s, own_m)]
    own_pv = [_dot(own_p[h], v_ref[pl.ds(own0, blk), hs]) for h, hs in enumerate(heads)]
    q_augs, carry = [], []
    for h in range(H_MIX):
        sel_t = _topk_block_rows(gates_t[h], qt)
        unpicked_t = jnp.concatenate([1.0 - sel_t, jnp.ones((KMEAN_ROWS - nb_pad, blk), F32)], axis=0)
        q_augs.append(jnp.concatenate([qhs[h], unpicked_t.T.astype(BF16)], axis=1))
        carry += [own_m[h], jnp.sum(own_p[h], axis=1, keepdims=True), own_pv[h]]

    def body(j, carry):
        r0 = pl.multiple_of(j * span, span)
        key_block = 2 * j + lax.broadcasted_iota(jnp.int32, (span, KMEAN_ROWS), 0) // blk
        block_bias = jnp.where(lax.broadcasted_iota(jnp.int32, (span, KMEAN_ROWS), 1) == key_block,
                               NEG_BIG, 0.0).astype(BF16)
        ss = [lax.dot_general(q_augs[h], jnp.concatenate([k_ref[pl.ds(r0, span), hs], block_bias], axis=1),
                              (((1,), (1,)), ((), ())), preferred_element_type=F32)
              for h, hs in enumerate(heads)]
        m_news = [jnp.maximum(carry[3 * h], jnp.max(ss[h], axis=1, keepdims=True)) for h in range(H_MIX)]
        ps = [jnp.exp(ss[h] - m_news[h]) for h in range(H_MIX)]
        pvs = [_dot(ps[h], v_ref[pl.ds(r0, span), hs]) for h, hs in enumerate(heads)]
        out = []
        for h in range(H_MIX):
            alpha = jnp.exp(carry[3 * h] - m_news[h])
            out += [m_news[h], alpha * carry[3 * h + 1] + jnp.sum(ps[h], axis=1, keepdims=True),
                    alpha * carry[3 * h + 2] + pvs[h]]
        return tuple(out)

    carry = lax.fori_loop(0, (qt + 1) // 2, body, tuple(carry))
    for h, hs in enumerate(heads):
        o_ref[:, hs] = (carry[3 * h + 2] / carry[3 * h + 1]).astype(o_ref.dtype)


def _moba_prompt(q_rope, k_bf, v_bf, kmean, *, bsz, seqlen):
    nb = seqlen // MOBA_BLOCK
    nb_pad = -(-nb // SUBLANES) * SUBLANES
    assert nb_pad <= KMEAN_ROWS
    return pl.pallas_call(
        functools.partial(_moba_prompt_kernel, nb_pad),
        grid=(bsz, nb),
        in_specs=[pl.BlockSpec((MOBA_BLOCK, D_MIX), lambda b, t: (b * nb + t, 0)),
                  pl.BlockSpec((seqlen, D_MIX), lambda b, t: (b, 0)),
                  pl.BlockSpec((seqlen, D_MIX), lambda b, t: (b, 0)),
                  pl.BlockSpec((1, KMEAN_ROWS, D_MIX), lambda b, t: (b, 0, 0))],
        out_specs=pl.BlockSpec((MOBA_BLOCK, D_MIX), lambda b, t: (b * nb + t, 0)),
        out_shape=jax.ShapeDtypeStruct((bsz * seqlen, D_MIX), BF16),
        compiler_params=_cparams("parallel", "arbitrary"),
        name="moba_prompt",
    )(q_rope, k_bf, v_bf, kmean)


SAMPLE_GROUP = SUBLANES


def _columns(rows):
    pad = jnp.zeros((HEAD_DIM - SAMPLE_GROUP, HEAD_DIM), F32)
    return jnp.concatenate([rows, pad], axis=0).T


def _sample_state_kernel(p_ref, g_ref, conv_ref, s_ref, c_ref, n_ref, m_ref,
                         cw_ref, alog_ref, dtb_ref, nw_ref, gbi_ref, gbf_ref,
                         gmnw_ref, gmw0_ref, gmb0_ref, cos_ref, sin_ref,
                         oa_ref, ob_ref, od_ref, convo_ref, so_ref, co_ref, no_ref, mo_ref,
                         qr_ref, kr_ref, gv_ref):
    bg = SAMPLE_GROUP
    gates = g_ref[...]

    cw = cw_ref[...]
    xa = p_ref[:, COL_A_Q:COL_A_Q + 3 * D_MIX]
    y = cw[CONV_W - 1:CONV_W] * xa
    for j in range(CONV_W - 1):
        y = y + cw[j:j + 1] * conv_ref[j]
    y = _silu(y)
    for j in range(CONV_W - 2):
        convo_ref[j] = conv_ref[j + 1]
    convo_ref[CONV_W - 2] = xa
    beta_all = _sigmoid(gates)
    eg_all = jnp.exp(-jnp.exp(alog_ref[...]) * _softplus(gates + dtb_ref[...]))
    nw = nw_ref[...]
    for h in range(H_MIX):
        hs = slice(h * HEAD_DIM, (h + 1) * HEAD_DIM)
        q = y[:, COL_A_Q + h * HEAD_DIM:COL_A_Q + (h + 1) * HEAD_DIM]
        k = y[:, COL_A_K + h * HEAD_DIM:COL_A_K + (h + 1) * HEAD_DIM]
        v = y[:, COL_A_V + h * HEAD_DIM:COL_A_V + (h + 1) * HEAD_DIM]
        z = p_ref[:, COL_A_Z + h * HEAD_DIM:COL_A_Z + (h + 1) * HEAD_DIM]
        q = q * lax.rsqrt(jnp.sum(q * q, axis=-1, keepdims=True) + 1e-6) * (HEAD_DIM ** -0.5)
        k = k * lax.rsqrt(jnp.sum(k * k, axis=-1, keepdims=True) + 1e-6)
        beta = beta_all[:, GATE_A_BETA + h:GATE_A_BETA + h + 1]
        eg = eg_all[:, GATE_A_DEC + h:GATE_A_DEC + h + 1]
        qk = jnp.sum(q * k, axis=-1, keepdims=True)
        kt = _columns(k)
        qt = _columns(q)
        for i in range(bg):
            s = s_ref[i, h]
            kcol = kt[:, i:i + 1]
            e_i = eg[i:i + 1, :]
            ks = jnp.sum(kcol * s, axis=0, keepdims=True)
            qs = jnp.sum(qt[:, i:i + 1] * s, axis=0, keepdims=True)
            v_new = beta[i:i + 1, :] * (v[i:i + 1, :] - e_i * ks)
            oa_ref[i:i + 1, hs] = e_i * qs + qk[i:i + 1, :] * v_new
            so_ref[i, h] = e_i * s + kcol * v_new
        o = oa_ref[:, hs]
        o = o * lax.rsqrt(jnp.mean(o * o, axis=-1, keepdims=True) + LN_EPS) * nw
        oa_ref[:, hs] = o * _silu(z)

    li_all = gates + gbi_ref[...]
    lf_all = _log_sigmoid(gates + gbf_ref[...])
    for h in range(H_MIX):
        hs = slice(h * HEAD_DIM, (h + 1) * HEAD_DIM)
        q = p_ref[:, COL_B_Q + h * HEAD_DIM:COL_B_Q + (h + 1) * HEAD_DIM]
        k = p_ref[:, COL_B_K + h * HEAD_DIM:COL_B_K + (h + 1) * HEAD_DIM] * (HEAD_DIM ** -0.5)
        v = p_ref[:, COL_B_V + h * HEAD_DIM:COL_B_V + (h + 1) * HEAD_DIM]
        og = p_ref[:, COL_B_O + h * HEAD_DIM:COL_B_O + (h + 1) * HEAD_DIM]
        li = li_all[:, GATE_B_I + h:GATE_B_I + h + 1]
        lf = lf_all[:, GATE_B_F + h:GATE_B_F + h + 1]
        m_prev = m_ref[:, h:h + 1]
        n_prev = n_ref[:, hs]
        m_new = jnp.maximum(lf + m_prev, li)
        w_in = jnp.exp(li - m_new)
        dec = jnp.exp(lf + m_prev - m_new)
        sc = jnp.sum(q * k, axis=-1, keepdims=True) * w_in
        den = sc + dec * jnp.sum(q * n_prev, axis=-1, keepdims=True)
        denom = jnp.maximum(jnp.abs(den), jnp.exp(-m_new))
        no_ref[:, hs] = dec * n_prev + w_in * k
        mo_ref[:, h:h + 1] = m_new
        kt = _columns(k)
        qt = _columns(q)
        wv = w_in * v
        for i in range(bg):
            c = c_ref[i, h]
            d_i = dec[i:i + 1, :]
            qc = jnp.sum(qt[:, i:i + 1] * c, axis=0, keepdims=True)
            ob_ref[i:i + 1, hs] = sc[i:i + 1, :] * v[i:i + 1, :] + d_i * qc
            co_ref[i, h] = d_i * c + kt[:, i:i + 1] * wv[i:i + 1, :]
        ob_ref[:, hs] = _sigmoid(og) * (ob_ref[:, hs] / denom)

    gu = _gelu_tanh(p_ref[:, COL_D_U:COL_D_U + D_MIX])
    gv = _gmlp_gv(p_ref[:, COL_D_V:COL_D_V + D_MIX], gmnw_ref[...])
    gv_ref[...] = gv
    od_ref[...] = gu * (gmw0_ref[...] * gv + gmb0_ref[...])

    cos = cos_ref[...]
    sin = sin_ref[...]
    qr_ref[...] = _rope(p_ref[:, COL_C_Q:COL_C_Q + D_MIX], cos, sin)
    kr_ref[...] = _rope(p_ref[:, COL_C_K:COL_C_K + D_MIX], cos, sin)


def _sample_state(proj, gates, conv_t, s0, c0, n0, m0, conv_w, alog_row, dtb_row, nw_row, gbi_row, gbf_row,
                  gm_nw_row, gm_w0_row, gm_b0_row, cos_row, sin_row, *, layer):
    bg = SAMPLE_GROUP
    nrows = proj.shape[0]
    full = lambda shape: pl.BlockSpec(shape, lambda i: (0,) * len(shape))
    row_blk = lambda w: pl.BlockSpec((bg, w), lambda i: (i, 0))
    mat_blk = pl.BlockSpec((bg, H_MIX, HEAD_DIM, HEAD_DIM), lambda i: (i, 0, 0, 0))
    mat_in = pl.BlockSpec((None, bg, H_MIX, HEAD_DIM, HEAD_DIM), lambda i: (layer, i, 0, 0, 0))
    conv_blk = pl.BlockSpec((CONV_W - 1, bg, 3 * D_MIX), lambda i: (0, i, 0))
    f = lambda shape: jax.ShapeDtypeStruct(shape, F32)
    return pl.pallas_call(
        _sample_state_kernel,
        grid=(nrows // bg,),
        in_specs=[row_blk(N_MAIN), row_blk(LANES),
                  conv_blk, mat_in, mat_in, row_blk(D_MIX), row_blk(H_MIX),
                  full((CONV_W, 3 * D_MIX)), full((1, LANES)), full((1, LANES)), full((1, HEAD_DIM)),
                  full((1, LANES)), full((1, LANES)),
                  full((1, D_MIX)), full((1, D_MIX)), full((1, D_MIX)),
                  full((1, HEAD_DIM)), full((1, HEAD_DIM))],
        out_specs=[row_blk(D_MIX), row_blk(D_MIX), row_blk(D_MIX), conv_blk, mat_blk, mat_blk,
                   row_blk(D_MIX), row_blk(H_MIX), row_blk(D_MIX), row_blk(D_MIX), row_blk(D_MIX)],
        out_shape=[f((nrows, D_MIX)), f((nrows, D_MIX)), f((nrows, D_MIX)),
                   f((CONV_W - 1, nrows, 3 * D_MIX)),
                   f((nrows, H_MIX, HEAD_DIM, HEAD_DIM)), f((nrows, H_MIX, HEAD_DIM, HEAD_DIM)),
                   f((nrows, D_MIX)), f((nrows, H_MIX)),
                   f((nrows, D_MIX)), f((nrows, D_MIX)), f((nrows, D_MIX))],
        compiler_params=_cparams("parallel"),
        name="sample_state",
    )(proj, gates, conv_t, s0, c0, n0, m0, conv_w, alog_row, dtb_row, nw_row, gbi_row, gbf_row,
      gm_nw_row, gm_w0_row, gm_b0_row, cos_row, sin_row)


def _head_rows(row):
    sub = lax.broadcasted_iota(jnp.int32, (SUBLANES, HEAD_DIM), 0)
    out = jnp.zeros((SUBLANES, HEAD_DIM), F32)
    for h in range(H_MIX):
        out = jnp.where(sub == h, jnp.broadcast_to(row[:, h * HEAD_DIM:(h + 1) * HEAD_DIM], (SUBLANES, HEAD_DIM)), out)
    return out


def _moba_decode_kernel(n_pages, page_rows, pt_ref, q_ref, kn_ref, vn_ref, *refs):
    del pt_ref
    k_refs = refs[:n_pages]
    v_refs = refs[n_pages:2 * n_pages]
    o_ref = refs[2 * n_pages]
    page_size = page_rows // H_MIX
    pages_per_block = MOBA_BLOCK // page_size
    n_blocks = n_pages // pages_per_block
    scale = HEAD_DIM ** -0.5

    q8 = _head_rows(q_ref[0])
    q8_bf = q8.astype(BF16)
    sub = lax.broadcasted_iota(jnp.int32, (SUBLANES, page_rows), 0)
    col = lax.broadcasted_iota(jnp.int32, (SUBLANES, page_rows), 1)
    head_mask = (col % H_MIX) == sub
    sub_d = lax.broadcasted_iota(jnp.int32, (SUBLANES, HEAD_DIM), 0)

    gates = []
    for j in range(n_blocks):
        fold = jnp.zeros((SUBLANES, HEAD_DIM), F32)
        for p in range(j * pages_per_block, (j + 1) * pages_per_block):
            fold = fold + jnp.sum(k_refs[p][0, 0].reshape(page_rows // SUBLANES, SUBLANES, HEAD_DIM), axis=0)
        ksum = fold
        for g in range(1, SUBLANES // H_MIX):
            ksum = ksum + pltpu.roll(fold, SUBLANES - g * H_MIX, 0)
        ksum = jnp.where(sub_d < H_MIX, ksum, 0.0)
        gates.append(jnp.sum(q8 * ksum, axis=1, keepdims=True) * (1.0 / MOBA_BLOCK))
    sels = []
    for n in range(n_blocks):
        rank = jnp.zeros((SUBLANES, 1), F32)
        for m in range(n_blocks):
            if m == n:
                continue
            ahead = gates[m] > gates[n]
            if m < n:
                ahead = ahead | (gates[m] == gates[n])
            rank = rank + jnp.where(ahead, 1.0, 0.0)
        sels.append(rank < float(MOBA_TOPK))

    keeps = [head_mask & sels[p // pages_per_block] for p in range(n_pages)]
    scores = [jnp.where(keeps[p],
                        lax.dot_general(q8_bf, k_refs[p][0, 0].astype(BF16), (((1,), (1,)), ((), ())),
                                        preferred_element_type=F32) * scale, NEG_BIG)
              for p in range(n_pages)]
    s_own = jnp.sum(q8 * _head_rows(kn_ref[0]), axis=1, keepdims=True) * scale
    m_all = s_own
    for s in scores:
        m_all = jnp.maximum(m_all, jnp.max(s, axis=1, keepdims=True))
    probs = [jnp.where(keeps[p], jnp.exp(scores[p] - m_all), 0.0) for p in range(n_pages)]
    e_own = jnp.exp(s_own - m_all)
    l_all = e_own
    acc = e_own * _head_rows(vn_ref[0])
    for p in range(n_pages):
        l_all = l_all + jnp.sum(probs[p], axis=1, keepdims=True)
        acc = acc + _dot(probs[p], v_refs[p][0, 0])
    out = acc / l_all
    for h in range(H_MIX):
        o_ref[0, :, h * HEAD_DIM:(h + 1) * HEAD_DIM] = out[h:h + 1, :]


def _moba_decode(page_table, q_rope, k_new, v_new, cache_k, cache_v, *, layer):
    bsz, n_pages = page_table.shape
    page_rows = cache_k.shape[2]
    page_size = page_rows // H_MIX
    assert MOBA_BLOCK % page_size == 0 and (n_pages * page_size) % MOBA_BLOCK == 0
    assert (n_pages * page_size) // MOBA_BLOCK >= MOBA_TOPK and SUBLANES % H_MIX == 0
    row = pl.BlockSpec((1, 1, D_MIX), lambda b, pt: (b, 0, 0))
    page_specs = [pl.BlockSpec((1, 1, page_rows, HEAD_DIM), lambda b, pt, p=p: (layer, pt[b, p], 0, 0))
                  for p in range(n_pages)]
    return pl.pallas_call(
        functools.partial(_moba_decode_kernel, n_pages, page_rows),
        grid_spec=pltpu.PrefetchScalarGridSpec(
            num_scalar_prefetch=1,
            grid=(bsz,),
            in_specs=[row, row, row] + page_specs + page_specs,
            out_specs=row),
        out_shape=jax.ShapeDtypeStruct((bsz, 1, D_MIX), F32),
        compiler_params=_cparams("parallel"),
        name="moba_decode",
    )(page_table, q_rope, k_new, v_new, *([cache_k] * n_pages), *([cache_v] * n_pages))


def _largest_divisor(n, candidates):
    for c in candidates:
        if n % c == 0:
            return c
    raise ValueError(f"no tile in {candidates} divides {n}")


def _tiles(rows, seqlen):
    return dict(
        proj_tm=_largest_divisor(rows, (1024, 512, 256, 128)),
        proj_tn=512,
        out_tm=_largest_divisor(rows, (512, 256, 128)),
        ffn_tm=_largest_divisor(rows, (512, 256, 128)),
        ffn_tf=1024,
        scan_tb=_largest_divisor(seqlen, (256, 128, 64, 32, 16, 8)),
        gmlp_tb=_largest_divisor(seqlen, (512, 256, 128)),
    )


def _lane_row(vals, offset):
    return jnp.zeros((1, LANES), F32).at[0, offset:offset + vals.shape[0]].set(vals.astype(F32))


def kernel(x_prompt, x_sample, state_gdn_conv, state_gdn_s, state_mlstm_c, state_mlstm_n, state_mlstm_m,
           cache_k, cache_v, page_table, w_in, gdn_conv_w, gdn_a_log, gdn_dt_bias, gdn_norm_w, mlstm_gate_b,
           gmlp_norm_w, gmlp_ws, gmlp_b, w_out, ln1_w, ln1_b, w_up, w_down, ln2_w, ln2_b):
    bsz, seq, d_model = x_prompt.shape
    dec_b, dec_s, _ = x_sample.shape
    depth = w_in.shape[0]
    assert dec_s == 1 and d_model == N_MIXERS * D_MIX and seq >= CONV_W - 1
    assert seq % MOBA_BLOCK == 0 and seq // MOBA_BLOCK <= KMEAN_ROWS and seq % GM_CHUNK == 0
    assert w_in.shape[2] == N_MAIN + 4 * H_MIX and cache_k.shape[3] == H_MIX
    alpha = (2.0 * depth) ** 0.25
    mp = bsz * seq
    past_len = page_table.shape[1] * cache_k.shape[2]
    n_pool, page_size = cache_k.shape[1], cache_k.shape[2]
    tp = _tiles(mp, seq)
    ts = _tiles(dec_b, seq)
    gw = D_MIX // GM_GROUPS

    cos_p, sin_p = _rope_tables(jnp.arange(seq, dtype=jnp.int32))
    cos_s, sin_s = _rope_tables(past_len + jnp.arange(dec_s, dtype=jnp.int32))
    ck = cache_k.reshape(depth, n_pool, page_size * H_MIX, HEAD_DIM)
    cv = cache_v.reshape(depth, n_pool, page_size * H_MIX, HEAD_DIM)

    w_main, w_gate = _pack_w_in(w_in.astype(F32), tr=_largest_divisor(d_model, (256, 128, 64, 32, 16)))
    w_out_bf = w_out.astype(BF16)
    w_up_bf = w_up.astype(BF16)
    w_down_bf = w_down.astype(BF16)
    gdn_s_in = state_gdn_s.astype(F32)
    ml_c_in = state_mlstm_c.astype(F32)

    xp = x_prompt.reshape(mp, d_model)
    xs = x_sample.reshape(dec_b, d_model)
    xp_in, xs_in = xp, xs
    p_st, s_st = [], []
    for l in range(depth):
        alog_row = _lane_row(gdn_a_log[l], GATE_A_DEC)
        dtb_row = _lane_row(gdn_dt_bias[l], GATE_A_DEC)
        nw_row = gdn_norm_w[l].reshape(1, HEAD_DIM).astype(F32)
        gbi_row = _lane_row(mlstm_gate_b[l][:H_MIX], GATE_B_I)
        gbf_row = _lane_row(mlstm_gate_b[l][H_MIX:], GATE_B_F)
        gm_nw_row = gmlp_norm_w[l].reshape(1, D_MIX).astype(F32)
        conv_w = gdn_conv_w[l].astype(F32)
        ln1 = (ln1_w[l].reshape(1, d_model).astype(F32), ln1_b[l].reshape(1, d_model).astype(F32))
        ln2 = (ln2_w[l].reshape(1, d_model).astype(F32), ln2_b[l].reshape(1, d_model).astype(F32))

        proj, gates = _proj(xp_in, w_main, w_gate, layer=l, tm=tp["proj_tm"], tn=tp["proj_tn"])
        oa, gdn_s_p = _gdn_prompt(proj, gates, conv_w, alog_row, dtb_row, nw_row,
                                  bsz=bsz, seqlen=seq, tb=tp["scan_tb"])
        ob, ml_c_p, ml_n_p, ml_m_p = _mlstm_prompt(proj, gates, gbi_row, gbf_row,
                                                   bsz=bsz, seqlen=seq, tb=tp["scan_tb"])
        q_rope, k_rows, v_rows, k_bf, v_bf, kmean = _rope_prompt(proj, cos_p, sin_p, bsz=bsz, seqlen=seq)
        oc = _moba_prompt(q_rope, k_bf, v_bf, kmean, bsz=bsz, seqlen=seq)
        od = _gmlp_prompt(proj, gm_nw_row, gmlp_ws[l].astype(F32), gmlp_b[l].astype(F32).T,
                          rows=mp, tb=tp["gmlp_tb"])
        x1, x1_bf = _outproj_ln((oa, ob, oc, od), w_out_bf, xp, *ln1, layer=l, alpha=alpha, tm=tp["out_tm"])
        xp, xp_in = _ffn_ln(x1_bf, w_up_bf, w_down_bf, x1, *ln2, layer=l, alpha=alpha,
                            tm=tp["ffn_tm"], tf=tp["ffn_tf"])
        proj_p = proj.reshape(bsz, seq, N_MAIN)
        p_st.append((proj_p[:, seq - (CONV_W - 1):, COL_A_Q:COL_A_Q + 3 * D_MIX],
                     gdn_s_p, ml_c_p, ml_n_p[:, :H_MIX], ml_m_p[:, :H_MIX, 0],
                     k_rows.reshape(bsz, seq, H_MIX, HEAD_DIM), v_rows.reshape(bsz, seq, H_MIX, HEAD_DIM)))

        proj, gates = _proj(xs_in, w_main, w_gate, layer=l, tm=ts["proj_tm"], tn=ts["proj_tn"])
        (oa, ob, od, conv_s, gdn_s_s, ml_c_s, ml_n_s, ml_m_s, q_s, k_s, gv_s) = _sample_state(
            proj, gates, jnp.transpose(state_gdn_conv[l].astype(F32), (1, 0, 2)),
            gdn_s_in, ml_c_in,
            state_mlstm_n[l].astype(F32).reshape(dec_b, D_MIX), state_mlstm_m[l].astype(F32),
            conv_w, alog_row, dtb_row, nw_row, gbi_row, gbf_row, gm_nw_row,
            jnp.repeat(gmlp_ws[l][:, 0, 0].astype(F32), gw).reshape(1, D_MIX),
            jnp.repeat(gmlp_b[l][:, 0].astype(F32), gw).reshape(1, D_MIX),
            cos_s, sin_s, layer=l)
        v_s = proj[:, COL_C_V:COL_C_V + D_MIX]
        oc = _moba_decode(page_table, q_s.reshape(dec_b, 1, D_MIX), k_s.reshape(dec_b, 1, D_MIX),
                          v_s.reshape(dec_b, 1, D_MIX), ck, cv, layer=l).reshape(dec_b, D_MIX)
        x1, x1_bf = _outproj_ln((oa, ob, oc, od), w_out_bf, xs, *ln1, layer=l, alpha=alpha, tm=ts["out_tm"])
        xs, xs_in = _ffn_ln(x1_bf, w_up_bf, w_down_bf, x1, *ln2, layer=l, alpha=alpha,
                            tm=ts["ffn_tm"], tf=ts["ffn_tf"])
        s_st.append((jnp.transpose(conv_s, (1, 0, 2)), gdn_s_s, ml_c_s,
                     ml_n_s.reshape(dec_b, H_MIX, HEAD_DIM), ml_m_s,
                     k_s.reshape(dec_b, dec_s, H_MIX, HEAD_DIM), v_s.reshape(dec_b, dec_s, H_MIX, HEAD_DIM),
                     gv_s.reshape(dec_b, dec_s, D_MIX)))

    def stk(sts, i):
        return jnp.stack([s[i] for s in sts], axis=0)

    dt = x_prompt.dtype
    yp = xp.reshape(bsz, seq, d_model).astype(dt)
    ys = xs.reshape(dec_b, dec_s, d_model).astype(dt)
    return (yp, ys) + tuple(stk(p_st, i).astype(dt) for i in range(7)) + tuple(stk(s_st, i).astype(dt) for i in range(8))
```

```python
import functools
import math

import jax
import jax.numpy as jnp
from jax import lax
from jax.experimental import pallas as pl
from jax.experimental.pallas import tpu as pltpu

F32 = jnp.float32
BF16 = jnp.bfloat16
HIGHEST = lax.Precision.HIGHEST

HEAD_DIM = 128
N_MIXERS = 4
CONV_W = 4
GDN_CHUNK = 64
MLSTM_CHUNK = 64
MOBA_BLOCK = 256
MOBA_TOPK = 3
GM_CHUNK = 128
GM_GROUPS = 4
ROPE_THETA = 500000.0
ROT_DIM = HEAD_DIM // 4
LN_EPS = 1e-5
NEG_BIG = -1e30

LANES = 128
SUBLANES = 8
VMEM_LIMIT_BYTES = 56 * 1024 * 1024


def _cparams(*sem):
    return pltpu.CompilerParams(dimension_semantics=sem, vmem_limit_bytes=VMEM_LIMIT_BYTES)


def _dot(a, b):
    return jnp.dot(a.astype(BF16), b.astype(BF16), preferred_element_type=F32)


def _dot_nt(a, b):
    return lax.dot_general(a.astype(BF16), b.astype(BF16), (((1,), (1,)), ((), ())),
                           preferred_element_type=F32)


def _dot_tn(a, b):
    return lax.dot_general(a.astype(BF16), b.astype(BF16), (((0,), (0,)), ((), ())),
                           preferred_element_type=F32)


def _dot_hi(a, b):
    return jnp.dot(a, b, precision=HIGHEST, preferred_element_type=F32)


def _dot_nt_hi(a, b):
    return lax.dot_general(a, b, (((1,), (1,)), ((), ())), precision=HIGHEST,
                           preferred_element_type=F32)


def _split2(x):
    hi = x.astype(BF16)
    return hi, (x - hi.astype(F32)).astype(BF16)


def _dot_split(a_parts, b_parts):
    ah, al = a_parts
    bh, bl = b_parts
    return (jnp.dot(ah, bh, preferred_element_type=F32) + jnp.dot(ah, bl, preferred_element_type=F32)
            + jnp.dot(al, bh, preferred_element_type=F32))


def _cumsum_rows(tril_bf, x):
    x0 = x.astype(BF16)
    r1 = x - x0.astype(F32)
    x1 = r1.astype(BF16)
    x2 = (r1 - x1.astype(F32)).astype(BF16)
    return (jnp.dot(tril_bf, x0, preferred_element_type=F32) + jnp.dot(tril_bf, x1, preferred_element_type=F32)
            + jnp.dot(tril_bf, x2, preferred_element_type=F32))


def _sigmoid(x):
    return 1.0 / (1.0 + jnp.exp(-x))


def _silu(x):
    return x * _sigmoid(x)


def _softplus(x):
    return jnp.maximum(x, 0.0) + jnp.log(1.0 + jnp.exp(-jnp.abs(x)))


def _log_sigmoid(x):
    return -_softplus(-x)


def _gelu_tanh(x):
    return 0.5 * x * (1.0 + jnp.tanh(math.sqrt(2.0 / math.pi) * (x + 0.044715 * (x * x * x))))


def _layer_norm(x, w, b):
    mu = jnp.mean(x, axis=-1, keepdims=True)
    xc = x - mu
    var = jnp.mean(xc * xc, axis=-1, keepdims=True)
    return xc * lax.rsqrt(var + LN_EPS) * w + b


D_MIX = 512
H_MIX = D_MIX // HEAD_DIM
COL_A_Q, COL_A_K, COL_A_V, COL_A_Z = 0, 512, 1024, 1536
COL_B_Q, COL_B_K, COL_B_V, COL_B_O = 2048, 2560, 3072, 3584
COL_C_Q, COL_C_K, COL_C_V = 4096, 4608, 5120
COL_D_U, COL_D_V = 5632, 6144
N_MAIN = 6656
GATE_A_BETA, GATE_A_DEC, GATE_B_I, GATE_B_F = 0, 4, 8, 12


def _tri_masks(n):
    r = lax.broadcasted_iota(jnp.int32, (n, n), 0)
    c = lax.broadcasted_iota(jnp.int32, (n, n), 1)
    return r >= c, r > c


def _pack_w_in_kernel(w_ref, main_ref, gate_ref):
    a_end = 4 * D_MIX
    b0 = a_end + 2 * H_MIX
    b_end = b0 + 4 * D_MIX
    c0 = b_end + 2 * H_MIX
    main_ref[0, 0:a_end, :] = w_ref[0, 0:a_end, :].astype(BF16)
    main_ref[0, a_end:2 * a_end, :] = w_ref[0, b0:b_end, :].astype(BF16)
    main_ref[0, 2 * a_end:N_MAIN, :] = w_ref[0, c0:c0 + N_MAIN - 2 * a_end, :].astype(BF16)
    cols = w_ref.shape[2]
    gate_ref[0] = jnp.concatenate([w_ref[0, a_end:b0, :], w_ref[0, b_end:c0, :],
                                   jnp.zeros((LANES - 4 * H_MIX, cols), F32)], axis=0).astype(BF16)


def _pack_w_in(w_in_t, *, tc):
    depth, n_in, d = w_in_t.shape
    return pl.pallas_call(
        _pack_w_in_kernel,
        grid=(depth, d // tc),
        in_specs=[pl.BlockSpec((1, n_in, tc), lambda l, i: (l, 0, i))],
        out_specs=[pl.BlockSpec((1, N_MAIN, tc), lambda l, i: (l, 0, i)),
                   pl.BlockSpec((1, LANES, tc), lambda l, i: (l, 0, i))],
        out_shape=[jax.ShapeDtypeStruct((depth, N_MAIN, d), BF16),
                   jax.ShapeDtypeStruct((depth, LANES, d), BF16)],
        compiler_params=_cparams("parallel", "parallel"),
        name="pack_w_in",
    )(w_in_t)


def _proj_kernel(x_ref, w_ref, wg_ref, o_ref, g_ref, xb_ref):
    nt = (((1,), (1,)), ((), ()))

    @pl.when(pl.program_id(1) == 0)
    def _():
        xb_ref[...] = x_ref[...].astype(BF16)
        g_ref[...] = lax.dot_general(xb_ref[...], wg_ref[0], nt, preferred_element_type=F32)

    o_ref[...] = lax.dot_general(xb_ref[...], w_ref[0], nt, preferred_element_type=F32)


def _proj(x, w_main_t, w_gate_t, *, layer, tm, tn):
    m, d = x.shape
    n = w_main_t.shape[1]
    return pl.pallas_call(
        _proj_kernel,
        grid=(m // tm, n // tn),
        in_specs=[pl.BlockSpec((tm, d), lambda i, j: (i, 0)),
                  pl.BlockSpec((1, tn, d), lambda i, j: (layer, j, 0)),
                  pl.BlockSpec((1, LANES, d), lambda i, j: (layer, 0, 0))],
        out_specs=[pl.BlockSpec((tm, tn), lambda i, j: (i, j)),
                   pl.BlockSpec((tm, LANES), lambda i, j: (i, 0))],
        out_shape=[jax.ShapeDtypeStruct((m, n), F32), jax.ShapeDtypeStruct((m, LANES), F32)],
        scratch_shapes=[pltpu.VMEM((tm, d), BF16)],
        compiler_params=_cparams("parallel", "arbitrary"),
        name="proj",
    )(x, w_main_t, w_gate_t)


def _outproj_ln_kernel(alpha, a_ref, b_ref, c_ref, d_ref, w_ref, x_ref, lw_ref, lb_ref, o_ref, ob_ref):
    y = alpha * x_ref[...]
    for i, m_ref in enumerate((a_ref, b_ref, c_ref, d_ref)):
        y = y + jnp.dot(m_ref[...].astype(BF16), w_ref[0, i * D_MIX:(i + 1) * D_MIX, :],
                        preferred_element_type=F32)
    y = _layer_norm(y, lw_ref[...], lb_ref[...])
    o_ref[...] = y
    ob_ref[...] = y.astype(BF16)


def _outproj_ln(mixes, w_out_bf, x, ln_w, ln_b, *, layer, alpha, tm):
    m, d = x.shape
    mix_spec = pl.BlockSpec((tm, D_MIX), lambda i: (i, 0))
    return pl.pallas_call(
        functools.partial(_outproj_ln_kernel, alpha),
        grid=(m // tm,),
        in_specs=[mix_spec, mix_spec, mix_spec, mix_spec,
                  pl.BlockSpec((1, d, d), lambda i: (layer, 0, 0)),
                  pl.BlockSpec((tm, d), lambda i: (i, 0)),
                  pl.BlockSpec((1, d), lambda i: (0, 0)),
                  pl.BlockSpec((1, d), lambda i: (0, 0))],
        out_specs=[pl.BlockSpec((tm, d), lambda i: (i, 0)),
                   pl.BlockSpec((tm, d), lambda i: (i, 0))],
        out_shape=[jax.ShapeDtypeStruct((m, d), F32), jax.ShapeDtypeStruct((m, d), BF16)],
        compiler_params=_cparams("parallel"),
        name="outproj_ln",
    )(*mixes, w_out_bf, x, ln_w, ln_b)


def _ffn_ln_kernel(alpha, xb_ref, wu_ref, wd_ref, x_ref, lw_ref, lb_ref, o_ref, ob_ref, acc_ref):
    j = pl.program_id(1)

    @pl.when(j == 0)
    def _():
        acc_ref[...] = alpha * x_ref[...]

    h = jnp.dot(xb_ref[...], wu_ref[0], preferred_element_type=F32)
    h = jnp.maximum(h, 0.0)
    h = (h * h).astype(BF16)
    acc_ref[...] += jnp.dot(h, wd_ref[0], preferred_element_type=F32)

    @pl.when(j == pl.num_programs(1) - 1)
    def _():
        y = _layer_norm(acc_ref[...], lw_ref[...], lb_ref[...])
        o_ref[...] = y
        ob_ref[...] = y.astype(BF16)


def _ffn_ln(x_bf, w_up_bf, w_down_bf, x, ln_w, ln_b, *, layer, alpha, tm, tf):
    m, d = x.shape
    f = w_up_bf.shape[2]
    return pl.pallas_call(
        functools.partial(_ffn_ln_kernel, alpha),
        grid=(m // tm, f // tf),
        in_specs=[pl.BlockSpec((tm, d), lambda i, j: (i, 0)),
                  pl.BlockSpec((1, d, tf), lambda i, j: (layer, 0, j)),
                  pl.BlockSpec((1, tf, d), lambda i, j: (layer, j, 0)),
                  pl.BlockSpec((tm, d), lambda i, j: (i, 0)),
                  pl.BlockSpec((1, d), lambda i, j: (0, 0)),
                  pl.BlockSpec((1, d), lambda i, j: (0, 0))],
        out_specs=[pl.BlockSpec((tm, d), lambda i, j: (i, 0)),
                   pl.BlockSpec((tm, d), lambda i, j: (i, 0))],
        out_shape=[jax.ShapeDtypeStruct((m, d), F32), jax.ShapeDtypeStruct((m, d), BF16)],
        scratch_shapes=[pltpu.VMEM((tm, d), F32)],
        compiler_params=_cparams("parallel", "arbitrary"),
        name="ffn_ln",
    )(x_bf, w_up_bf, w_down_bf, x, ln_w, ln_b)


def _unit_lower_inverses(mats):
    n = mats[0].shape[0]
    r = lax.broadcasted_iota(jnp.int32, (n, n), 0)
    c = lax.broadcasted_iota(jnp.int32, (n, n), 1)
    eye = jnp.where(r == c, 1.0, 0.0).astype(F32)
    ts = [eye - a for a in mats]
    ps = [_split2(a) for a in mats]
    span = 2
    while span < n:
        ps = [_split2(_dot_split(p, p)) for p in ps]
        ts = [t + _dot_split(_split2(t), p) for t, p in zip(ts, ps)]
        span *= 2
    return ts


def _gdn_prompt_kernel(tb, cl, x_ref, g_ref, cw_ref, alog_ref, dtb_ref, nw_ref,
                       o_ref, s_out_ref, xbuf, s_ref):
    t = pl.program_id(1)
    dqkv = 3 * D_MIX

    @pl.when(t == 0)
    def _():
        xbuf[0:SUBLANES, :] = jnp.zeros((SUBLANES, dqkv), F32)
        s_ref[...] = jnp.zeros_like(s_ref)

    @pl.when(t > 0)
    def _():
        xbuf[0:SUBLANES, :] = xbuf[tb:tb + SUBLANES, :]

    xbuf[SUBLANES:SUBLANES + tb, :] = x_ref[:, 0:dqkv]
    cw = cw_ref[...]
    y = cw[0:1] * xbuf[SUBLANES - 3:SUBLANES - 3 + tb, :]
    for j in range(1, CONV_W):
        y = y + cw[j:j + 1] * xbuf[SUBLANES - 3 + j:SUBLANES - 3 + j + tb, :]
    y = _silu(y)

    gates = g_ref[...]
    beta_all = _sigmoid(gates)
    g_all = -jnp.exp(alog_ref[...]) * _softplus(gates + dtb_ref[...])
    tri, tri_s = _tri_masks(cl)
    tril_bf = jnp.where(tri, 1.0, 0.0).astype(BF16)
    nw = nw_ref[...]
    states = [s_ref[h] for h in range(H_MIX)]

    a_mats, pre = [], []
    for c in range(tb // cl):
        r0 = c * cl
        gcum = _cumsum_rows(tril_bf, g_all[r0:r0 + cl, :])
        gcum_t = gcum.T
        for h in range(H_MIX):
            q = y[r0:r0 + cl, COL_A_Q + h * HEAD_DIM:COL_A_Q + (h + 1) * HEAD_DIM]
            k = y[r0:r0 + cl, COL_A_K + h * HEAD_DIM:COL_A_K + (h + 1) * HEAD_DIM]
            v = y[r0:r0 + cl, COL_A_V + h * HEAD_DIM:COL_A_V + (h + 1) * HEAD_DIM]
            q = q * lax.rsqrt(jnp.sum(q * q, axis=-1, keepdims=True) + 1e-6) * (HEAD_DIM ** -0.5)
            k = k * lax.rsqrt(jnp.sum(k * k, axis=-1, keepdims=True) + 1e-6)
            beta = beta_all[r0:r0 + cl, GATE_A_BETA + h:GATE_A_BETA + h + 1]
            gc = gcum[:, GATE_A_DEC + h:GATE_A_DEC + h + 1]
            gr = gcum_t[GATE_A_DEC + h:GATE_A_DEC + h + 1, :]
            decay = jnp.exp(jnp.where(tri, gc - gr, NEG_BIG))
            kb = k * beta
            egc = jnp.exp(gc)
            gc_last = gc[cl - 1:cl, :]
            kk = _dot_nt(jnp.concatenate([kb, q], axis=0), k)
            a_mats.append(jnp.where(tri_s, kk[0:cl] * decay, 0.0))
            pre.append(dict(attn=jnp.where(tri, kk[cl:2 * cl] * decay, 0.0),
                            rhs=jnp.concatenate([v * beta, kb * egc], axis=1),
                            q_dec=q * egc, k_tail_t=(k * jnp.exp(gc_last - gc)).T, g_tot=jnp.exp(gc_last)))
    t_invs = _unit_lower_inverses(a_mats)
    uws = [_dot(t_inv, p["rhs"]) for t_inv, p in zip(t_invs, pre)]

    for c in range(tb // cl):
        r0 = c * cl
        for h in range(H_MIX):
            p, uw = pre[c * H_MIX + h], uws[c * H_MIX + h]
            z = x_ref[r0:r0 + cl, COL_A_Z + h * HEAD_DIM:COL_A_Z + (h + 1) * HEAD_DIM]
            s = states[h]
            ws_qs = _dot(jnp.concatenate([uw[:, HEAD_DIM:], p["q_dec"]], axis=0), s)
            v_new = uw[:, :HEAD_DIM] - ws_qs[0:cl]
            o = ws_qs[cl:2 * cl] + _dot(p["attn"], v_new)
            states[h] = s * p["g_tot"] + _dot(p["k_tail_t"], v_new)
            o = o * lax.rsqrt(jnp.mean(o * o, axis=-1, keepdims=True) + LN_EPS) * nw
            o_ref[r0:r0 + cl, h * HEAD_DIM:(h + 1) * HEAD_DIM] = (o * _silu(z)).astype(o_ref.dtype)

    for h in range(H_MIX):
        s_ref[h] = states[h]

    @pl.when(t == pl.num_programs(1) - 1)
    def _():
        s_out_ref[0] = s_ref[...]


def _gdn_prompt(proj, gates, conv_w, alog_row, dtb_row, nw_row, *, bsz, seqlen, tb):
    cl = math.gcd(seqlen, GDN_CHUNK)
    nt = seqlen // tb
    return pl.pallas_call(
        functools.partial(_gdn_prompt_kernel, tb, cl),
        grid=(bsz, nt),
        in_specs=[pl.BlockSpec((tb, 4 * D_MIX), lambda b, t: (b * nt + t, 0)),
                  pl.BlockSpec((tb, LANES), lambda b, t: (b * nt + t, 0)),
                  pl.BlockSpec((CONV_W, 3 * D_MIX), lambda b, t: (0, 0)),
                  pl.BlockSpec((1, LANES), lambda b, t: (0, 0)),
                  pl.BlockSpec((1, LANES), lambda b, t: (0, 0)),
                  pl.BlockSpec((1, HEAD_DIM), lambda b, t: (0, 0))],
        out_specs=[pl.BlockSpec((tb, D_MIX), lambda b, t: (b * nt + t, 0)),
                   pl.BlockSpec((1, H_MIX, HEAD_DIM, HEAD_DIM), lambda b, t: (b, 0, 0, 0))],
        out_shape=[jax.ShapeDtypeStruct((bsz * seqlen, D_MIX), BF16),
                   jax.ShapeDtypeStruct((bsz, H_MIX, HEAD_DIM, HEAD_DIM), F32)],
        scratch_shapes=[pltpu.VMEM((tb + SUBLANES, 3 * D_MIX), F32),
                        pltpu.VMEM((H_MIX, HEAD_DIM, HEAD_DIM), F32)],
        compiler_params=_cparams("parallel", "arbitrary"),
        name="gdn_prompt",
    )(proj, gates, conv_w, alog_row, dtb_row, nw_row)


def _mlstm_prompt_kernel(tb, cl, x_ref, g_ref, gbi_ref, gbf_ref,
                         o_ref, c_out_ref, n_out_ref, m_out_ref, c_ref, n_ref, m_ref):
    t = pl.program_id(1)

    @pl.when(t == 0)
    def _():
        c_ref[...] = jnp.zeros_like(c_ref)
        n_ref[...] = jnp.zeros_like(n_ref)
        m_ref[...] = jnp.zeros_like(m_ref)

    gates = g_ref[...]
    li_all = gates + gbi_ref[...]
    lf_all = _log_sigmoid(gates + gbf_ref[...])
    tri, _ = _tri_masks(cl)
    tril_bf = jnp.where(tri, 1.0, 0.0).astype(BF16)
    cs = [c_ref[h] for h in range(H_MIX)]
    ns = [n_ref[h:h + 1, :] for h in range(H_MIX)]
    ms = [m_ref[h:h + 1, 0:1] for h in range(H_MIX)]

    for c in range(tb // cl):
        r0 = c * cl
        bcum = _cumsum_rows(tril_bf, lf_all[r0:r0 + cl, :])
        bcum_t = bcum.T
        li_t = li_all[r0:r0 + cl, :].T
        for h in range(H_MIX):
            q = x_ref[r0:r0 + cl, h * HEAD_DIM:(h + 1) * HEAD_DIM]
            k = x_ref[r0:r0 + cl, D_MIX + h * HEAD_DIM:D_MIX + (h + 1) * HEAD_DIM] * (HEAD_DIM ** -0.5)
            v = x_ref[r0:r0 + cl, 2 * D_MIX + h * HEAD_DIM:2 * D_MIX + (h + 1) * HEAD_DIM]
            og = x_ref[r0:r0 + cl, 3 * D_MIX + h * HEAD_DIM:3 * D_MIX + (h + 1) * HEAD_DIM]
            b = bcum[:, GATE_B_F + h:GATE_B_F + h + 1]
            br = bcum_t[GATE_B_F + h:GATE_B_F + h + 1, :]
            ic = li_all[r0:r0 + cl, GATE_B_I + h:GATE_B_I + h + 1]
            ir = li_t[GATE_B_I + h:GATE_B_I + h + 1, :]
            m_prev = ms[h]
            dlog = jnp.where(tri, b - br + ir, NEG_BIG)
            inter = b + m_prev
            m = jnp.maximum(jnp.max(dlog, axis=1, keepdims=True), inter)
            s = _dot_nt(q, k) * jnp.exp(dlog - m)
            scale_prev = jnp.exp(inter - m)
            num = _dot(s, v) + scale_prev * _dot(q, cs[h])
            den = jnp.sum(s, axis=1, keepdims=True) + scale_prev * jnp.sum(q * ns[h], axis=1, keepdims=True)
            hh = num / jnp.maximum(jnp.abs(den), jnp.exp(-m))
            b_end = b[cl - 1:cl, :]
            wlog = b_end - b + ic
            m_new = jnp.maximum(b_end + m_prev, jnp.max(wlog, axis=0, keepdims=True))
            wk = jnp.exp(wlog - m_new) * k
            dec = jnp.exp(b_end + m_prev - m_new)
            cs[h] = dec * cs[h] + _dot_tn(wk, v)
            ns[h] = dec * ns[h] + jnp.sum(wk, axis=0, keepdims=True)
            ms[h] = m_new
            o_ref[r0:r0 + cl, h * HEAD_DIM:(h + 1) * HEAD_DIM] = (_sigmoid(og) * hh).astype(o_ref.dtype)

    for h in range(H_MIX):
        c_ref[h] = cs[h]
        n_ref[h:h + 1, :] = ns[h]
        m_ref[h:h + 1, :] = jnp.broadcast_to(ms[h], (1, LANES))

    @pl.when(t == pl.num_programs(1) - 1)
    def _():
        c_out_ref[0] = c_ref[...]
        n_out_ref[0] = n_ref[...]
        m_out_ref[0] = m_ref[...]


def _mlstm_prompt(proj, gates, gbi_row, gbf_row, *, bsz, seqlen, tb):
    cl = math.gcd(seqlen, MLSTM_CHUNK)
    nt = seqlen // tb
    return pl.pallas_call(
        functools.partial(_mlstm_prompt_kernel, tb, cl),
        grid=(bsz, nt),
        in_specs=[pl.BlockSpec((tb, 4 * D_MIX), lambda b, t: (b * nt + t, COL_B_Q // (4 * D_MIX))),
                  pl.BlockSpec((tb, LANES), lambda b, t: (b * nt + t, 0)),
                  pl.BlockSpec((1, LANES), lambda b, t: (0, 0)),
                  pl.BlockSpec((1, LANES), lambda b, t: (0, 0))],
        out_specs=[pl.BlockSpec((tb, D_MIX), lambda b, t: (b * nt + t, 0)),
                   pl.BlockSpec((1, H_MIX, HEAD_DIM, HEAD_DIM), lambda b, t: (b, 0, 0, 0)),
                   pl.BlockSpec((1, SUBLANES, HEAD_DIM), lambda b, t: (b, 0, 0)),
                   pl.BlockSpec((1, SUBLANES, LANES), lambda b, t: (b, 0, 0))],
        out_shape=[jax.ShapeDtypeStruct((bsz * seqlen, D_MIX), BF16),
                   jax.ShapeDtypeStruct((bsz, H_MIX, HEAD_DIM, HEAD_DIM), F32),
                   jax.ShapeDtypeStruct((bsz, SUBLANES, HEAD_DIM), F32),
                   jax.ShapeDtypeStruct((bsz, SUBLANES, LANES), F32)],
        scratch_shapes=[pltpu.VMEM((H_MIX, HEAD_DIM, HEAD_DIM), F32),
                        pltpu.VMEM((SUBLANES, HEAD_DIM), F32),
                        pltpu.VMEM((SUBLANES, LANES), F32)],
        compiler_params=_cparams("parallel", "arbitrary"),
        name="mlstm_prompt",
    )(proj, gates, gbi_row, gbf_row)


def _gmlp_gv(dv, nw):
    gv = _gelu_tanh(dv)
    mu = jnp.mean(gv, axis=-1, keepdims=True)
    gc = gv - mu
    var = jnp.mean(gc * gc, axis=-1, keepdims=True)
    return gc * lax.rsqrt(var + LN_EPS) * nw


def _gmlp_prompt_kernel(tb, u_ref, v_ref, nw_ref, ws_ref, bt_ref, o_ref):
    gu = _gelu_tanh(u_ref[...])
    gv = _gmlp_gv(v_ref[...], nw_ref[...])
    tri, _ = _tri_masks(GM_CHUNK)
    gw = D_MIX // GM_GROUPS
    for g in range(GM_GROUPS):
        wm = jnp.where(tri, ws_ref[g], 0.0)
        bias = bt_ref[:, g:g + 1]
        for c in range(tb // GM_CHUNK):
            r0 = c * GM_CHUNK
            z = _dot(wm, gv[r0:r0 + GM_CHUNK, g * gw:(g + 1) * gw]) + bias
            o_ref[r0:r0 + GM_CHUNK, g * gw:(g + 1) * gw] = (
                gu[r0:r0 + GM_CHUNK, g * gw:(g + 1) * gw] * z).astype(o_ref.dtype)


def _gmlp_prompt(proj, nw_row, ws, b_t, *, rows, tb):
    return pl.pallas_call(
        functools.partial(_gmlp_prompt_kernel, tb),
        grid=(rows // tb,),
        in_specs=[pl.BlockSpec((tb, D_MIX), lambda i: (i, COL_D_U // D_MIX)),
                  pl.BlockSpec((tb, D_MIX), lambda i: (i, COL_D_V // D_MIX)),
                  pl.BlockSpec((1, D_MIX), lambda i: (0, 0)),
                  pl.BlockSpec((GM_GROUPS, GM_CHUNK, GM_CHUNK), lambda i: (0, 0, 0)),
                  pl.BlockSpec((GM_CHUNK, GM_GROUPS), lambda i: (0, 0))],
        out_specs=pl.BlockSpec((tb, D_MIX), lambda i: (i, 0)),
        out_shape=jax.ShapeDtypeStruct((rows, D_MIX), BF16),
        compiler_params=_cparams("parallel"),
        name="gmlp_prompt",
    )(proj, proj, nw_row, ws, b_t)


KMEAN_ROWS = 128


def _rope_tables(pos):
    half = ROT_DIM // 2
    inv_freq = ROPE_THETA ** (-jnp.arange(half, dtype=F32) * (2.0 / ROT_DIM))
    ang = pos.astype(F32)[:, None] * inv_freq[None, :]
    cos, sin = jnp.cos(ang), jnp.sin(ang)
    rest = jnp.ones((pos.shape[0], HEAD_DIM - ROT_DIM), F32)
    return (jnp.concatenate([cos, cos, rest], axis=1),
            jnp.concatenate([-sin, sin, 0.0 * rest], axis=1))


def _rope(x, cos, sin):
    lane = lax.broadcasted_iota(jnp.int32, (x.shape[0], HEAD_DIM), 1)
    first_half = lane < ROT_DIM // 2
    outs = []
    for h in range(H_MIX):
        xh = x[:, h * HEAD_DIM:(h + 1) * HEAD_DIM]
        rot = jnp.where(first_half, pltpu.roll(xh, HEAD_DIM - ROT_DIM // 2, 1), pltpu.roll(xh, ROT_DIM // 2, 1))
        outs.append(xh * cos + rot * sin)
    return jnp.concatenate(outs, axis=1)


def _rope_kernel(q_ref, k_ref, v_ref, cos_ref, sin_ref, qo_ref, ko_ref, vo_ref, kb_ref, vb_ref, km_ref):
    t = pl.program_id(1)
    cos = cos_ref[...]
    sin = sin_ref[...]
    qo_ref[...] = _rope(q_ref[...], cos, sin)
    kr = _rope(k_ref[...], cos, sin)
    v = v_ref[...]
    for h in range(H_MIX):
        ko_ref[pl.ds(h, MOBA_BLOCK, stride=H_MIX), :] = kr[:, h * HEAD_DIM:(h + 1) * HEAD_DIM]
        vo_ref[pl.ds(h, MOBA_BLOCK, stride=H_MIX), :] = v[:, h * HEAD_DIM:(h + 1) * HEAD_DIM]
    kb_ref[...] = kr.astype(BF16)
    vb_ref[...] = v.astype(BF16)

    @pl.when(t == 0)
    def _():
        km_ref[...] = jnp.zeros_like(km_ref)

    km_ref[0, pl.ds(t, 1), :] = jnp.mean(kr, axis=0, keepdims=True)


def _rope_prompt(proj, cos_tab, sin_tab, *, bsz, seqlen):
    nb = seqlen // MOBA_BLOCK
    col = lambda c: pl.BlockSpec((MOBA_BLOCK, D_MIX), lambda b, t: (b * nb + t, c // D_MIX))
    row = pl.BlockSpec((MOBA_BLOCK, D_MIX), lambda b, t: (b * nb + t, 0))
    head_rows = pl.BlockSpec((MOBA_BLOCK * H_MIX, HEAD_DIM), lambda b, t: (b * nb + t, 0))
    tab = pl.BlockSpec((MOBA_BLOCK, HEAD_DIM), lambda b, t: (t, 0))
    return pl.pallas_call(
        _rope_kernel,
        grid=(bsz, nb),
        in_specs=[col(COL_C_Q), col(COL_C_K), col(COL_C_V), tab, tab],
        out_specs=[row, head_rows, head_rows, row, row,
                   pl.BlockSpec((1, KMEAN_ROWS, D_MIX), lambda b, t: (b, 0, 0))],
        out_shape=[jax.ShapeDtypeStruct((bsz * seqlen, D_MIX), F32),
                   jax.ShapeDtypeStruct((bsz * seqlen * H_MIX, HEAD_DIM), F32),
                   jax.ShapeDtypeStruct((bsz * seqlen * H_MIX, HEAD_DIM), F32),
                   jax.ShapeDtypeStruct((bsz * seqlen, D_MIX), BF16),
                   jax.ShapeDtypeStruct((bsz * seqlen, D_MIX), BF16),
                   jax.ShapeDtypeStruct((bsz, KMEAN_ROWS, D_MIX), F32)],
        compiler_params=_cparams("parallel", "arbitrary"),
        name="rope_prompt",
    )(proj, proj, proj, cos_tab, sin_tab)


def _topk_block_rows(gate_t, n_valid):
    row = lax.broadcasted_iota(jnp.int32, gate_t.shape, 0)
    neg_inf = float("-inf")
    g = jnp.where(row < n_valid, gate_t, neg_inf)
    sel = jnp.zeros(gate_t.shape, F32)
    for _ in range(MOBA_TOPK):
        mx = jnp.max(g, axis=0, keepdims=True)
        first = jnp.min(jnp.where(g == mx, row, KMEAN_ROWS), axis=0, keepdims=True)
        pick = (row == first) & (mx > neg_inf)
        sel = jnp.where(pick, 1.0, sel)
        g = jnp.where(pick, neg_inf, g)
    return sel


def _moba_prompt_kernel(nb_pad, q_ref, k_ref, v_ref, km_ref, o_ref):
    qt = pl.program_id(1)
    blk = MOBA_BLOCK
    span = 2 * blk
    tri, _ = _tri_masks(blk)
    own0 = pl.multiple_of(qt * blk, blk)
    heads = [slice(h * HEAD_DIM, (h + 1) * HEAD_DIM) for h in range(H_MIX)]

    gates_t = [_dot_nt_hi(km_ref[0][0:nb_pad, hs], q_ref[:, hs]) for hs in heads]
    qhs = [(q_ref[:, hs] * (HEAD_DIM ** -0.5)).astype(BF16) for hs in heads]
    own_s = [jnp.where(tri, _dot_nt(qhs[h], k_ref[pl.ds(own0, blk), hs]), NEG_BIG) for h, hs in enumerate(heads)]
    own_m = [jnp.max(s, axis=1, keepdims=True) for s in own_s]
    own_p = [jnp.exp(s - m) for s, m in zip(own_s, own_m)]
    own_pv = [_dot(own_p[h], v_ref[pl.ds(own0, blk), hs]) for h, hs in enumerate(heads)]
    q_augs, carry = [], []
    for h in range(H_MIX):
        sel_t = _topk_block_rows(gates_t[h], qt)
        unpicked_t = jnp.concatenate([1.0 - sel_t, jnp.ones((KMEAN_ROWS - nb_pad, blk), F32)], axis=0)
        q_augs.append(jnp.concatenate([qhs[h], unpicked_t.T.astype(BF16)], axis=1))
        carry += [own_m[h], jnp.sum(own_p[h], axis=1, keepdims=True), own_pv[h]]

    def body(j, carry):
        r0 = pl.multiple_of(j * span, span)
        key_block = 2 * j + lax.broadcasted_iota(jnp.int32, (span, KMEAN_ROWS), 0) // blk
        block_bias = jnp.where(lax.broadcasted_iota(jnp.int32, (span, KMEAN_ROWS), 1) == key_block,
                               NEG_BIG, 0.0).astype(BF16)
        ss = [lax.dot_general(q_augs[h], jnp.concatenate([k_ref[pl.ds(r0, span), hs], block_bias], axis=1),
                              (((1,), (1,)), ((), ())), preferred_element_type=F32)
              for h, hs in enumerate(heads)]
        m_news = [jnp.maximum(carry[3 * h], jnp.max(ss[h], axis=1, keepdims=True)) for h in range(H_MIX)]
        ps = [jnp.exp(ss[h] - m_news[h]) for h in range(H_MIX)]
        pvs = [_dot(ps[h], v_ref[pl.ds(r0, span), hs]) for h, hs in enumerate(heads)]
        out = []
        for h in range(H_MIX):
            alpha = jnp.exp(carry[3 * h] - m_news[h])
            out += [m_news[h], alpha * carry[3 * h + 1] + jnp.sum(ps[h], axis=1, keepdims=True),
                    alpha * carry[3 * h + 2] + pvs[h]]
        return tuple(out)

    carry = lax.fori_loop(0, (qt + 1) // 2, body, tuple(carry))
    for h, hs in enumerate(heads):
        o_ref[:, hs] = (carry[3 * h + 2] / carry[3 * h + 1]).astype(o_ref.dtype)


def _moba_prompt(q_rope, k_bf, v_bf, kmean, *, bsz, seqlen):
    nb = seqlen // MOBA_BLOCK
    nb_pad = -(-nb // SUBLANES) * SUBLANES
    assert nb_pad <= KMEAN_ROWS
    return pl.pallas_call(
        functools.partial(_moba_prompt_kernel, nb_pad),
        grid=(bsz, nb),
        in_specs=[pl.BlockSpec((MOBA_BLOCK, D_MIX), lambda b, t: (b * nb + t, 0)),
                  pl.BlockSpec((seqlen, D_MIX), lambda b, t: (b, 0)),
                  pl.BlockSpec((seqlen, D_MIX), lambda b, t: (b, 0)),
                  pl.BlockSpec((1, KMEAN_ROWS, D_MIX), lambda b, t: (b, 0, 0))],
        out_specs=pl.BlockSpec((MOBA_BLOCK, D_MIX), lambda b, t: (b * nb + t, 0)),
        out_shape=jax.ShapeDtypeStruct((bsz * seqlen, D_MIX), BF16),
        compiler_params=_cparams("parallel", "arbitrary"),
        name="moba_prompt",
    )(q_rope, k_bf, v_bf, kmean)


SAMPLE_GROUP = SUBLANES


def _columns(rows):
    pad = jnp.zeros((HEAD_DIM - SAMPLE_GROUP, HEAD_DIM), F32)
    return jnp.concatenate([rows, pad], axis=0).T


def _sample_state_kernel(p_ref, g_ref, conv_ref, s_ref, c_ref, n_ref, m_ref,
                         cw_ref, alog_ref, dtb_ref, nw_ref, gbi_ref, gbf_ref,
                         gmnw_ref, gmw0_ref, gmb0_ref, cos_ref, sin_ref,
                         oa_ref, ob_ref, od_ref, convo_ref, so_ref, co_ref, no_ref, mo_ref,
                         qr_ref, kr_ref, gv_ref):
    bg = SAMPLE_GROUP
    gates = g_ref[...]

    cw = cw_ref[...]
    xa = p_ref[:, COL_A_Q:COL_A_Q + 3 * D_MIX]
    y = cw[CONV_W - 1:CONV_W] * xa
    for j in range(CONV_W - 1):
        y = y + cw[j:j + 1] * conv_ref[j]
    y = _silu(y)
    for j in range(CONV_W - 2):
        convo_ref[j] = conv_ref[j + 1]
    convo_ref[CONV_W - 2] = xa
    beta_all = _sigmoid(gates)
    eg_all = jnp.exp(-jnp.exp(alog_ref[...]) * _softplus(gates + dtb_ref[...]))
    nw = nw_ref[...]
    for h in range(H_MIX):
        hs = slice(h * HEAD_DIM, (h + 1) * HEAD_DIM)
        q = y[:, COL_A_Q + h * HEAD_DIM:COL_A_Q + (h + 1) * HEAD_DIM]
        k = y[:, COL_A_K + h * HEAD_DIM:COL_A_K + (h + 1) * HEAD_DIM]
        v = y[:, COL_A_V + h * HEAD_DIM:COL_A_V + (h + 1) * HEAD_DIM]
        z = p_ref[:, COL_A_Z + h * HEAD_DIM:COL_A_Z + (h + 1) * HEAD_DIM]
        q = q * lax.rsqrt(jnp.sum(q * q, axis=-1, keepdims=True) + 1e-6) * (HEAD_DIM ** -0.5)
        k = k * lax.rsqrt(jnp.sum(k * k, axis=-1, keepdims=True) + 1e-6)
        beta = beta_all[:, GATE_A_BETA + h:GATE_A_BETA + h + 1]
        eg = eg_all[:, GATE_A_DEC + h:GATE_A_DEC + h + 1]
        qk = jnp.sum(q * k, axis=-1, keepdims=True)
        kt = _columns(k)
        qt = _columns(q)
        for i in range(bg):
            s = s_ref[i, h]
            kcol = kt[:, i:i + 1]
            e_i = eg[i:i + 1, :]
            ks = jnp.sum(kcol * s, axis=0, keepdims=True)
            qs = jnp.sum(qt[:, i:i + 1] * s, axis=0, keepdims=True)
            v_new = beta[i:i + 1, :] * (v[i:i + 1, :] - e_i * ks)
            oa_ref[i:i + 1, hs] = e_i * qs + qk[i:i + 1, :] * v_new
            so_ref[i, h] = e_i * s + kcol * v_new
        o = oa_ref[:, hs]
        o = o * lax.rsqrt(jnp.mean(o * o, axis=-1, keepdims=True) + LN_EPS) * nw
        oa_ref[:, hs] = o * _silu(z)

    li_all = gates + gbi_ref[...]
    lf_all = _log_sigmoid(gates + gbf_ref[...])
    for h in range(H_MIX):
        hs = slice(h * HEAD_DIM, (h + 1) * HEAD_DIM)
        q = p_ref[:, COL_B_Q + h * HEAD_DIM:COL_B_Q + (h + 1) * HEAD_DIM]
        k = p_ref[:, COL_B_K + h * HEAD_DIM:COL_B_K + (h + 1) * HEAD_DIM] * (HEAD_DIM ** -0.5)
        v = p_ref[:, COL_B_V + h * HEAD_DIM:COL_B_V + (h + 1) * HEAD_DIM]
        og = p_ref[:, COL_B_O + h * HEAD_DIM:COL_B_O + (h + 1) * HEAD_DIM]
        li = li_all[:, GATE_B_I + h:GATE_B_I + h + 1]
        lf = lf_all[:, GATE_B_F + h:GATE_B_F + h + 1]
        m_prev = m_ref[:, h:h + 1]
        n_prev = n_ref[:, hs]
        m_new = jnp.maximum(lf + m_prev, li)
        w_in = jnp.exp(li - m_new)
        dec = jnp.exp(lf + m_prev - m_new)
        sc = jnp.sum(q * k, axis=-1, keepdims=True) * w_in
        den = sc + dec * jnp.sum(q * n_prev, axis=-1, keepdims=True)
        denom = jnp.maximum(jnp.abs(den), jnp.exp(-m_new))
        no_ref[:, hs] = dec * n_prev + w_in * k
        mo_ref[:, h:h + 1] = m_new
        kt = _columns(k)
        qt = _columns(q)
        wv = w_in * v
        for i in range(bg):
            c = c_ref[i, h]
            d_i = dec[i:i + 1, :]
            qc = jnp.sum(qt[:, i:i + 1] * c, axis=0, keepdims=True)
            ob_ref[i:i + 1, hs] = sc[i:i + 1, :] * v[i:i + 1, :] + d_i * qc
            co_ref[i, h] = d_i * c + kt[:, i:i + 1] * wv[i:i + 1, :]
        ob_ref[:, hs] = _sigmoid(og) * (ob_ref[:, hs] / denom)

    gu = _gelu_tanh(p_ref[:, COL_D_U:COL_D_U + D_MIX])
    gv = _gmlp_gv(p_ref[:, COL_D_V:COL_D_V + D_MIX], gmnw_ref[...])
    gv_ref[...] = gv
    od_ref[...] = gu * (gmw0_ref[...] * gv + gmb0_ref[...])

    cos = cos_ref[...]
    sin = sin_ref[...]
    qr_ref[...] = _rope(p_ref[:, COL_C_Q:COL_C_Q + D_MIX], cos, sin)
    kr_ref[...] = _rope(p_ref[:, COL_C_K:COL_C_K + D_MIX], cos, sin)


def _sample_state(proj, gates, conv_t, s0, c0, n0, m0, conv_w, alog_row, dtb_row, nw_row, gbi_row, gbf_row,
                  gm_nw_row, gm_w0_row, gm_b0_row, cos_row, sin_row, *, layer):
    bg = SAMPLE_GROUP
    nrows = proj.shape[0]
    full = lambda shape: pl.BlockSpec(shape, lambda i: (0,) * len(shape))
    row_blk = lambda w: pl.BlockSpec((bg, w), lambda i: (i, 0))
    mat_blk = pl.BlockSpec((bg, H_MIX, HEAD_DIM, HEAD_DIM), lambda i: (i, 0, 0, 0))
    mat_in = pl.BlockSpec((None, bg, H_MIX, HEAD_DIM, HEAD_DIM), lambda i: (layer, i, 0, 0, 0))
    conv_blk = pl.BlockSpec((CONV_W - 1, bg, 3 * D_MIX), lambda i: (0, i, 0))
    f = lambda shape: jax.ShapeDtypeStruct(shape, F32)
    return pl.pallas_call(
        _sample_state_kernel,
        grid=(nrows // bg,),
        in_specs=[row_blk(N_MAIN), row_blk(LANES),
                  conv_blk, mat_in, mat_in, row_blk(D_MIX), row_blk(H_MIX),
                  full((CONV_W, 3 * D_MIX)), full((1, LANES)), full((1, LANES)), full((1, HEAD_DIM)),
                  full((1, LANES)), full((1, LANES)),
                  full((1, D_MIX)), full((1, D_MIX)), full((1, D_MIX)),
                  full((1, HEAD_DIM)), full((1, HEAD_DIM))],
        out_specs=[row_blk(D_MIX), row_blk(D_MIX), row_blk(D_MIX), conv_blk, mat_blk, mat_blk,
                   row_blk(D_MIX), row_blk(H_MIX), row_blk(D_MIX), row_blk(D_MIX), row_blk(D_MIX)],
        out_shape=[f((nrows, D_MIX)), f((nrows, D_MIX)), f((nrows, D_MIX)),
                   f((CONV_W - 1, nrows, 3 * D_MIX)),
                   f((nrows, H_MIX, HEAD_DIM, HEAD_DIM)), f((nrows, H_MIX, HEAD_DIM, HEAD_DIM)),
                   f((nrows, D_MIX)), f((nrows, H_MIX)),
                   f((nrows, D_MIX)), f((nrows, D_MIX)), f((nrows, D_MIX))],
        compiler_params=_cparams("parallel"),
        name="sample_state",
    )(proj, gates, conv_t, s0, c0, n0, m0, conv_w, alog_row, dtb_row, nw_row, gbi_row, gbf_row,
      gm_nw_row, gm_w0_row, gm_b0_row, cos_row, sin_row)


def _head_rows(row):
    sub = lax.broadcasted_iota(jnp.int32, (SUBLANES, HEAD_DIM), 0)
    out = jnp.zeros((SUBLANES, HEAD_DIM), F32)
    for h in range(H_MIX):
        out = jnp.where(sub == h, jnp.broadcast_to(row[:, h * HEAD_DIM:(h + 1) * HEAD_DIM], (SUBLANES, HEAD_DIM)), out)
    return out


def _moba_decode_kernel(n_pages, page_rows, pt_ref, q_ref, kn_ref, vn_ref, *refs):
    del pt_ref
    k_refs = refs[:n_pages]
    v_refs = refs[n_pages:2 * n_pages]
    o_ref = refs[2 * n_pages]
    page_size = page_rows // H_MIX
    pages_per_block = MOBA_BLOCK // page_size
    n_blocks = n_pages // pages_per_block
    scale = HEAD_DIM ** -0.5

    q8 = _head_rows(q_ref[0])
    q8_bf = q8.astype(BF16)
    sub = lax.broadcasted_iota(jnp.int32, (SUBLANES, page_rows), 0)
    col = lax.broadcasted_iota(jnp.int32, (SUBLANES, page_rows), 1)
    head_mask = (col % H_MIX) == sub
    sub_d = lax.broadcasted_iota(jnp.int32, (SUBLANES, HEAD_DIM), 0)

    gates = []
    for j in range(n_blocks):
        fold = jnp.zeros((SUBLANES, HEAD_DIM), F32)
        for p in range(j * pages_per_block, (j + 1) * pages_per_block):
            fold = fold + jnp.sum(k_refs[p][0, 0].reshape(page_rows // SUBLANES, SUBLANES, HEAD_DIM), axis=0)
        ksum = fold
        for g in range(1, SUBLANES // H_MIX):
            ksum = ksum + pltpu.roll(fold, SUBLANES - g * H_MIX, 0)
        ksum = jnp.where(sub_d < H_MIX, ksum, 0.0)
        gates.append(jnp.sum(q8 * ksum, axis=1, keepdims=True) * (1.0 / MOBA_BLOCK))
    sels = []
    for n in range(n_blocks):
        rank = jnp.zeros((SUBLANES, 1), F32)
        for m in range(n_blocks):
            if m == n:
                continue
            ahead = gates[m] > gates[n]
            if m < n:
                ahead = ahead | (gates[m] == gates[n])
            rank = rank + jnp.where(ahead, 1.0, 0.0)
        sels.append(rank < float(MOBA_TOPK))

    keeps = [head_mask & sels[p // pages_per_block] for p in range(n_pages)]
    scores = [jnp.where(keeps[p],
                        lax.dot_general(q8_bf, k_refs[p][0, 0].astype(BF16), (((1,), (1,)), ((), ())),
                                        preferred_element_type=F32) * scale, NEG_BIG)
              for p in range(n_pages)]
    s_own = jnp.sum(q8 * _head_rows(kn_ref[0]), axis=1, keepdims=True) * scale
    m_all = s_own
    for s in scores:
        m_all = jnp.maximum(m_all, jnp.max(s, axis=1, keepdims=True))
    probs = [jnp.where(keeps[p], jnp.exp(scores[p] - m_all), 0.0) for p in range(n_pages)]
    e_own = jnp.exp(s_own - m_all)
    l_all = e_own
    acc = e_own * _head_rows(vn_ref[0])
    for p in range(n_pages):
        l_all = l_all + jnp.sum(probs[p], axis=1, keepdims=True)
        acc = acc + _dot(probs[p], v_refs[p][0, 0])
    out = acc / l_all
    for h in range(H_MIX):
        o_ref[0, :, h * HEAD_DIM:(h + 1) * HEAD_DIM] = out[h:h + 1, :]


def _moba_decode(page_table, q_rope, k_new, v_new, cache_k, cache_v, *, layer):
    bsz, n_pages = page_table.shape
    page_rows = cache_k.shape[2]
    page_size = page_rows // H_MIX
    assert MOBA_BLOCK % page_size == 0 and (n_pages * page_size) % MOBA_BLOCK == 0
    assert (n_pages * page_size) // MOBA_BLOCK >= MOBA_TOPK and SUBLANES % H_MIX == 0
    row = pl.BlockSpec((1, 1, D_MIX), lambda b, pt: (b, 0, 0))
    page_specs = [pl.BlockSpec((1, 1, page_rows, HEAD_DIM), lambda b, pt, p=p: (layer, pt[b, p], 0, 0))
                  for p in range(n_pages)]
    return pl.pallas_call(
        functools.partial(_moba_decode_kernel, n_pages, page_rows),
        grid_spec=pltpu.PrefetchScalarGridSpec(
            num_scalar_prefetch=1,
            grid=(bsz,),
            in_specs=[row, row, row] + page_specs + page_specs,
            out_specs=row),
        out_shape=jax.ShapeDtypeStruct((bsz, 1, D_MIX), F32),
        compiler_params=_cparams("parallel"),
        name="moba_decode",
    )(page_table, q_rope, k_new, v_new, *([cache_k] * n_pages), *([cache_v] * n_pages))


def _largest_divisor(n, candidates):
    for c in candidates:
        if n % c == 0:
            return c
    raise ValueError(f"no tile in {candidates} divides {n}")


def _tiles(rows, seqlen):
    return dict(
        proj_tm=_largest_divisor(rows, (1024, 512, 256, 128)),
        proj_tn=512,
        out_tm=_largest_divisor(rows, (512, 256, 128)),
        ffn_tm=_largest_divisor(rows, (512, 256, 128)),
        ffn_tf=1024,
        scan_tb=_largest_divisor(seqlen, (256, 128, 64, 32, 16, 8)),
        gmlp_tb=_largest_divisor(seqlen, (512, 256, 128)),
    )


def _lane_row(vals, offset):
    return jnp.zeros((1, LANES), F32).at[0, offset:offset + vals.shape[0]].set(vals.astype(F32))


def kernel(x_prompt, x_sample, state_gdn_conv, state_gdn_s, state_mlstm_c, state_mlstm_n, state_mlstm_m,
           cache_k, cache_v, page_table, w_in, gdn_conv_w, gdn_a_log, gdn_dt_bias, gdn_norm_w, mlstm_gate_b,
           gmlp_norm_w, gmlp_ws, gmlp_b, w_out, ln1_w, ln1_b, w_up, w_down, ln2_w, ln2_b):
    bsz, seq, d_model = x_prompt.shape
    dec_b, dec_s, _ = x_sample.shape
    depth = w_in.shape[0]
    assert dec_s == 1 and d_model == N_MIXERS * D_MIX and seq >= CONV_W - 1
    assert seq % MOBA_BLOCK == 0 and seq // MOBA_BLOCK <= KMEAN_ROWS and seq % GM_CHUNK == 0
    assert w_in.shape[2] == N_MAIN + 4 * H_MIX and cache_k.shape[3] == H_MIX
    alpha = (2.0 * depth) ** 0.25
    mp = bsz * seq
    past_len = page_table.shape[1] * cache_k.shape[2]
    n_pool, page_size = cache_k.shape[1], cache_k.shape[2]
    tp = _tiles(mp, seq)
    ts = _tiles(dec_b, seq)
    gw = D_MIX // GM_GROUPS

    cos_p, sin_p = _rope_tables(jnp.arange(seq, dtype=jnp.int32))
    cos_s, sin_s = _rope_tables(past_len + jnp.arange(dec_s, dtype=jnp.int32))
    ck = cache_k.reshape(depth, n_pool, page_size * H_MIX, HEAD_DIM)
    cv = cache_v.reshape(depth, n_pool, page_size * H_MIX, HEAD_DIM)

    w_main, w_gate = _pack_w_in(jnp.transpose(w_in.astype(F32), (0, 2, 1)), tc=_largest_divisor(d_model, (256, 128)))
    w_out_bf = w_out.astype(BF16)
    w_up_bf = w_up.astype(BF16)
    w_down_bf = w_down.astype(BF16)
    gdn_s_in = state_gdn_s.astype(F32)
    ml_c_in = state_mlstm_c.astype(F32)

    xp = x_prompt.reshape(mp, d_model)
    xs = x_sample.reshape(dec_b, d_model)
    xp_in, xs_in = xp, xs
    p_st, s_st = [], []
    for l in range(depth):
        alog_row = _lane_row(gdn_a_log[l], GATE_A_DEC)
        dtb_row = _lane_row(gdn_dt_bias[l], GATE_A_DEC)
        nw_row = gdn_norm_w[l].reshape(1, HEAD_DIM).astype(F32)
        gbi_row = _lane_row(mlstm_gate_b[l][:H_MIX], GATE_B_I)
        gbf_row = _lane_row(mlstm_gate_b[l][H_MIX:], GATE_B_F)
        gm_nw_row = gmlp_norm_w[l].reshape(1, D_MIX).astype(F32)
        conv_w = gdn_conv_w[l].astype(F32)
        ln1 = (ln1_w[l].reshape(1, d_model).astype(F32), ln1_b[l].reshape(1, d_model).astype(F32))
        ln2 = (ln2_w[l].reshape(1, d_model).astype(F32), ln2_b[l].reshape(1, d_model).astype(F32))

        proj, gates = _proj(xp_in, w_main, w_gate, layer=l, tm=tp["proj_tm"], tn=tp["proj_tn"])
        oa, gdn_s_p = _gdn_prompt(proj, gates, conv_w, alog_row, dtb_row, nw_row,
                                  bsz=bsz, seqlen=seq, tb=tp["scan_tb"])
        ob, ml_c_p, ml_n_p, ml_m_p = _mlstm_prompt(proj, gates, gbi_row, gbf_row,
                                                   bsz=bsz, seqlen=seq, tb=tp["scan_tb"])
        q_rope, k_rows, v_rows, k_bf, v_bf, kmean = _rope_prompt(proj, cos_p, sin_p, bsz=bsz, seqlen=seq)
        oc = _moba_prompt(q_rope, k_bf, v_bf, kmean, bsz=bsz, seqlen=seq)
        od = _gmlp_prompt(proj, gm_nw_row, gmlp_ws[l].astype(F32), gmlp_b[l].astype(F32).T,
                          rows=mp, tb=tp["gmlp_tb"])
        x1, x1_bf = _outproj_ln((oa, ob, oc, od), w_out_bf, xp, *ln1, layer=l, alpha=alpha, tm=tp["out_tm"])
        xp, xp_in = _ffn_ln(x1_bf, w_up_bf, w_down_bf, x1, *ln2, layer=l, alpha=alpha,
                            tm=tp["ffn_tm"], tf=tp["ffn_tf"])
        proj_p = proj.reshape(bsz, seq, N_MAIN)
        p_st.append((proj_p[:, seq - (CONV_W - 1):, COL_A_Q:COL_A_Q + 3 * D_MIX],
                     gdn_s_p, ml_c_p, ml_n_p[:, :H_MIX], ml_m_p[:, :H_MIX, 0],
                     k_rows.reshape(bsz, seq, H_MIX, HEAD_DIM), v_rows.reshape(bsz, seq, H_MIX, HEAD_DIM)))

        proj, gates = _proj(xs_in, w_main, w_gate, layer=l, tm=ts["proj_tm"], tn=ts["proj_tn"])
        (oa, ob, od, conv_s, gdn_s_s, ml_c_s, ml_n_s, ml_m_s, q_s, k_s, gv_s) = _sample_state(
            proj, gates, jnp.transpose(state_gdn_conv[l].astype(F32), (1, 0, 2)),
            gdn_s_in, ml_c_in,
            state_mlstm_n[l].astype(F32).reshape(dec_b, D_MIX), state_mlstm_m[l].astype(F32),
            conv_w, alog_row, dtb_row, nw_row, gbi_row, gbf_row, gm_nw_row,
            jnp.repeat(gmlp_ws[l][:, 0, 0].astype(F32), gw).reshape(1, D_MIX),
            jnp.repeat(gmlp_b[l][:, 0].astype(F32), gw).reshape(1, D_MIX),
            cos_s, sin_s, layer=l)
        v_s = proj[:, COL_C_V:COL_C_V + D_MIX]
        oc = _moba_decode(page_table, q_s.reshape(dec_b, 1, D_MIX), k_s.reshape(dec_b, 1, D_MIX),
                          v_s.reshape(dec_b, 1, D_MIX), ck, cv, layer=l).reshape(dec_b, D_MIX)
        x1, x1_bf = _outproj_ln((oa, ob, oc, od), w_out_bf, xs, *ln1, layer=l, alpha=alpha, tm=ts["out_tm"])
        xs, xs_in = _ffn_ln(x1_bf, w_up_bf, w_down_bf, x1, *ln2, layer=l, alpha=alpha,
                            tm=ts["ffn_tm"], tf=ts["ffn_tf"])
        s_st.append((jnp.transpose(conv_s, (1, 0, 2)), gdn_s_s, ml_c_s,
                     ml_n_s.reshape(dec_b, H_MIX, HEAD_DIM), ml_m_s,
                     k_s.reshape(dec_b, dec_s, H_MIX, HEAD_DIM), v_s.reshape(dec_b, dec_s, H_MIX, HEAD_DIM),
                     gv_s.reshape(dec_b, dec_s, D_MIX)))

    def stk(sts, i):
        return jnp.stack([s[i] for s in sts], axis=0)

    dt = x_prompt.dtype
    yp = xp.reshape(bsz, seq, d_model).astype(dt)
    ys = xs.reshape(dec_b, dec_s, d_model).astype(dt)
    return (yp, ys) + tuple(stk(p_st, i).astype(dt) for i in range(7)) + tuple(stk(s_st, i).astype(dt) for i in range(8))
```

```python
import functools
import math

import jax
import jax.numpy as jnp
from jax import lax
from jax.experimental import pallas as pl
from jax.experimental.pallas import tpu as pltpu

F32 = jnp.float32
BF16 = jnp.bfloat16
HIGHEST = lax.Precision.HIGHEST

HEAD_DIM = 128
N_MIXERS = 4
CONV_W = 4
GDN_CHUNK = 64
MLSTM_CHUNK = 64
MOBA_BLOCK = 256
MOBA_TOPK = 3
GM_CHUNK = 128
GM_GROUPS = 4
ROPE_THETA = 500000.0
ROT_DIM = HEAD_DIM // 4
LN_EPS = 1e-5
NEG_BIG = -1e30

LANES = 128
SUBLANES = 8
VMEM_LIMIT_BYTES = 56 * 1024 * 1024


def _cparams(*sem):
    return pltpu.CompilerParams(dimension_semantics=sem, vmem_limit_bytes=VMEM_LIMIT_BYTES)


def _dot(a, b):
    return jnp.dot(a.astype(BF16), b.astype(BF16), preferred_element_type=F32)


def _dot_nt(a, b):
    return lax.dot_general(a.astype(BF16), b.astype(BF16), (((1,), (1,)), ((), ())),
                           preferred_element_type=F32)


def _dot_tn(a, b):
    return lax.dot_general(a.astype(BF16), b.astype(BF16), (((0,), (0,)), ((), ())),
                           preferred_element_type=F32)


def _dot_hi(a, b):
    return jnp.dot(a, b, precision=HIGHEST, preferred_element_type=F32)


def _dot_nt_hi(a, b):
    return lax.dot_general(a, b, (((1,), (1,)), ((), ())), precision=HIGHEST,
                           preferred_element_type=F32)


def _split2(x):
    hi = x.astype(BF16)
    return hi, (x - hi.astype(F32)).astype(BF16)


def _dot_split(a_parts, b_parts):
    ah, al = a_parts
    bh, bl = b_parts
    return (jnp.dot(ah, bh, preferred_element_type=F32) + jnp.dot(ah, bl, preferred_element_type=F32)
            + jnp.dot(al, bh, preferred_element_type=F32))


def _cumsum_rows(tril_bf, x):
    x0 = x.astype(BF16)
    r1 = x - x0.astype(F32)
    x1 = r1.astype(BF16)
    x2 = (r1 - x1.astype(F32)).astype(BF16)
    return (jnp.dot(tril_bf, x0, preferred_element_type=F32) + jnp.dot(tril_bf, x1, preferred_element_type=F32)
            + jnp.dot(tril_bf, x2, preferred_element_type=F32))


def _sigmoid(x):
    return 1.0 / (1.0 + jnp.exp(-x))


def _silu(x):
    return x * _sigmoid(x)


def _softplus(x):
    return jnp.maximum(x, 0.0) + jnp.log(1.0 + jnp.exp(-jnp.abs(x)))


def _log_sigmoid(x):
    return -_softplus(-x)


def _gelu_tanh(x):
    return 0.5 * x * (1.0 + jnp.tanh(math.sqrt(2.0 / math.pi) * (x + 0.044715 * (x * x * x))))


def _layer_norm(x, w, b):
    mu = jnp.mean(x, axis=-1, keepdims=True)
    xc = x - mu
    var = jnp.mean(xc * xc, axis=-1, keepdims=True)
    return xc * lax.rsqrt(var + LN_EPS) * w + b


D_MIX = 512
H_MIX = D_MIX // HEAD_DIM
COL_A_Q, COL_A_K, COL_A_V, COL_A_Z = 0, 512, 1024, 1536
COL_B_Q, COL_B_K, COL_B_V, COL_B_O = 2048, 2560, 3072, 3584
COL_C_Q, COL_C_K, COL_C_V = 4096, 4608, 5120
COL_D_U, COL_D_V = 5632, 6144
N_MAIN = 6656
GATE_A_BETA, GATE_A_DEC, GATE_B_I, GATE_B_F = 0, 4, 8, 12


def _layer_param(shape, layer):
    return pl.BlockSpec((None,) + tuple(shape), lambda *_: (layer,) + (0,) * len(shape))


def _tri_masks(n):
    r = lax.broadcasted_iota(jnp.int32, (n, n), 0)
    c = lax.broadcasted_iota(jnp.int32, (n, n), 1)
    return r >= c, r > c


def _pack_w_in_kernel(w_ref, main_ref, gate_ref):
    a_end = 4 * D_MIX
    b0 = a_end + 2 * H_MIX
    b_end = b0 + 4 * D_MIX
    c0 = b_end + 2 * H_MIX
    main_ref[0, 0:a_end, :] = w_ref[0, 0:a_end, :].astype(BF16)
    main_ref[0, a_end:2 * a_end, :] = w_ref[0, b0:b_end, :].astype(BF16)
    main_ref[0, 2 * a_end:N_MAIN, :] = w_ref[0, c0:c0 + N_MAIN - 2 * a_end, :].astype(BF16)
    cols = w_ref.shape[2]
    gate_ref[0] = jnp.concatenate([w_ref[0, a_end:b0, :], w_ref[0, b_end:c0, :],
                                   jnp.zeros((LANES - 4 * H_MIX, cols), F32)], axis=0).astype(BF16)


def _pack_w_in(w_in_t, *, tc):
    depth, n_in, d = w_in_t.shape
    return pl.pallas_call(
        _pack_w_in_kernel,
        grid=(depth, d // tc),
        in_specs=[pl.BlockSpec((1, n_in, tc), lambda l, i: (l, 0, i))],
        out_specs=[pl.BlockSpec((1, N_MAIN, tc), lambda l, i: (l, 0, i)),
                   pl.BlockSpec((1, LANES, tc), lambda l, i: (l, 0, i))],
        out_shape=[jax.ShapeDtypeStruct((depth, N_MAIN, d), BF16),
                   jax.ShapeDtypeStruct((depth, LANES, d), BF16)],
        compiler_params=_cparams("parallel", "parallel"),
        name="pack_w_in",
    )(w_in_t)


def _proj_kernel(x_ref, w_ref, wg_ref, o_ref, g_ref, xb_ref):
    nt = (((1,), (1,)), ((), ()))

    @pl.when(pl.program_id(1) == 0)
    def _():
        xb_ref[...] = x_ref[...].astype(BF16)
        g_ref[...] = lax.dot_general(xb_ref[...], wg_ref[0], nt, preferred_element_type=F32)

    o_ref[...] = lax.dot_general(xb_ref[...], w_ref[0], nt, preferred_element_type=F32)


def _proj(x, w_main_t, w_gate_t, *, layer, tm, tn):
    m, d = x.shape
    n = w_main_t.shape[1]
    return pl.pallas_call(
        _proj_kernel,
        grid=(m // tm, n // tn),
        in_specs=[pl.BlockSpec((tm, d), lambda i, j: (i, 0)),
                  pl.BlockSpec((1, tn, d), lambda i, j: (layer, j, 0)),
                  pl.BlockSpec((1, LANES, d), lambda i, j: (layer, 0, 0))],
        out_specs=[pl.BlockSpec((tm, tn), lambda i, j: (i, j)),
                   pl.BlockSpec((tm, LANES), lambda i, j: (i, 0))],
        out_shape=[jax.ShapeDtypeStruct((m, n), F32), jax.ShapeDtypeStruct((m, LANES), F32)],
        scratch_shapes=[pltpu.VMEM((tm, d), BF16)],
        compiler_params=_cparams("parallel", "arbitrary"),
        name="proj",
    )(x, w_main_t, w_gate_t)


def _outproj_ln_kernel(alpha, a_ref, b_ref, c_ref, d_ref, w_ref, x_ref, lw_ref, lb_ref, o_ref, ob_ref):
    y = alpha * x_ref[...]
    for i, m_ref in enumerate((a_ref, b_ref, c_ref, d_ref)):
        y = y + jnp.dot(m_ref[...].astype(BF16), w_ref[0, i * D_MIX:(i + 1) * D_MIX, :],
                        preferred_element_type=F32)
    y = _layer_norm(y, lw_ref[...], lb_ref[...])
    o_ref[...] = y
    ob_ref[...] = y.astype(BF16)


def _outproj_ln(mixes, w_out_bf, x, ln_w, ln_b, *, layer, alpha, tm):
    m, d = x.shape
    mix_spec = pl.BlockSpec((tm, D_MIX), lambda i: (i, 0))
    return pl.pallas_call(
        functools.partial(_outproj_ln_kernel, alpha),
        grid=(m // tm,),
        in_specs=[mix_spec, mix_spec, mix_spec, mix_spec,
                  pl.BlockSpec((1, d, d), lambda i: (layer, 0, 0)),
                  pl.BlockSpec((tm, d), lambda i: (i, 0)),
                  _layer_param((1, d), layer), _layer_param((1, d), layer)],
        out_specs=[pl.BlockSpec((tm, d), lambda i: (i, 0)),
                   pl.BlockSpec((tm, d), lambda i: (i, 0))],
        out_shape=[jax.ShapeDtypeStruct((m, d), F32), jax.ShapeDtypeStruct((m, d), BF16)],
        compiler_params=_cparams("parallel"),
        name="outproj_ln",
    )(*mixes, w_out_bf, x, ln_w, ln_b)


def _ffn_ln_kernel(alpha, xb_ref, wu_ref, wd_ref, x_ref, lw_ref, lb_ref, o_ref, ob_ref, acc_ref):
    j = pl.program_id(1)

    @pl.when(j == 0)
    def _():
        acc_ref[...] = alpha * x_ref[...]

    h = jnp.dot(xb_ref[...], wu_ref[0], preferred_element_type=F32)
    h = jnp.maximum(h, 0.0)
    h = (h * h).astype(BF16)
    acc_ref[...] += jnp.dot(h, wd_ref[0], preferred_element_type=F32)

    @pl.when(j == pl.num_programs(1) - 1)
    def _():
        y = _layer_norm(acc_ref[...], lw_ref[...], lb_ref[...])
        o_ref[...] = y
        ob_ref[...] = y.astype(BF16)


def _ffn_ln(x_bf, w_up_bf, w_down_bf, x, ln_w, ln_b, *, layer, alpha, tm, tf):
    m, d = x.shape
    f = w_up_bf.shape[2]
    return pl.pallas_call(
        functools.partial(_ffn_ln_kernel, alpha),
        grid=(m // tm, f // tf),
        in_specs=[pl.BlockSpec((tm, d), lambda i, j: (i, 0)),
                  pl.BlockSpec((1, d, tf), lambda i, j: (layer, 0, j)),
                  pl.BlockSpec((1, tf, d), lambda i, j: (layer, j, 0)),
                  pl.BlockSpec((tm, d), lambda i, j: (i, 0)),
                  _layer_param((1, d), layer), _layer_param((1, d), layer)],
        out_specs=[pl.BlockSpec((tm, d), lambda i, j: (i, 0)),
                   pl.BlockSpec((tm, d), lambda i, j: (i, 0))],
        out_shape=[jax.ShapeDtypeStruct((m, d), F32), jax.ShapeDtypeStruct((m, d), BF16)],
        scratch_shapes=[pltpu.VMEM((tm, d), F32)],
        compiler_params=_cparams("parallel", "arbitrary"),
        name="ffn_ln",
    )(x_bf, w_up_bf, w_down_bf, x, ln_w, ln_b)


def _unit_lower_inverses(mats):
    n = mats[0].shape[0]
    r = lax.broadcasted_iota(jnp.int32, (n, n), 0)
    c = lax.broadcasted_iota(jnp.int32, (n, n), 1)
    eye = jnp.where(r == c, 1.0, 0.0).astype(F32)
    ts = [eye - a for a in mats]
    ps = [_split2(a) for a in mats]
    span = 2
    while span < n:
        ps = [_split2(_dot_split(p, p)) for p in ps]
        ts = [t + _dot_split(_split2(t), p) for t, p in zip(ts, ps)]
        span *= 2
    return ts


def _gdn_prompt_kernel(tb, cl, x_ref, g_ref, cw_ref, alog_ref, dtb_ref, nw_ref,
                       o_ref, s_out_ref, xbuf, s_ref):
    t = pl.program_id(1)
    dqkv = 3 * D_MIX

    @pl.when(t == 0)
    def _():
        xbuf[0:SUBLANES, :] = jnp.zeros((SUBLANES, dqkv), F32)
        s_ref[...] = jnp.zeros_like(s_ref)

    @pl.when(t > 0)
    def _():
        xbuf[0:SUBLANES, :] = xbuf[tb:tb + SUBLANES, :]

    xbuf[SUBLANES:SUBLANES + tb, :] = x_ref[:, 0:dqkv]
    cw = cw_ref[...]
    y = cw[0:1] * xbuf[SUBLANES - 3:SUBLANES - 3 + tb, :]
    for j in range(1, CONV_W):
        y = y + cw[j:j + 1] * xbuf[SUBLANES - 3 + j:SUBLANES - 3 + j + tb, :]
    y = _silu(y)

    gates = g_ref[...]
    beta_all = _sigmoid(gates)
    g_all = -jnp.exp(alog_ref[...]) * _softplus(gates + dtb_ref[...])
    tri, tri_s = _tri_masks(cl)
    tril_bf = jnp.where(tri, 1.0, 0.0).astype(BF16)
    nw = nw_ref[...]
    states = [s_ref[h] for h in range(H_MIX)]

    a_mats, pre = [], []
    for c in range(tb // cl):
        r0 = c * cl
        gcum = _cumsum_rows(tril_bf, g_all[r0:r0 + cl, :])
        gcum_t = gcum.T
        for h in range(H_MIX):
            q = y[r0:r0 + cl, COL_A_Q + h * HEAD_DIM:COL_A_Q + (h + 1) * HEAD_DIM]
            k = y[r0:r0 + cl, COL_A_K + h * HEAD_DIM:COL_A_K + (h + 1) * HEAD_DIM]
            v = y[r0:r0 + cl, COL_A_V + h * HEAD_DIM:COL_A_V + (h + 1) * HEAD_DIM]
            q = q * lax.rsqrt(jnp.sum(q * q, axis=-1, keepdims=True) + 1e-6) * (HEAD_DIM ** -0.5)
            k = k * lax.rsqrt(jnp.sum(k * k, axis=-1, keepdims=True) + 1e-6)
            beta = beta_all[r0:r0 + cl, GATE_A_BETA + h:GATE_A_BETA + h + 1]
            gc = gcum[:, GATE_A_DEC + h:GATE_A_DEC + h + 1]
            gr = gcum_t[GATE_A_DEC + h:GATE_A_DEC + h + 1, :]
            decay = jnp.exp(jnp.where(tri, gc - gr, NEG_BIG))
            kb = k * beta
            egc = jnp.exp(gc)
            gc_last = gc[cl - 1:cl, :]
            kk = _dot_nt(jnp.concatenate([kb, q], axis=0), k)
            a_mats.append(jnp.where(tri_s, kk[0:cl] * decay, 0.0))
            pre.append(dict(attn=jnp.where(tri, kk[cl:2 * cl] * decay, 0.0),
                            rhs=jnp.concatenate([v * beta, kb * egc], axis=1),
                            q_dec=q * egc, k_tail_t=(k * jnp.exp(gc_last - gc)).T, g_tot=jnp.exp(gc_last)))
    t_invs = _unit_lower_inverses(a_mats)
    uws = [_dot(t_inv, p["rhs"]) for t_inv, p in zip(t_invs, pre)]

    for c in range(tb // cl):
        r0 = c * cl
        for h in range(H_MIX):
            p, uw = pre[c * H_MIX + h], uws[c * H_MIX + h]
            z = x_ref[r0:r0 + cl, COL_A_Z + h * HEAD_DIM:COL_A_Z + (h + 1) * HEAD_DIM]
            s = states[h]
            ws_qs = _dot(jnp.concatenate([uw[:, HEAD_DIM:], p["q_dec"]], axis=0), s)
            v_new = uw[:, :HEAD_DIM] - ws_qs[0:cl]
            o = ws_qs[cl:2 * cl] + _dot(p["attn"], v_new)
            states[h] = s * p["g_tot"] + _dot(p["k_tail_t"], v_new)
            o = o * lax.rsqrt(jnp.mean(o * o, axis=-1, keepdims=True) + LN_EPS) * nw
            o_ref[r0:r0 + cl, h * HEAD_DIM:(h + 1) * HEAD_DIM] = (o * _silu(z)).astype(o_ref.dtype)

    for h in range(H_MIX):
        s_ref[h] = states[h]

    @pl.when(t == pl.num_programs(1) - 1)
    def _():
        s_out_ref[0] = s_ref[...]


def _gdn_prompt(proj, gates, conv_w, alog_row, dtb_row, nw_row, *, layer, bsz, seqlen, tb):
    cl = math.gcd(seqlen, GDN_CHUNK)
    nt = seqlen // tb
    return pl.pallas_call(
        functools.partial(_gdn_prompt_kernel, tb, cl),
        grid=(bsz, nt),
        in_specs=[pl.BlockSpec((tb, 4 * D_MIX), lambda b, t: (b * nt + t, 0)),
                  pl.BlockSpec((tb, LANES), lambda b, t: (b * nt + t, 0)),
                  _layer_param((CONV_W, 3 * D_MIX), layer),
                  _layer_param((1, LANES), layer), _layer_param((1, LANES), layer),
                  _layer_param((1, HEAD_DIM), layer)],
        out_specs=[pl.BlockSpec((tb, D_MIX), lambda b, t: (b * nt + t, 0)),
                   pl.BlockSpec((1, H_MIX, HEAD_DIM, HEAD_DIM), lambda b, t: (b, 0, 0, 0))],
        out_shape=[jax.ShapeDtypeStruct((bsz * seqlen, D_MIX), BF16),
                   jax.ShapeDtypeStruct((bsz, H_MIX, HEAD_DIM, HEAD_DIM), F32)],
        scratch_shapes=[pltpu.VMEM((tb + SUBLANES, 3 * D_MIX), F32),
                        pltpu.VMEM((H_MIX, HEAD_DIM, HEAD_DIM), F32)],
        compiler_params=_cparams("parallel", "arbitrary"),
        name="gdn_prompt",
    )(proj, gates, conv_w, alog_row, dtb_row, nw_row)


def _mlstm_prompt_kernel(tb, cl, x_ref, g_ref, gbi_ref, gbf_ref,
                         o_ref, c_out_ref, n_out_ref, m_out_ref, c_ref, n_ref, m_ref):
    t = pl.program_id(1)

    @pl.when(t == 0)
    def _():
        c_ref[...] = jnp.zeros_like(c_ref)
        n_ref[...] = jnp.zeros_like(n_ref)
        m_ref[...] = jnp.zeros_like(m_ref)

    gates = g_ref[...]
    li_all = gates + gbi_ref[...]
    lf_all = _log_sigmoid(gates + gbf_ref[...])
    tri, _ = _tri_masks(cl)
    tril_bf = jnp.where(tri, 1.0, 0.0).astype(BF16)
    cs = [c_ref[h] for h in range(H_MIX)]
    ns = [n_ref[h:h + 1, :] for h in range(H_MIX)]
    ms = [m_ref[h:h + 1, 0:1] for h in range(H_MIX)]

    for c in range(tb // cl):
        r0 = c * cl
        bcum = _cumsum_rows(tril_bf, lf_all[r0:r0 + cl, :])
        bcum_t = bcum.T
        li_t = li_all[r0:r0 + cl, :].T
        for h in range(H_MIX):
            q = x_ref[r0:r0 + cl, h * HEAD_DIM:(h + 1) * HEAD_DIM]
            k = x_ref[r0:r0 + cl, D_MIX + h * HEAD_DIM:D_MIX + (h + 1) * HEAD_DIM] * (HEAD_DIM ** -0.5)
            v = x_ref[r0:r0 + cl, 2 * D_MIX + h * HEAD_DIM:2 * D_MIX + (h + 1) * HEAD_DIM]
            og = x_ref[r0:r0 + cl, 3 * D_MIX + h * HEAD_DIM:3 * D_MIX + (h + 1) * HEAD_DIM]
            b = bcum[:, GATE_B_F + h:GATE_B_F + h + 1]
            br = bcum_t[GATE_B_F + h:GATE_B_F + h + 1, :]
            ic = li_all[r0:r0 + cl, GATE_B_I + h:GATE_B_I + h + 1]
            ir = li_t[GATE_B_I + h:GATE_B_I + h + 1, :]
            m_prev = ms[h]
            dlog = jnp.where(tri, b - br + ir, NEG_BIG)
            inter = b + m_prev
            m = jnp.maximum(jnp.max(dlog, axis=1, keepdims=True), inter)
            s = _dot_nt(q, k) * jnp.exp(dlog - m)
            scale_prev = jnp.exp(inter - m)
            num = _dot(s, v) + scale_prev * _dot(q, cs[h])
            den = jnp.sum(s, axis=1, keepdims=True) + scale_prev * jnp.sum(q * ns[h], axis=1, keepdims=True)
            hh = num / jnp.maximum(jnp.abs(den), jnp.exp(-m))
            b_end = b[cl - 1:cl, :]
            wlog = b_end - b + ic
            m_new = jnp.maximum(b_end + m_prev, jnp.max(wlog, axis=0, keepdims=True))
            wk = jnp.exp(wlog - m_new) * k
            dec = jnp.exp(b_end + m_prev - m_new)
            cs[h] = dec * cs[h] + _dot_tn(wk, v)
            ns[h] = dec * ns[h] + jnp.sum(wk, axis=0, keepdims=True)
            ms[h] = m_new
            o_ref[r0:r0 + cl, h * HEAD_DIM:(h + 1) * HEAD_DIM] = (_sigmoid(og) * hh).astype(o_ref.dtype)

    for h in range(H_MIX):
        c_ref[h] = cs[h]
        n_ref[h:h + 1, :] = ns[h]
        m_ref[h:h + 1, :] = jnp.broadcast_to(ms[h], (1, LANES))

    @pl.when(t == pl.num_programs(1) - 1)
    def _():
        c_out_ref[0] = c_ref[...]
        n_out_ref[0] = n_ref[...]
        m_out_ref[0] = m_ref[...]


def _mlstm_prompt(proj, gates, gbi_row, gbf_row, *, layer, bsz, seqlen, tb):
    cl = math.gcd(seqlen, MLSTM_CHUNK)
    nt = seqlen // tb
    return pl.pallas_call(
        functools.partial(_mlstm_prompt_kernel, tb, cl),
        grid=(bsz, nt),
        in_specs=[pl.BlockSpec((tb, 4 * D_MIX), lambda b, t: (b * nt + t, COL_B_Q // (4 * D_MIX))),
                  pl.BlockSpec((tb, LANES), lambda b, t: (b * nt + t, 0)),
                  _layer_param((1, LANES), layer), _layer_param((1, LANES), layer)],
        out_specs=[pl.BlockSpec((tb, D_MIX), lambda b, t: (b * nt + t, 0)),
                   pl.BlockSpec((1, H_MIX, HEAD_DIM, HEAD_DIM), lambda b, t: (b, 0, 0, 0)),
                   pl.BlockSpec((1, SUBLANES, HEAD_DIM), lambda b, t: (b, 0, 0)),
                   pl.BlockSpec((1, SUBLANES, LANES), lambda b, t: (b, 0, 0))],
        out_shape=[jax.ShapeDtypeStruct((bsz * seqlen, D_MIX), BF16),
                   jax.ShapeDtypeStruct((bsz, H_MIX, HEAD_DIM, HEAD_DIM), F32),
                   jax.ShapeDtypeStruct((bsz, SUBLANES, HEAD_DIM), F32),
                   jax.ShapeDtypeStruct((bsz, SUBLANES, LANES), F32)],
        scratch_shapes=[pltpu.VMEM((H_MIX, HEAD_DIM, HEAD_DIM), F32),
                        pltpu.VMEM((SUBLANES, HEAD_DIM), F32),
                        pltpu.VMEM((SUBLANES, LANES), F32)],
        compiler_params=_cparams("parallel", "arbitrary"),
        name="mlstm_prompt",
    )(proj, gates, gbi_row, gbf_row)


def _gmlp_gv(dv, nw):
    gv = _gelu_tanh(dv)
    mu = jnp.mean(gv, axis=-1, keepdims=True)
    gc = gv - mu
    var = jnp.mean(gc * gc, axis=-1, keepdims=True)
    return gc * lax.rsqrt(var + LN_EPS) * nw


def _gmlp_prompt_kernel(tb, u_ref, v_ref, nw_ref, ws_ref, bt_ref, o_ref):
    gu = _gelu_tanh(u_ref[...])
    gv = _gmlp_gv(v_ref[...], nw_ref[...])
    tri, _ = _tri_masks(GM_CHUNK)
    gw = D_MIX // GM_GROUPS
    for g in range(GM_GROUPS):
        wm = jnp.where(tri, ws_ref[g], 0.0)
        bias = bt_ref[:, g:g + 1]
        for c in range(tb // GM_CHUNK):
            r0 = c * GM_CHUNK
            z = _dot(wm, gv[r0:r0 + GM_CHUNK, g * gw:(g + 1) * gw]) + bias
            o_ref[r0:r0 + GM_CHUNK, g * gw:(g + 1) * gw] = (
                gu[r0:r0 + GM_CHUNK, g * gw:(g + 1) * gw] * z).astype(o_ref.dtype)


def _gmlp_prompt(proj, nw_row, ws, b_t, *, layer, rows, tb):
    return pl.pallas_call(
        functools.partial(_gmlp_prompt_kernel, tb),
        grid=(rows // tb,),
        in_specs=[pl.BlockSpec((tb, D_MIX), lambda i: (i, COL_D_U // D_MIX)),
                  pl.BlockSpec((tb, D_MIX), lambda i: (i, COL_D_V // D_MIX)),
                  _layer_param((1, D_MIX), layer),
                  _layer_param((GM_GROUPS, GM_CHUNK, GM_CHUNK), layer),
                  _layer_param((GM_CHUNK, GM_GROUPS), layer)],
        out_specs=pl.BlockSpec((tb, D_MIX), lambda i: (i, 0)),
        out_shape=jax.ShapeDtypeStruct((rows, D_MIX), BF16),
        compiler_params=_cparams("parallel"),
        name="gmlp_prompt",
    )(proj, proj, nw_row, ws, b_t)


KMEAN_ROWS = 128


def _rope_tables(pos):
    half = ROT_DIM // 2
    inv_freq = ROPE_THETA ** (-jnp.arange(half, dtype=F32) * (2.0 / ROT_DIM))
    ang = pos.astype(F32)[:, None] * inv_freq[None, :]
    cos, sin = jnp.cos(ang), jnp.sin(ang)
    rest = jnp.ones((pos.shape[0], HEAD_DIM - ROT_DIM), F32)
    return (jnp.concatenate([cos, cos, rest], axis=1),
            jnp.concatenate([-sin, sin, 0.0 * rest], axis=1))


def _rope(x, cos, sin):
    lane = lax.broadcasted_iota(jnp.int32, (x.shape[0], HEAD_DIM), 1)
    first_half = lane < ROT_DIM // 2
    outs = []
    for h in range(H_MIX):
        xh = x[:, h * HEAD_DIM:(h + 1) * HEAD_DIM]
        rot = jnp.where(first_half, pltpu.roll(xh, HEAD_DIM - ROT_DIM // 2, 1), pltpu.roll(xh, ROT_DIM // 2, 1))
        outs.append(xh * cos + rot * sin)
    return jnp.concatenate(outs, axis=1)


def _rope_kernel(q_ref, k_ref, v_ref, cos_ref, sin_ref, qo_ref, ko_ref, vo_ref, kb_ref, vb_ref, km_ref):
    t = pl.program_id(1)
    cos = cos_ref[...]
    sin = sin_ref[...]
    qo_ref[...] = _rope(q_ref[...], cos, sin)
    kr = _rope(k_ref[...], cos, sin)
    v = v_ref[...]
    for h in range(H_MIX):
        ko_ref[pl.ds(h, MOBA_BLOCK, stride=H_MIX), :] = kr[:, h * HEAD_DIM:(h + 1) * HEAD_DIM]
        vo_ref[pl.ds(h, MOBA_BLOCK, stride=H_MIX), :] = v[:, h * HEAD_DIM:(h + 1) * HEAD_DIM]
    kb_ref[...] = kr.astype(BF16)
    vb_ref[...] = v.astype(BF16)

    @pl.when(t == 0)
    def _():
        km_ref[...] = jnp.zeros_like(km_ref)

    km_ref[0, pl.ds(t, 1), :] = jnp.mean(kr, axis=0, keepdims=True)


def _rope_prompt(proj, cos_tab, sin_tab, *, bsz, seqlen):
    nb = seqlen // MOBA_BLOCK
    col = lambda c: pl.BlockSpec((MOBA_BLOCK, D_MIX), lambda b, t: (b * nb + t, c // D_MIX))
    row = pl.BlockSpec((MOBA_BLOCK, D_MIX), lambda b, t: (b * nb + t, 0))
    head_rows = pl.BlockSpec((MOBA_BLOCK * H_MIX, HEAD_DIM), lambda b, t: (b * nb + t, 0))
    tab = pl.BlockSpec((MOBA_BLOCK, HEAD_DIM), lambda b, t: (t, 0))
    return pl.pallas_call(
        _rope_kernel,
        grid=(bsz, nb),
        in_specs=[col(COL_C_Q), col(COL_C_K), col(COL_C_V), tab, tab],
        out_specs=[row, head_rows, head_rows, row, row,
                   pl.BlockSpec((1, KMEAN_ROWS, D_MIX), lambda b, t: (b, 0, 0))],
        out_shape=[jax.ShapeDtypeStruct((bsz * seqlen, D_MIX), F32),
                   jax.ShapeDtypeStruct((bsz * seqlen * H_MIX, HEAD_DIM), F32),
                   jax.ShapeDtypeStruct((bsz * seqlen * H_MIX, HEAD_DIM), F32),
                   jax.ShapeDtypeStruct((bsz * seqlen, D_MIX), BF16),
                   jax.ShapeDtypeStruct((bsz * seqlen, D_MIX), BF16),
                   jax.ShapeDtypeStruct((bsz, KMEAN_ROWS, D_MIX), F32)],
        compiler_params=_cparams("parallel", "arbitrary"),
        name="rope_prompt",
    )(proj, proj, proj, cos_tab, sin_tab)


def _topk_block_rows(gate_t, n_valid):
    row = lax.broadcasted_iota(jnp.int32, gate_t.shape, 0)
    neg_inf = float("-inf")
    g = jnp.where(row < n_valid, gate_t, neg_inf)
    sel = jnp.zeros(gate_t.shape, F32)
    for _ in range(MOBA_TOPK):
        mx = jnp.max(g, axis=0, keepdims=True)
        first = jnp.min(jnp.where(g == mx, row, KMEAN_ROWS), axis=0, keepdims=True)
        pick = (row == first) & (mx > neg_inf)
        sel = jnp.where(pick, 1.0, sel)
        g = jnp.where(pick, neg_inf, g)
    return sel


def _moba_prompt_kernel(nb_pad, q_ref, k_ref, v_ref, km_ref, o_ref):
    qt = pl.program_id(1)
    blk = MOBA_BLOCK
    span = 2 * blk
    tri, _ = _tri_masks(blk)
    own0 = pl.multiple_of(qt * blk, blk)
    heads = [slice(h * HEAD_DIM, (h + 1) * HEAD_DIM) for h in range(H_MIX)]

    gates_t = [_dot_nt_hi(km_ref[0][0:nb_pad, hs], q_ref[:, hs]) for hs in heads]
    qhs = [(q_ref[:, hs] * (HEAD_DIM ** -0.5)).astype(BF16) for hs in heads]
    own_s = [jnp.where(tri, _dot_nt(qhs[h], k_ref[pl.ds(own0, blk), hs]), NEG_BIG) for h, hs in enumerate(heads)]
    own_m = [jnp.max(s, axis=1, keepdims=True) for s in own_s]
    own_p = [jnp.exp(s - m) for s, m in zip(own_s, own_m)]
    own_pv = [_dot(own_p[h], v_ref[pl.ds(own0, blk), hs]) for h, hs in enumerate(heads)]
    q_augs, carry = [], []
    for h in range(H_MIX):
        sel_t = _topk_block_rows(gates_t[h], qt)
        unpicked_t = jnp.concatenate([1.0 - sel_t, jnp.ones((KMEAN_ROWS - nb_pad, blk), F32)], axis=0)
        q_augs.append(jnp.concatenate([qhs[h], unpicked_t.T.astype(BF16)], axis=1))
        carry += [own_m[h], jnp.sum(own_p[h], axis=1, keepdims=True), own_pv[h]]

    def body(j, carry):
        r0 = pl.multiple_of(j * span, span)
        key_block = 2 * j + lax.broadcasted_iota(jnp.int32, (span, KMEAN_ROWS), 0) // blk
        block_bias = jnp.where(lax.broadcasted_iota(jnp.int32, (span, KMEAN_ROWS), 1) == key_block,
                               NEG_BIG, 0.0).astype(BF16)
        ss = [lax.dot_general(q_augs[h], jnp.concatenate([k_ref[pl.ds(r0, span), hs], block_bias], axis=1),
                              (((1,), (1,)), ((), ())), preferred_element_type=F32)
              for h, hs in enumerate(heads)]
        m_news = [jnp.maximum(carry[3 * h], jnp.max(ss[h], axis=1, keepdims=True)) for h in range(H_MIX)]
        ps = [jnp.exp(ss[h] - m_news[h]) for h in range(H_MIX)]
        pvs = [_dot(ps[h], v_ref[pl.ds(r0, span), hs]) for h, hs in enumerate(heads)]
        out = []
        for h in range(H_MIX):
            alpha = jnp.exp(carry[3 * h] - m_news[h])
            out += [m_news[h], alpha * carry[3 * h + 1] + jnp.sum(ps[h], axis=1, keepdims=True),
                    alpha * carry[3 * h + 2] + pvs[h]]
        return tuple(out)

    carry = lax.fori_loop(0, (qt + 1) // 2, body, tuple(carry))
    for h, hs in enumerate(heads):
        o_ref[:, hs] = (carry[3 * h + 2] / carry[3 * h + 1]).astype(o_ref.dtype)


def _moba_prompt(q_rope, k_bf, v_bf, kmean, *, bsz, seqlen):
    nb = seqlen // MOBA_BLOCK
    nb_pad = -(-nb // SUBLANES) * SUBLANES
    assert nb_pad <= KMEAN_ROWS
    return pl.pallas_call(
        functools.partial(_moba_prompt_kernel, nb_pad),
        grid=(bsz, nb),
        in_specs=[pl.BlockSpec((MOBA_BLOCK, D_MIX), lambda b, t: (b * nb + t, 0)),
                  pl.BlockSpec((seqlen, D_MIX), lambda b, t: (b, 0)),
                  pl.BlockSpec((seqlen, D_MIX), lambda b, t: (b, 0)),
                  pl.BlockSpec((1, KMEAN_ROWS, D_MIX), lambda b, t: (b, 0, 0))],
        out_specs=pl.BlockSpec((MOBA_BLOCK, D_MIX), lambda b, t: (b * nb + t, 0)),
        out_shape=jax.ShapeDtypeStruct((bsz * seqlen, D_MIX), BF16),
        compiler_params=_cparams("parallel", "arbitrary"),
        name="moba_prompt",
    )(q_rope, k_bf, v_bf, kmean)


SAMPLE_GROUP = SUBLANES


def _columns(rows):
    pad = jnp.zeros((HEAD_DIM - SAMPLE_GROUP, HEAD_DIM), F32)
    return jnp.concatenate([rows, pad], axis=0).T


def _sample_state_kernel(p_ref, g_ref, conv_ref, s_ref, c_ref, n_ref, m_ref,
                         cw_ref, alog_ref, dtb_ref, nw_ref, gbi_ref, gbf_ref,
                         gmnw_ref, gmw0_ref, gmb0_ref, cos_ref, sin_ref,
                         oa_ref, ob_ref, od_ref, convo_ref, so_ref, co_ref, no_ref, mo_ref,
                         qr_ref, kr_ref, gv_ref):
    bg = SAMPLE_GROUP
    gates = g_ref[...]

    cw = cw_ref[...]
    xa = p_ref[:, COL_A_Q:COL_A_Q + 3 * D_MIX]
    y = cw[CONV_W - 1:CONV_W] * xa
    for j in range(CONV_W - 1):
        y = y + cw[j:j + 1] * conv_ref[j]
    y = _silu(y)
    for j in range(CONV_W - 2):
        convo_ref[j] = conv_ref[j + 1]
    convo_ref[CONV_W - 2] = xa
    beta_all = _sigmoid(gates)
    eg_all = jnp.exp(-jnp.exp(alog_ref[...]) * _softplus(gates + dtb_ref[...]))
    nw = nw_ref[...]
    for h in range(H_MIX):
        hs = slice(h * HEAD_DIM, (h + 1) * HEAD_DIM)
        q = y[:, COL_A_Q + h * HEAD_DIM:COL_A_Q + (h + 1) * HEAD_DIM]
        k = y[:, COL_A_K + h * HEAD_DIM:COL_A_K + (h + 1) * HEAD_DIM]
        v = y[:, COL_A_V + h * HEAD_DIM:COL_A_V + (h + 1) * HEAD_DIM]
        z = p_ref[:, COL_A_Z + h * HEAD_DIM:COL_A_Z + (h + 1) * HEAD_DIM]
        q = q * lax.rsqrt(jnp.sum(q * q, axis=-1, keepdims=True) + 1e-6) * (HEAD_DIM ** -0.5)
        k = k * lax.rsqrt(jnp.sum(k * k, axis=-1, keepdims=True) + 1e-6)
        beta = beta_all[:, GATE_A_BETA + h:GATE_A_BETA + h + 1]
        eg = eg_all[:, GATE_A_DEC + h:GATE_A_DEC + h + 1]
        qk = jnp.sum(q * k, axis=-1, keepdims=True)
        kt = _columns(k)
        qt = _columns(q)
        for i in range(bg):
            s = s_ref[i, h]
            kcol = kt[:, i:i + 1]
            e_i = eg[i:i + 1, :]
            ks = jnp.sum(kcol * s, axis=0, keepdims=True)
            qs = jnp.sum(qt[:, i:i + 1] * s, axis=0, keepdims=True)
            v_new = beta[i:i + 1, :] * (v[i:i + 1, :] - e_i * ks)
            oa_ref[i:i + 1, hs] = e_i * qs + qk[i:i + 1, :] * v_new
            so_ref[i, h] = e_i * s + kcol * v_new
        o = oa_ref[:, hs]
        o = o * lax.rsqrt(jnp.mean(o * o, axis=-1, keepdims=True) + LN_EPS) * nw
        oa_ref[:, hs] = o * _silu(z)

    li_all = gates + gbi_ref[...]
    lf_all = _log_sigmoid(gates + gbf_ref[...])
    for h in range(H_MIX):
        hs = slice(h * HEAD_DIM, (h + 1) * HEAD_DIM)
        q = p_ref[:, COL_B_Q + h * HEAD_DIM:COL_B_Q + (h + 1) * HEAD_DIM]
        k = p_ref[:, COL_B_K + h * HEAD_DIM:COL_B_K + (h + 1) * HEAD_DIM] * (HEAD_DIM ** -0.5)
        v = p_ref[:, COL_B_V + h * HEAD_DIM:COL_B_V + (h + 1) * HEAD_DIM]
        og = p_ref[:, COL_B_O + h * HEAD_DIM:COL_B_O + (h + 1) * HEAD_DIM]
        li = li_all[:, GATE_B_I + h:GATE_B_I + h + 1]
        lf = lf_all[:, GATE_B_F + h:GATE_B_F + h + 1]
        m_prev = m_ref[:, h:h + 1]
        n_prev = n_ref[:, hs]
        m_new = jnp.maximum(lf + m_prev, li)
        w_in = jnp.exp(li - m_new)
        dec = jnp.exp(lf + m_prev - m_new)
        sc = jnp.sum(q * k, axis=-1, keepdims=True) * w_in
        den = sc + dec * jnp.sum(q * n_prev, axis=-1, keepdims=True)
        denom = jnp.maximum(jnp.abs(den), jnp.exp(-m_new))
        no_ref[:, hs] = dec * n_prev + w_in * k
        mo_ref[:, h:h + 1] = m_new
        kt = _columns(k)
        qt = _columns(q)
        wv = w_in * v
        for i in range(bg):
            c = c_ref[i, h]
            d_i = dec[i:i + 1, :]
            qc = jnp.sum(qt[:, i:i + 1] * c, axis=0, keepdims=True)
            ob_ref[i:i + 1, hs] = sc[i:i + 1, :] * v[i:i + 1, :] + d_i * qc
            co_ref[i, h] = d_i * c + kt[:, i:i + 1] * wv[i:i + 1, :]
        ob_ref[:, hs] = _sigmoid(og) * (ob_ref[:, hs] / denom)

    gu = _gelu_tanh(p_ref[:, COL_D_U:COL_D_U + D_MIX])
    gv = _gmlp_gv(p_ref[:, COL_D_V:COL_D_V + D_MIX], gmnw_ref[...])
    gv_ref[...] = gv
    od_ref[...] = gu * (gmw0_ref[...] * gv + gmb0_ref[...])

    cos = cos_ref[...]
    sin = sin_ref[...]
    qr_ref[...] = _rope(p_ref[:, COL_C_Q:COL_C_Q + D_MIX], cos, sin)
    kr_ref[...] = _rope(p_ref[:, COL_C_K:COL_C_K + D_MIX], cos, sin)


def _sample_state(proj, gates, conv_t, s0, c0, n0, m0, conv_w, alog_row, dtb_row, nw_row, gbi_row, gbf_row,
                  gm_nw_row, gm_w0_row, gm_b0_row, cos_row, sin_row, *, layer):
    bg = SAMPLE_GROUP
    nrows = proj.shape[0]
    full = lambda shape: pl.BlockSpec(shape, lambda i: (0,) * len(shape))
    lp = lambda shape: _layer_param(shape, layer)
    row_blk = lambda w: pl.BlockSpec((bg, w), lambda i: (i, 0))
    mat_blk = pl.BlockSpec((bg, H_MIX, HEAD_DIM, HEAD_DIM), lambda i: (i, 0, 0, 0))
    mat_in = pl.BlockSpec((None, bg, H_MIX, HEAD_DIM, HEAD_DIM), lambda i: (layer, i, 0, 0, 0))
    conv_blk = pl.BlockSpec((CONV_W - 1, bg, 3 * D_MIX), lambda i: (0, i, 0))
    conv_in = pl.BlockSpec((None, CONV_W - 1, bg, 3 * D_MIX), lambda i: (layer, 0, i, 0))
    row_in = lambda w: pl.BlockSpec((None, bg, w), lambda i: (layer, i, 0))
    f = lambda shape: jax.ShapeDtypeStruct(shape, F32)
    return pl.pallas_call(
        _sample_state_kernel,
        grid=(nrows // bg,),
        in_specs=[row_blk(N_MAIN), row_blk(LANES),
                  conv_in, mat_in, mat_in, row_in(D_MIX), row_in(H_MIX),
                  lp((CONV_W, 3 * D_MIX)), lp((1, LANES)), lp((1, LANES)), lp((1, HEAD_DIM)),
                  lp((1, LANES)), lp((1, LANES)),
                  lp((1, D_MIX)), lp((1, D_MIX)), lp((1, D_MIX)),
                  full((1, HEAD_DIM)), full((1, HEAD_DIM))],
        out_specs=[row_blk(D_MIX), row_blk(D_MIX), row_blk(D_MIX), conv_blk, mat_blk, mat_blk,
                   row_blk(D_MIX), row_blk(H_MIX), row_blk(D_MIX), row_blk(D_MIX), row_blk(D_MIX)],
        out_shape=[f((nrows, D_MIX)), f((nrows, D_MIX)), f((nrows, D_MIX)),
                   f((CONV_W - 1, nrows, 3 * D_MIX)),
                   f((nrows, H_MIX, HEAD_DIM, HEAD_DIM)), f((nrows, H_MIX, HEAD_DIM, HEAD_DIM)),
                   f((nrows, D_MIX)), f((nrows, H_MIX)),
                   f((nrows, D_MIX)), f((nrows, D_MIX)), f((nrows, D_MIX))],
        compiler_params=_cparams("parallel"),
        name="sample_state",
    )(proj, gates, conv_t, s0, c0, n0, m0, conv_w, alog_row, dtb_row, nw_row, gbi_row, gbf_row,
      gm_nw_row, gm_w0_row, gm_b0_row, cos_row, sin_row)


def _head_rows(row):
    sub = lax.broadcasted_iota(jnp.int32, (SUBLANES, HEAD_DIM), 0)
    out = jnp.zeros((SUBLANES, HEAD_DIM), F32)
    for h in range(H_MIX):
        out = jnp.where(sub == h, jnp.broadcast_to(row[:, h * HEAD_DIM:(h + 1) * HEAD_DIM], (SUBLANES, HEAD_DIM)), out)
    return out


def _moba_decode_kernel(n_pages, page_rows, pt_ref, q_ref, kn_ref, vn_ref, *refs):
    del pt_ref
    k_refs = refs[:n_pages]
    v_refs = refs[n_pages:2 * n_pages]
    o_ref = refs[2 * n_pages]
    page_size = page_rows // H_MIX
    pages_per_block = MOBA_BLOCK // page_size
    n_blocks = n_pages // pages_per_block
    scale = HEAD_DIM ** -0.5

    q8 = _head_rows(q_ref[0])
    q8_bf = q8.astype(BF16)
    sub = lax.broadcasted_iota(jnp.int32, (SUBLANES, page_rows), 0)
    col = lax.broadcasted_iota(jnp.int32, (SUBLANES, page_rows), 1)
    head_mask = (col % H_MIX) == sub
    sub_d = lax.broadcasted_iota(jnp.int32, (SUBLANES, HEAD_DIM), 0)

    gates = []
    for j in range(n_blocks):
        fold = jnp.zeros((SUBLANES, HEAD_DIM), F32)
        for p in range(j * pages_per_block, (j + 1) * pages_per_block):
            fold = fold + jnp.sum(k_refs[p][0, 0].reshape(page_rows // SUBLANES, SUBLANES, HEAD_DIM), axis=0)
        ksum = fold
        for g in range(1, SUBLANES // H_MIX):
            ksum = ksum + pltpu.roll(fold, SUBLANES - g * H_MIX, 0)
        ksum = jnp.where(sub_d < H_MIX, ksum, 0.0)
        gates.append(jnp.sum(q8 * ksum, axis=1, keepdims=True) * (1.0 / MOBA_BLOCK))
    sels = []
    for n in range(n_blocks):
        rank = jnp.zeros((SUBLANES, 1), F32)
        for m in range(n_blocks):
            if m == n:
                continue
            ahead = gates[m] > gates[n]
            if m < n:
                ahead = ahead | (gates[m] == gates[n])
            rank = rank + jnp.where(ahead, 1.0, 0.0)
        sels.append(rank < float(MOBA_TOPK))

    keeps = [head_mask & sels[p // pages_per_block] for p in range(n_pages)]
    scores = [jnp.where(keeps[p],
                        lax.dot_general(q8_bf, k_refs[p][0, 0].astype(BF16), (((1,), (1,)), ((), ())),
                                        preferred_element_type=F32) * scale, NEG_BIG)
              for p in range(n_pages)]
    s_own = jnp.sum(q8 * _head_rows(kn_ref[0]), axis=1, keepdims=True) * scale
    m_all = s_own
    for s in scores:
        m_all = jnp.maximum(m_all, jnp.max(s, axis=1, keepdims=True))
    probs = [jnp.where(keeps[p], jnp.exp(scores[p] - m_all), 0.0) for p in range(n_pages)]
    e_own = jnp.exp(s_own - m_all)
    l_all = e_own
    acc = e_own * _head_rows(vn_ref[0])
    for p in range(n_pages):
        l_all = l_all + jnp.sum(probs[p], axis=1, keepdims=True)
        acc = acc + _dot(probs[p], v_refs[p][0, 0])
    out = acc / l_all
    for h in range(H_MIX):
        o_ref[0, :, h * HEAD_DIM:(h + 1) * HEAD_DIM] = out[h:h + 1, :]


def _moba_decode(page_table, q_rope, k_new, v_new, cache_k, cache_v, *, layer):
    bsz, n_pages = page_table.shape
    page_rows = cache_k.shape[2]
    page_size = page_rows // H_MIX
    assert MOBA_BLOCK % page_size == 0 and (n_pages * page_size) % MOBA_BLOCK == 0
    assert (n_pages * page_size) // MOBA_BLOCK >= MOBA_TOPK and SUBLANES % H_MIX == 0
    row = pl.BlockSpec((1, 1, D_MIX), lambda b, pt: (b, 0, 0))
    page_specs = [pl.BlockSpec((1, 1, page_rows, HEAD_DIM), lambda b, pt, p=p: (layer, pt[b, p], 0, 0))
                  for p in range(n_pages)]
    return pl.pallas_call(
        functools.partial(_moba_decode_kernel, n_pages, page_rows),
        grid_spec=pltpu.PrefetchScalarGridSpec(
            num_scalar_prefetch=1,
            grid=(bsz,),
            in_specs=[row, row, row] + page_specs + page_specs,
            out_specs=row),
        out_shape=jax.ShapeDtypeStruct((bsz, 1, D_MIX), F32),
        compiler_params=_cparams("parallel"),
        name="moba_decode",
    )(page_table, q_rope, k_new, v_new, *([cache_k] * n_pages), *([cache_v] * n_pages))


def _largest_divisor(n, candidates):
    for c in candidates:
        if n % c == 0:
            return c
    raise ValueError(f"no tile in {candidates} divides {n}")


def _tiles(rows, seqlen):
    return dict(
        proj_tm=_largest_divisor(rows, (1024, 512, 256, 128)),
        proj_tn=512,
        out_tm=_largest_divisor(rows, (512, 256, 128)),
        ffn_tm=_largest_divisor(rows, (512, 256, 128)),
        ffn_tf=1024,
        scan_tb=_largest_divisor(seqlen, (256, 128, 64, 32, 16, 8)),
        mlstm_tb=_largest_divisor(seqlen, (512, 256, 128, 64, 32, 16, 8)),
        gmlp_tb=_largest_divisor(seqlen, (512, 256, 128)),
    )


def _lane_rows(vals, offset):
    out = jnp.zeros((vals.shape[0], 1, LANES), F32)
    return out.at[:, 0, offset:offset + vals.shape[1]].set(vals.astype(F32))


def kernel(x_prompt, x_sample, state_gdn_conv, state_gdn_s, state_mlstm_c, state_mlstm_n, state_mlstm_m,
           cache_k, cache_v, page_table, w_in, gdn_conv_w, gdn_a_log, gdn_dt_bias, gdn_norm_w, mlstm_gate_b,
           gmlp_norm_w, gmlp_ws, gmlp_b, w_out, ln1_w, ln1_b, w_up, w_down, ln2_w, ln2_b):
    bsz, seq, d_model = x_prompt.shape
    dec_b, dec_s, _ = x_sample.shape
    depth = w_in.shape[0]
    assert dec_s == 1 and d_model == N_MIXERS * D_MIX and seq >= CONV_W - 1
    assert seq % MOBA_BLOCK == 0 and seq // MOBA_BLOCK <= KMEAN_ROWS and seq % GM_CHUNK == 0
    assert w_in.shape[2] == N_MAIN + 4 * H_MIX and cache_k.shape[3] == H_MIX
    alpha = (2.0 * depth) ** 0.25
    mp = bsz * seq
    past_len = page_table.shape[1] * cache_k.shape[2]
    n_pool, page_size = cache_k.shape[1], cache_k.shape[2]
    tp = _tiles(mp, seq)
    ts = _tiles(dec_b, seq)
    gw = D_MIX // GM_GROUPS

    cos_p, sin_p = _rope_tables(jnp.arange(seq, dtype=jnp.int32))
    cos_s, sin_s = _rope_tables(past_len + jnp.arange(dec_s, dtype=jnp.int32))
    ck = cache_k.reshape(depth, n_pool, page_size * H_MIX, HEAD_DIM)
    cv = cache_v.reshape(depth, n_pool, page_size * H_MIX, HEAD_DIM)

    w_main, w_gate = _pack_w_in(jnp.transpose(w_in.astype(F32), (0, 2, 1)), tc=_largest_divisor(d_model, (256, 128)))
    w_out_bf = w_out.astype(BF16)
    w_up_bf = w_up.astype(BF16)
    w_down_bf = w_down.astype(BF16)
    gdn_s_in = state_gdn_s.astype(F32)
    ml_c_in = state_mlstm_c.astype(F32)

    alog_row = _lane_rows(gdn_a_log, GATE_A_DEC)
    dtb_row = _lane_rows(gdn_dt_bias, GATE_A_DEC)
    nw_row = gdn_norm_w.reshape(depth, 1, HEAD_DIM).astype(F32)
    gbi_row = _lane_rows(mlstm_gate_b[:, :H_MIX], GATE_B_I)
    gbf_row = _lane_rows(mlstm_gate_b[:, H_MIX:], GATE_B_F)
    gm_nw_row = gmlp_norm_w.reshape(depth, 1, D_MIX).astype(F32)
    gm_ws = gmlp_ws.astype(F32)
    gm_bt = jnp.transpose(gmlp_b.astype(F32), (0, 2, 1))
    gm_w0_row = jnp.repeat(gmlp_ws[:, :, 0, 0].astype(F32), gw, axis=1).reshape(depth, 1, D_MIX)
    gm_b0_row = jnp.repeat(gmlp_b[:, :, 0].astype(F32), gw, axis=1).reshape(depth, 1, D_MIX)
    conv_w = gdn_conv_w.astype(F32)
    ln1 = (ln1_w.reshape(depth, 1, d_model).astype(F32), ln1_b.reshape(depth, 1, d_model).astype(F32))
    ln2 = (ln2_w.reshape(depth, 1, d_model).astype(F32), ln2_b.reshape(depth, 1, d_model).astype(F32))
    conv_in = jnp.transpose(state_gdn_conv.astype(F32), (0, 2, 1, 3))
    ml_n_in = state_mlstm_n.astype(F32).reshape(depth, dec_b, D_MIX)
    ml_m_in = state_mlstm_m.astype(F32)

    xp = x_prompt.reshape(mp, d_model)
    xs = x_sample.reshape(dec_b, d_model)
    xp_in, xs_in = xp, xs
    p_st, s_st = [], []
    for l in range(depth):
        proj, gates = _proj(xp_in, w_main, w_gate, layer=l, tm=tp["proj_tm"], tn=tp["proj_tn"])
        oa, gdn_s_p = _gdn_prompt(proj, gates, conv_w, alog_row, dtb_row, nw_row,
                                  layer=l, bsz=bsz, seqlen=seq, tb=tp["scan_tb"])
        ob, ml_c_p, ml_n_p, ml_m_p = _mlstm_prompt(proj, gates, gbi_row, gbf_row,
                                                   layer=l, bsz=bsz, seqlen=seq, tb=tp["mlstm_tb"])
        q_rope, k_rows, v_rows, k_bf, v_bf, kmean = _rope_prompt(proj, cos_p, sin_p, bsz=bsz, seqlen=seq)
        oc = _moba_prompt(q_rope, k_bf, v_bf, kmean, bsz=bsz, seqlen=seq)
        od = _gmlp_prompt(proj, gm_nw_row, gm_ws, gm_bt, layer=l, rows=mp, tb=tp["gmlp_tb"])
        x1, x1_bf = _outproj_ln((oa, ob, oc, od), w_out_bf, xp, *ln1, layer=l, alpha=alpha, tm=tp["out_tm"])
        xp, xp_in = _ffn_ln(x1_bf, w_up_bf, w_down_bf, x1, *ln2, layer=l, alpha=alpha,
                            tm=tp["ffn_tm"], tf=tp["ffn_tf"])
        proj_p = proj.reshape(bsz, seq, N_MAIN)
        p_st.append((proj_p[:, seq - (CONV_W - 1):, COL_A_Q:COL_A_Q + 3 * D_MIX],
                     gdn_s_p, ml_c_p, ml_n_p[:, :H_MIX], ml_m_p[:, :H_MIX, 0],
                     k_rows.reshape(bsz, seq, H_MIX, HEAD_DIM), v_rows.reshape(bsz, seq, H_MIX, HEAD_DIM)))

        proj, gates = _proj(xs_in, w_main, w_gate, layer=l, tm=ts["proj_tm"], tn=ts["proj_tn"])
        (oa, ob, od, conv_s, gdn_s_s, ml_c_s, ml_n_s, ml_m_s, q_s, k_s, gv_s) = _sample_state(
            proj, gates, conv_in, gdn_s_in, ml_c_in, ml_n_in, ml_m_in,
            conv_w, alog_row, dtb_row, nw_row, gbi_row, gbf_row, gm_nw_row, gm_w0_row, gm_b0_row,
            cos_s, sin_s, layer=l)
        v_s = proj[:, COL_C_V:COL_C_V + D_MIX]
        oc = _moba_decode(page_table, q_s.reshape(dec_b, 1, D_MIX), k_s.reshape(dec_b, 1, D_MIX),
                          v_s.reshape(dec_b, 1, D_MIX), ck, cv, layer=l).reshape(dec_b, D_MIX)
        x1, x1_bf = _outproj_ln((oa, ob, oc, od), w_out_bf, xs, *ln1, layer=l, alpha=alpha, tm=ts["out_tm"])
        xs, xs_in = _ffn_ln(x1_bf, w_up_bf, w_down_bf, x1, *ln2, layer=l, alpha=alpha,
                            tm=ts["ffn_tm"], tf=ts["ffn_tf"])
        s_st.append((jnp.transpose(conv_s, (1, 0, 2)), gdn_s_s, ml_c_s,
                     ml_n_s.reshape(dec_b, H_MIX, HEAD_DIM), ml_m_s,
                     k_s.reshape(dec_b, dec_s, H_MIX, HEAD_DIM), v_s.reshape(dec_b, dec_s, H_MIX, HEAD_DIM),
                     gv_s.reshape(dec_b, dec_s, D_MIX)))

    def stk(sts, i):
        return jnp.stack([s[i] for s in sts], axis=0)

    dt = x_prompt.dtype
    yp = xp.reshape(bsz, seq, d_model).astype(dt)
    ys = xs.reshape(dec_b, dec_s, d_model).astype(dt)
    return (yp, ys) + tuple(stk(p_st, i).astype(dt) for i in range(7)) + tuple(stk(s_st, i).astype(dt) for i in range(8))
```

```python
import functools
import math

import jax
import jax.numpy as jnp
from jax import lax
from jax.experimental import pallas as pl
from jax.experimental.pallas import tpu as pltpu

F32 = jnp.float32
BF16 = jnp.bfloat16
HIGHEST = lax.Precision.HIGHEST

HEAD_DIM = 128
N_MIXERS = 4
CONV_W = 4
GDN_CHUNK = 64
MLSTM_CHUNK = 64
MOBA_BLOCK = 256
MOBA_TOPK = 3
GM_CHUNK = 128
GM_GROUPS = 4
ROPE_THETA = 500000.0
ROT_DIM = HEAD_DIM // 4
LN_EPS = 1e-5
NEG_BIG = -1e30

LANES = 128
SUBLANES = 8
VMEM_LIMIT_BYTES = 56 * 1024 * 1024


def _cparams(*sem):
    return pltpu.CompilerParams(dimension_semantics=sem, vmem_limit_bytes=VMEM_LIMIT_BYTES)


def _dot(a, b):
    return jnp.dot(a.astype(BF16), b.astype(BF16), preferred_element_type=F32)


def _dot_nt(a, b):
    return lax.dot_general(a.astype(BF16), b.astype(BF16), (((1,), (1,)), ((), ())),
                           preferred_element_type=F32)


def _dot_tn(a, b):
    return lax.dot_general(a.astype(BF16), b.astype(BF16), (((0,), (0,)), ((), ())),
                           preferred_element_type=F32)


def _dot_hi(a, b):
    return jnp.dot(a, b, precision=HIGHEST, preferred_element_type=F32)


def _dot_nt_hi(a, b):
    return lax.dot_general(a, b, (((1,), (1,)), ((), ())), precision=HIGHEST,
                           preferred_element_type=F32)


def _split2(x):
    hi = x.astype(BF16)
    return hi, (x - hi.astype(F32)).astype(BF16)


def _dot_split(a_parts, b_parts):
    ah, al = a_parts
    bh, bl = b_parts
    return (jnp.dot(ah, bh, preferred_element_type=F32) + jnp.dot(ah, bl, preferred_element_type=F32)
            + jnp.dot(al, bh, preferred_element_type=F32))


def _cumsum_rows(tril_bf, x):
    x0 = x.astype(BF16)
    r1 = x - x0.astype(F32)
    x1 = r1.astype(BF16)
    x2 = (r1 - x1.astype(F32)).astype(BF16)
    return (jnp.dot(tril_bf, x0, preferred_element_type=F32) + jnp.dot(tril_bf, x1, preferred_element_type=F32)
            + jnp.dot(tril_bf, x2, preferred_element_type=F32))


def _sigmoid(x):
    return 1.0 / (1.0 + jnp.exp(-x))


def _silu(x):
    return x * _sigmoid(x)


def _softplus(x):
    return jnp.maximum(x, 0.0) + jnp.log(1.0 + jnp.exp(-jnp.abs(x)))


def _log_sigmoid(x):
    return -_softplus(-x)


def _gelu_tanh(x):
    return 0.5 * x * (1.0 + jnp.tanh(math.sqrt(2.0 / math.pi) * (x + 0.044715 * (x * x * x))))


def _layer_norm(x, w, b):
    mu = jnp.mean(x, axis=-1, keepdims=True)
    xc = x - mu
    var = jnp.mean(xc * xc, axis=-1, keepdims=True)
    return xc * lax.rsqrt(var + LN_EPS) * w + b


D_MIX = 512
H_MIX = D_MIX // HEAD_DIM
COL_A_Q, COL_A_K, COL_A_V, COL_A_Z = 0, 512, 1024, 1536
COL_B_Q, COL_B_K, COL_B_V, COL_B_O = 2048, 2560, 3072, 3584
COL_C_Q, COL_C_K, COL_C_V = 4096, 4608, 5120
COL_D_U, COL_D_V = 5632, 6144
N_MAIN = 6656
GATE_A_BETA, GATE_A_DEC, GATE_B_I, GATE_B_F = 0, 4, 8, 12


def _layer_param(shape, layer):
    return pl.BlockSpec((None,) + tuple(shape), lambda *_: (layer,) + (0,) * len(shape))


def _tri_masks(n):
    r = lax.broadcasted_iota(jnp.int32, (n, n), 0)
    c = lax.broadcasted_iota(jnp.int32, (n, n), 1)
    return r >= c, r > c


def _pack_w_in_kernel(w_ref, main_ref, gate_ref):
    a_end = 4 * D_MIX
    b0 = a_end + 2 * H_MIX
    b_end = b0 + 4 * D_MIX
    c0 = b_end + 2 * H_MIX
    main_ref[0, 0:a_end, :] = w_ref[0, 0:a_end, :].astype(BF16)
    main_ref[0, a_end:2 * a_end, :] = w_ref[0, b0:b_end, :].astype(BF16)
    main_ref[0, 2 * a_end:N_MAIN, :] = w_ref[0, c0:c0 + N_MAIN - 2 * a_end, :].astype(BF16)
    cols = w_ref.shape[2]
    gate_ref[0] = jnp.concatenate([w_ref[0, a_end:b0, :], w_ref[0, b_end:c0, :],
                                   jnp.zeros((LANES - 4 * H_MIX, cols), F32)], axis=0).astype(BF16)


def _pack_w_in(w_in_t, *, tc):
    depth, n_in, d = w_in_t.shape
    return pl.pallas_call(
        _pack_w_in_kernel,
        grid=(depth, d // tc),
        in_specs=[pl.BlockSpec((1, n_in, tc), lambda l, i: (l, 0, i))],
        out_specs=[pl.BlockSpec((1, N_MAIN, tc), lambda l, i: (l, 0, i)),
                   pl.BlockSpec((1, LANES, tc), lambda l, i: (l, 0, i))],
        out_shape=[jax.ShapeDtypeStruct((depth, N_MAIN, d), BF16),
                   jax.ShapeDtypeStruct((depth, LANES, d), BF16)],
        compiler_params=_cparams("parallel", "parallel"),
        name="pack_w_in",
    )(w_in_t)


def _proj_kernel(x_ref, w_ref, wg_ref, o_ref, g_ref, xb_ref):
    nt = (((1,), (1,)), ((), ()))

    @pl.when(pl.program_id(1) == 0)
    def _():
        xb_ref[...] = x_ref[...].astype(BF16)
        g_ref[...] = lax.dot_general(xb_ref[...], wg_ref[0], nt, preferred_element_type=F32)

    o_ref[...] = lax.dot_general(xb_ref[...], w_ref[0], nt, preferred_element_type=F32)


def _proj(x, w_main_t, w_gate_t, *, layer, tm, tn):
    m, d = x.shape
    n = w_main_t.shape[1]
    return pl.pallas_call(
        _proj_kernel,
        grid=(m // tm, n // tn),
        in_specs=[pl.BlockSpec((tm, d), lambda i, j: (i, 0)),
                  pl.BlockSpec((1, tn, d), lambda i, j: (layer, j, 0)),
                  pl.BlockSpec((1, LANES, d), lambda i, j: (layer, 0, 0))],
        out_specs=[pl.BlockSpec((tm, tn), lambda i, j: (i, j)),
                   pl.BlockSpec((tm, LANES), lambda i, j: (i, 0))],
        out_shape=[jax.ShapeDtypeStruct((m, n), F32), jax.ShapeDtypeStruct((m, LANES), F32)],
        scratch_shapes=[pltpu.VMEM((tm, d), BF16)],
        compiler_params=_cparams("parallel", "arbitrary"),
        name="proj",
    )(x, w_main_t, w_gate_t)


def _outproj_ln_kernel(alpha, a_ref, b_ref, c_ref, d_ref, w_ref, x_ref, lw_ref, lb_ref, o_ref, ob_ref):
    y = alpha * x_ref[...]
    for i, m_ref in enumerate((a_ref, b_ref, c_ref, d_ref)):
        y = y + jnp.dot(m_ref[...].astype(BF16), w_ref[0, i * D_MIX:(i + 1) * D_MIX, :],
                        preferred_element_type=F32)
    y = _layer_norm(y, lw_ref[...], lb_ref[...])
    o_ref[...] = y
    ob_ref[...] = y.astype(BF16)


def _outproj_ln(mixes, w_out_bf, x, ln_w, ln_b, *, layer, alpha, tm):
    m, d = x.shape
    mix_spec = pl.BlockSpec((tm, D_MIX), lambda i: (i, 0))
    return pl.pallas_call(
        functools.partial(_outproj_ln_kernel, alpha),
        grid=(m // tm,),
        in_specs=[mix_spec, mix_spec, mix_spec, mix_spec,
                  pl.BlockSpec((1, d, d), lambda i: (layer, 0, 0)),
                  pl.BlockSpec((tm, d), lambda i: (i, 0)),
                  _layer_param((1, d), layer), _layer_param((1, d), layer)],
        out_specs=[pl.BlockSpec((tm, d), lambda i: (i, 0)),
                   pl.BlockSpec((tm, d), lambda i: (i, 0))],
        out_shape=[jax.ShapeDtypeStruct((m, d), F32), jax.ShapeDtypeStruct((m, d), BF16)],
        compiler_params=_cparams("parallel"),
        name="outproj_ln",
    )(*mixes, w_out_bf, x, ln_w, ln_b)


def _ffn_ln_kernel(alpha, xb_ref, wu_ref, wd_ref, x_ref, lw_ref, lb_ref, o_ref, ob_ref, acc_ref):
    j = pl.program_id(1)

    @pl.when(j == 0)
    def _():
        acc_ref[...] = alpha * x_ref[...]

    h = jnp.dot(xb_ref[...], wu_ref[0], preferred_element_type=F32)
    h = jnp.maximum(h, 0.0)
    h = (h * h).astype(BF16)
    acc_ref[...] += jnp.dot(h, wd_ref[0], preferred_element_type=F32)

    @pl.when(j == pl.num_programs(1) - 1)
    def _():
        y = _layer_norm(acc_ref[...], lw_ref[...], lb_ref[...])
        o_ref[...] = y
        ob_ref[...] = y.astype(BF16)


def _ffn_ln(x_bf, w_up_bf, w_down_bf, x, ln_w, ln_b, *, layer, alpha, tm, tf):
    m, d = x.shape
    f = w_up_bf.shape[2]
    return pl.pallas_call(
        functools.partial(_ffn_ln_kernel, alpha),
        grid=(m // tm, f // tf),
        in_specs=[pl.BlockSpec((tm, d), lambda i, j: (i, 0)),
                  pl.BlockSpec((1, d, tf), lambda i, j: (layer, 0, j)),
                  pl.BlockSpec((1, tf, d), lambda i, j: (layer, j, 0)),
                  pl.BlockSpec((tm, d), lambda i, j: (i, 0)),
                  _layer_param((1, d), layer), _layer_param((1, d), layer)],
        out_specs=[pl.BlockSpec((tm, d), lambda i, j: (i, 0)),
                   pl.BlockSpec((tm, d), lambda i, j: (i, 0))],
        out_shape=[jax.ShapeDtypeStruct((m, d), F32), jax.ShapeDtypeStruct((m, d), BF16)],
        scratch_shapes=[pltpu.VMEM((tm, d), F32)],
        compiler_params=_cparams("parallel", "arbitrary"),
        name="ffn_ln",
    )(x_bf, w_up_bf, w_down_bf, x, ln_w, ln_b)


def _unit_lower_inverses(mats):
    n = mats[0].shape[0]
    r = lax.broadcasted_iota(jnp.int32, (n, n), 0)
    c = lax.broadcasted_iota(jnp.int32, (n, n), 1)
    eye = jnp.where(r == c, 1.0, 0.0).astype(F32)
    ts = [eye - a for a in mats]
    ps = [_split2(a) for a in mats]
    span = 2
    while span < n:
        ps = [_split2(_dot_split(p, p)) for p in ps]
        ts = [t + _dot_split(_split2(t), p) for t, p in zip(ts, ps)]
        span *= 2
    return ts


def _gdn_prompt_kernel(tb, cl, x_ref, g_ref, cw_ref, alog_ref, dtb_ref, nw_ref,
                       o_ref, s_out_ref, xbuf, s_ref):
    t = pl.program_id(1)
    dqkv = 3 * D_MIX

    @pl.when(t == 0)
    def _():
        xbuf[0:SUBLANES, :] = jnp.zeros((SUBLANES, dqkv), F32)
        s_ref[...] = jnp.zeros_like(s_ref)

    @pl.when(t > 0)
    def _():
        xbuf[0:SUBLANES, :] = xbuf[tb:tb + SUBLANES, :]

    xbuf[SUBLANES:SUBLANES + tb, :] = x_ref[:, 0:dqkv]
    cw = cw_ref[...]
    y = cw[0:1] * xbuf[SUBLANES - 3:SUBLANES - 3 + tb, :]
    for j in range(1, CONV_W):
        y = y + cw[j:j + 1] * xbuf[SUBLANES - 3 + j:SUBLANES - 3 + j + tb, :]
    y = _silu(y)

    gates = g_ref[...]
    beta_all = _sigmoid(gates)
    g_all = -jnp.exp(alog_ref[...]) * _softplus(gates + dtb_ref[...])
    tri, tri_s = _tri_masks(cl)
    tril_bf = jnp.where(tri, 1.0, 0.0).astype(BF16)
    nw = nw_ref[...]
    states = [s_ref[h] for h in range(H_MIX)]

    a_mats, pre = [], []
    for c in range(tb // cl):
        r0 = c * cl
        gcum = _cumsum_rows(tril_bf, g_all[r0:r0 + cl, :])
        gcum_t = gcum.T
        for h in range(H_MIX):
            q = y[r0:r0 + cl, COL_A_Q + h * HEAD_DIM:COL_A_Q + (h + 1) * HEAD_DIM]
            k = y[r0:r0 + cl, COL_A_K + h * HEAD_DIM:COL_A_K + (h + 1) * HEAD_DIM]
            v = y[r0:r0 + cl, COL_A_V + h * HEAD_DIM:COL_A_V + (h + 1) * HEAD_DIM]
            q = q * lax.rsqrt(jnp.sum(q * q, axis=-1, keepdims=True) + 1e-6) * (HEAD_DIM ** -0.5)
            k = k * lax.rsqrt(jnp.sum(k * k, axis=-1, keepdims=True) + 1e-6)
            beta = beta_all[r0:r0 + cl, GATE_A_BETA + h:GATE_A_BETA + h + 1]
            gc = gcum[:, GATE_A_DEC + h:GATE_A_DEC + h + 1]
            gr = gcum_t[GATE_A_DEC + h:GATE_A_DEC + h + 1, :]
            decay = jnp.exp(jnp.where(tri, gc - gr, NEG_BIG))
            kb = k * beta
            egc = jnp.exp(gc)
            gc_last = gc[cl - 1:cl, :]
            kk = _dot_nt(jnp.concatenate([kb, q], axis=0), k)
            a_mats.append(jnp.where(tri_s, kk[0:cl] * decay, 0.0))
            pre.append(dict(attn=jnp.where(tri, kk[cl:2 * cl] * decay, 0.0),
                            rhs=jnp.concatenate([v * beta, kb * egc], axis=1),
                            q_dec=q * egc, k_tail_t=(k * jnp.exp(gc_last - gc)).T, g_tot=jnp.exp(gc_last)))
    t_invs = _unit_lower_inverses(a_mats)
    uws = [_dot(t_inv, p["rhs"]) for t_inv, p in zip(t_invs, pre)]

    for c in range(tb // cl):
        r0 = c * cl
        for h in range(H_MIX):
            p, uw = pre[c * H_MIX + h], uws[c * H_MIX + h]
            z = x_ref[r0:r0 + cl, COL_A_Z + h * HEAD_DIM:COL_A_Z + (h + 1) * HEAD_DIM]
            s = states[h]
            ws_qs = _dot(jnp.concatenate([uw[:, HEAD_DIM:], p["q_dec"]], axis=0), s)
            v_new = uw[:, :HEAD_DIM] - ws_qs[0:cl]
            o = ws_qs[cl:2 * cl] + _dot(p["attn"], v_new)
            states[h] = s * p["g_tot"] + _dot(p["k_tail_t"], v_new)
            o = o * lax.rsqrt(jnp.mean(o * o, axis=-1, keepdims=True) + LN_EPS) * nw
            o_ref[r0:r0 + cl, h * HEAD_DIM:(h + 1) * HEAD_DIM] = (o * _silu(z)).astype(o_ref.dtype)

    for h in range(H_MIX):
        s_ref[h] = states[h]

    @pl.when(t == pl.num_programs(1) - 1)
    def _():
        s_out_ref[0] = s_ref[...]


def _gdn_prompt(proj, gates, conv_w, alog_row, dtb_row, nw_row, *, layer, bsz, seqlen, tb):
    cl = math.gcd(seqlen, GDN_CHUNK)
    nt = seqlen // tb
    return pl.pallas_call(
        functools.partial(_gdn_prompt_kernel, tb, cl),
        grid=(bsz, nt),
        in_specs=[pl.BlockSpec((tb, 4 * D_MIX), lambda b, t: (b * nt + t, 0)),
                  pl.BlockSpec((tb, LANES), lambda b, t: (b * nt + t, 0)),
                  _layer_param((CONV_W, 3 * D_MIX), layer),
                  _layer_param((1, LANES), layer), _layer_param((1, LANES), layer),
                  _layer_param((1, HEAD_DIM), layer)],
        out_specs=[pl.BlockSpec((tb, D_MIX), lambda b, t: (b * nt + t, 0)),
                   pl.BlockSpec((1, H_MIX, HEAD_DIM, HEAD_DIM), lambda b, t: (b, 0, 0, 0))],
        out_shape=[jax.ShapeDtypeStruct((bsz * seqlen, D_MIX), BF16),
                   jax.ShapeDtypeStruct((bsz, H_MIX, HEAD_DIM, HEAD_DIM), F32)],
        scratch_shapes=[pltpu.VMEM((tb + SUBLANES, 3 * D_MIX), F32),
                        pltpu.VMEM((H_MIX, HEAD_DIM, HEAD_DIM), F32)],
        compiler_params=_cparams("parallel", "arbitrary"),
        name="gdn_prompt",
    )(proj, gates, conv_w, alog_row, dtb_row, nw_row)


def _mlstm_prompt_kernel(tb, cl, x_ref, g_ref, gbi_ref, gbf_ref,
                         o_ref, c_out_ref, n_out_ref, m_out_ref, c_ref, n_ref, m_ref):
    t = pl.program_id(1)

    @pl.when(t == 0)
    def _():
        c_ref[...] = jnp.zeros_like(c_ref)
        n_ref[...] = jnp.zeros_like(n_ref)
        m_ref[...] = jnp.zeros_like(m_ref)

    gates = g_ref[...]
    li_all = gates + gbi_ref[...]
    lf_all = _log_sigmoid(gates + gbf_ref[...])
    tri, _ = _tri_masks(cl)
    tril_bf = jnp.where(tri, 1.0, 0.0).astype(BF16)
    cs = [c_ref[h] for h in range(H_MIX)]
    ns = [n_ref[h:h + 1, :] for h in range(H_MIX)]
    ms = [m_ref[h:h + 1, 0:1] for h in range(H_MIX)]

    for c in range(tb // cl):
        r0 = c * cl
        bcum = _cumsum_rows(tril_bf, lf_all[r0:r0 + cl, :])
        bcum_t = bcum.T
        li_t = li_all[r0:r0 + cl, :].T
        for h in range(H_MIX):
            q = x_ref[r0:r0 + cl, h * HEAD_DIM:(h + 1) * HEAD_DIM]
            k = x_ref[r0:r0 + cl, D_MIX + h * HEAD_DIM:D_MIX + (h + 1) * HEAD_DIM] * (HEAD_DIM ** -0.5)
            v = x_ref[r0:r0 + cl, 2 * D_MIX + h * HEAD_DIM:2 * D_MIX + (h + 1) * HEAD_DIM]
            og = x_ref[r0:r0 + cl, 3 * D_MIX + h * HEAD_DIM:3 * D_MIX + (h + 1) * HEAD_DIM]
            b = bcum[:, GATE_B_F + h:GATE_B_F + h + 1]
            br = bcum_t[GATE_B_F + h:GATE_B_F + h + 1, :]
            ic = li_all[r0:r0 + cl, GATE_B_I + h:GATE_B_I + h + 1]
            ir = li_t[GATE_B_I + h:GATE_B_I + h + 1, :]
            m_prev = ms[h]
            dlog = jnp.where(tri, b - br + ir, NEG_BIG)
            inter = b + m_prev
            m = jnp.maximum(jnp.max(dlog, axis=1, keepdims=True), inter)
            s = _dot_nt(q, k) * jnp.exp(dlog - m)
            scale_prev = jnp.exp(inter - m)
            num = _dot(s, v) + scale_prev * _dot(q, cs[h])
            den = jnp.sum(s, axis=1, keepdims=True) + scale_prev * jnp.sum(q * ns[h], axis=1, keepdims=True)
            hh = num / jnp.maximum(jnp.abs(den), jnp.exp(-m))
            b_end = b[cl - 1:cl, :]
            wlog = b_end - b + ic
            m_new = jnp.maximum(b_end + m_prev, jnp.max(wlog, axis=0, keepdims=True))
            wk = jnp.exp(wlog - m_new) * k
            dec = jnp.exp(b_end + m_prev - m_new)
            cs[h] = dec * cs[h] + _dot_tn(wk, v)
            ns[h] = dec * ns[h] + jnp.sum(wk, axis=0, keepdims=True)
            ms[h] = m_new
            o_ref[r0:r0 + cl, h * HEAD_DIM:(h + 1) * HEAD_DIM] = (_sigmoid(og) * hh).astype(o_ref.dtype)

    for h in range(H_MIX):
        c_ref[h] = cs[h]
        n_ref[h:h + 1, :] = ns[h]
        m_ref[h:h + 1, :] = jnp.broadcast_to(ms[h], (1, LANES))

    @pl.when(t == pl.num_programs(1) - 1)
    def _():
        c_out_ref[0] = c_ref[...]
        n_out_ref[0] = n_ref[...]
        m_out_ref[0] = m_ref[...]


def _mlstm_prompt(proj, gates, gbi_row, gbf_row, *, layer, bsz, seqlen, tb):
    cl = math.gcd(seqlen, MLSTM_CHUNK)
    nt = seqlen // tb
    return pl.pallas_call(
        functools.partial(_mlstm_prompt_kernel, tb, cl),
        grid=(bsz, nt),
        in_specs=[pl.BlockSpec((tb, 4 * D_MIX), lambda b, t: (b * nt + t, COL_B_Q // (4 * D_MIX))),
                  pl.BlockSpec((tb, LANES), lambda b, t: (b * nt + t, 0)),
                  _layer_param((1, LANES), layer), _layer_param((1, LANES), layer)],
        out_specs=[pl.BlockSpec((tb, D_MIX), lambda b, t: (b * nt + t, 0)),
                   pl.BlockSpec((1, H_MIX, HEAD_DIM, HEAD_DIM), lambda b, t: (b, 0, 0, 0)),
                   pl.BlockSpec((1, SUBLANES, HEAD_DIM), lambda b, t: (b, 0, 0)),
                   pl.BlockSpec((1, SUBLANES, LANES), lambda b, t: (b, 0, 0))],
        out_shape=[jax.ShapeDtypeStruct((bsz * seqlen, D_MIX), BF16),
                   jax.ShapeDtypeStruct((bsz, H_MIX, HEAD_DIM, HEAD_DIM), F32),
                   jax.ShapeDtypeStruct((bsz, SUBLANES, HEAD_DIM), F32),
                   jax.ShapeDtypeStruct((bsz, SUBLANES, LANES), F32)],
        scratch_shapes=[pltpu.VMEM((H_MIX, HEAD_DIM, HEAD_DIM), F32),
                        pltpu.VMEM((SUBLANES, HEAD_DIM), F32),
                        pltpu.VMEM((SUBLANES, LANES), F32)],
        compiler_params=_cparams("parallel", "arbitrary"),
        name="mlstm_prompt",
    )(proj, gates, gbi_row, gbf_row)


def _gmlp_gv(dv, nw):
    gv = _gelu_tanh(dv)
    mu = jnp.mean(gv, axis=-1, keepdims=True)
    gc = gv - mu
    var = jnp.mean(gc * gc, axis=-1, keepdims=True)
    return gc * lax.rsqrt(var + LN_EPS) * nw


def _gmlp_prompt_kernel(tb, u_ref, v_ref, nw_ref, ws_ref, bt_ref, o_ref):
    gu = _gelu_tanh(u_ref[...])
    gv = _gmlp_gv(v_ref[...], nw_ref[...])
    tri, _ = _tri_masks(GM_CHUNK)
    gw = D_MIX // GM_GROUPS
    for g in range(GM_GROUPS):
        wm = jnp.where(tri, ws_ref[g], 0.0)
        bias = bt_ref[:, g:g + 1]
        for c in range(tb // GM_CHUNK):
            r0 = c * GM_CHUNK
            z = _dot(wm, gv[r0:r0 + GM_CHUNK, g * gw:(g + 1) * gw]) + bias
            o_ref[r0:r0 + GM_CHUNK, g * gw:(g + 1) * gw] = (
                gu[r0:r0 + GM_CHUNK, g * gw:(g + 1) * gw] * z).astype(o_ref.dtype)


def _gmlp_prompt(proj, nw_row, ws, b_t, *, layer, rows, tb):
    return pl.pallas_call(
        functools.partial(_gmlp_prompt_kernel, tb),
        grid=(rows // tb,),
        in_specs=[pl.BlockSpec((tb, D_MIX), lambda i: (i, COL_D_U // D_MIX)),
                  pl.BlockSpec((tb, D_MIX), lambda i: (i, COL_D_V // D_MIX)),
                  _layer_param((1, D_MIX), layer),
                  _layer_param((GM_GROUPS, GM_CHUNK, GM_CHUNK), layer),
                  _layer_param((GM_CHUNK, GM_GROUPS), layer)],
        out_specs=pl.BlockSpec((tb, D_MIX), lambda i: (i, 0)),
        out_shape=jax.ShapeDtypeStruct((rows, D_MIX), BF16),
        compiler_params=_cparams("parallel"),
        name="gmlp_prompt",
    )(proj, proj, nw_row, ws, b_t)


KMEAN_ROWS = 128


def _rope_tables(pos):
    half = ROT_DIM // 2
    inv_freq = ROPE_THETA ** (-jnp.arange(half, dtype=F32) * (2.0 / ROT_DIM))
    ang = pos.astype(F32)[:, None] * inv_freq[None, :]
    cos, sin = jnp.cos(ang), jnp.sin(ang)
    rest = jnp.ones((pos.shape[0], HEAD_DIM - ROT_DIM), F32)
    return (jnp.concatenate([cos, cos, rest], axis=1),
            jnp.concatenate([-sin, sin, 0.0 * rest], axis=1))


def _rope(x, cos, sin):
    lane = lax.broadcasted_iota(jnp.int32, (x.shape[0], HEAD_DIM), 1)
    first_half = lane < ROT_DIM // 2
    outs = []
    for h in range(H_MIX):
        xh = x[:, h * HEAD_DIM:(h + 1) * HEAD_DIM]
        rot = jnp.where(first_half, pltpu.roll(xh, HEAD_DIM - ROT_DIM // 2, 1), pltpu.roll(xh, ROT_DIM // 2, 1))
        outs.append(xh * cos + rot * sin)
    return jnp.concatenate(outs, axis=1)


def _rope_kernel(q_ref, k_ref, v_ref, cos_ref, sin_ref, qo_ref, ko_ref, vo_ref, kb_ref, vb_ref, km_ref):
    t = pl.program_id(1)
    cos = cos_ref[...]
    sin = sin_ref[...]
    qo_ref[...] = _rope(q_ref[...], cos, sin)
    kr = _rope(k_ref[...], cos, sin)
    v = v_ref[...]
    for h in range(H_MIX):
        ko_ref[pl.ds(h, MOBA_BLOCK, stride=H_MIX), :] = kr[:, h * HEAD_DIM:(h + 1) * HEAD_DIM]
        vo_ref[pl.ds(h, MOBA_BLOCK, stride=H_MIX), :] = v[:, h * HEAD_DIM:(h + 1) * HEAD_DIM]
    kb_ref[...] = kr.astype(BF16)
    vb_ref[...] = v.astype(BF16)

    @pl.when(t == 0)
    def _():
        km_ref[...] = jnp.zeros_like(km_ref)

    km_ref[0, pl.ds(t, 1), :] = jnp.mean(kr, axis=0, keepdims=True)


def _rope_prompt(proj, cos_tab, sin_tab, *, bsz, seqlen):
    nb = seqlen // MOBA_BLOCK
    col = lambda c: pl.BlockSpec((MOBA_BLOCK, D_MIX), lambda b, t: (b * nb + t, c // D_MIX))
    row = pl.BlockSpec((MOBA_BLOCK, D_MIX), lambda b, t: (b * nb + t, 0))
    head_rows = pl.BlockSpec((MOBA_BLOCK * H_MIX, HEAD_DIM), lambda b, t: (b * nb + t, 0))
    tab = pl.BlockSpec((MOBA_BLOCK, HEAD_DIM), lambda b, t: (t, 0))
    return pl.pallas_call(
        _rope_kernel,
        grid=(bsz, nb),
        in_specs=[col(COL_C_Q), col(COL_C_K), col(COL_C_V), tab, tab],
        out_specs=[row, head_rows, head_rows, row, row,
                   pl.BlockSpec((1, KMEAN_ROWS, D_MIX), lambda b, t: (b, 0, 0))],
        out_shape=[jax.ShapeDtypeStruct((bsz * seqlen, D_MIX), F32),
                   jax.ShapeDtypeStruct((bsz * seqlen * H_MIX, HEAD_DIM), F32),
                   jax.ShapeDtypeStruct((bsz * seqlen * H_MIX, HEAD_DIM), F32),
                   jax.ShapeDtypeStruct((bsz * seqlen, D_MIX), BF16),
                   jax.ShapeDtypeStruct((bsz * seqlen, D_MIX), BF16),
                   jax.ShapeDtypeStruct((bsz, KMEAN_ROWS, D_MIX), F32)],
        compiler_params=_cparams("parallel", "arbitrary"),
        name="rope_prompt",
    )(proj, proj, proj, cos_tab, sin_tab)


def _topk_block_rows(gate_t, n_valid):
    row = lax.broadcasted_iota(jnp.int32, gate_t.shape, 0)
    neg_inf = float("-inf")
    g = jnp.where(row < n_valid, gate_t, neg_inf)
    sel = jnp.zeros(gate_t.shape, F32)
    for _ in range(MOBA_TOPK):
        mx = jnp.max(g, axis=0, keepdims=True)
        first = jnp.min(jnp.where(g == mx, row, KMEAN_ROWS), axis=0, keepdims=True)
        pick = (row == first) & (mx > neg_inf)
        sel = jnp.where(pick, 1.0, sel)
        g = jnp.where(pick, neg_inf, g)
    return sel


def _moba_prompt_kernel(nb_pad, q_ref, k_ref, v_ref, km_ref, o_ref):
    qt = pl.program_id(1)
    blk = MOBA_BLOCK
    span = 2 * blk
    tri, _ = _tri_masks(blk)
    own0 = pl.multiple_of(qt * blk, blk)
    heads = [slice(h * HEAD_DIM, (h + 1) * HEAD_DIM) for h in range(H_MIX)]

    gates_t = [_dot_nt_hi(km_ref[0][0:nb_pad, hs], q_ref[:, hs]) for hs in heads]
    qhs = [(q_ref[:, hs] * (HEAD_DIM ** -0.5)).astype(BF16) for hs in heads]
    own_s = [jnp.where(tri, _dot_nt(qhs[h], k_ref[pl.ds(own0, blk), hs]), NEG_BIG) for h, hs in enumerate(heads)]
    own_m = [jnp.max(s, axis=1, keepdims=True) for s in own_s]
    own_p = [jnp.exp(s - m) for s, m in zip(own_s, own_m)]
    own_pv = [_dot(own_p[h], v_ref[pl.ds(own0, blk), hs]) for h, hs in enumerate(heads)]
    q_augs, carry = [], []
    for h in range(H_MIX):
        sel_t = _topk_block_rows(gates_t[h], qt)
        unpicked_t = jnp.concatenate([1.0 - sel_t, jnp.ones((KMEAN_ROWS - nb_pad, blk), F32)], axis=0)
        q_augs.append(jnp.concatenate([qhs[h], unpicked_t.T.astype(BF16)], axis=1))
        carry += [own_m[h], jnp.sum(own_p[h], axis=1, keepdims=True), own_pv[h]]

    def body(j, carry):
        r0 = pl.multiple_of(j * span, span)
        key_block = 2 * j + lax.broadcasted_iota(jnp.int32, (span, KMEAN_ROWS), 0) // blk
        block_bias = jnp.where(lax.broadcasted_iota(jnp.int32, (span, KMEAN_ROWS), 1) == key_block,
                               NEG_BIG, 0.0).astype(BF16)
        ss = [lax.dot_general(q_augs[h], jnp.concatenate([k_ref[pl.ds(r0, span), hs], block_bias], axis=1),
                              (((1,), (1,)), ((), ())), preferred_element_type=F32)
              for h, hs in enumerate(heads)]
        m_news = [jnp.maximum(carry[3 * h], jnp.max(ss[h], axis=1, keepdims=True)) for h in range(H_MIX)]
        ps = [jnp.exp(ss[h] - m_news[h]) for h in range(H_MIX)]
        pvs = [_dot(ps[h], v_ref[pl.ds(r0, span), hs]) for h, hs in enumerate(heads)]
        out = []
        for h in range(H_MIX):
            alpha = jnp.exp(carry[3 * h] - m_news[h])
            out += [m_news[h], alpha * carry[3 * h + 1] + jnp.sum(ps[h], axis=1, keepdims=True),
                    alpha * carry[3 * h + 2] + pvs[h]]
        return tuple(out)

    carry = lax.fori_loop(0, (qt + 1) // 2, body, tuple(carry))
    for h, hs in enumerate(heads):
        o_ref[:, hs] = (carry[3 * h + 2] / carry[3 * h + 1]).astype(o_ref.dtype)


def _moba_prompt(q_rope, k_bf, v_bf, kmean, *, bsz, seqlen):
    nb = seqlen // MOBA_BLOCK
    nb_pad = -(-nb // SUBLANES) * SUBLANES
    assert nb_pad <= KMEAN_ROWS
    return pl.pallas_call(
        functools.partial(_moba_prompt_kernel, nb_pad),
        grid=(bsz, nb),
        in_specs=[pl.BlockSpec((MOBA_BLOCK, D_MIX), lambda b, t: (b * nb + t, 0)),
                  pl.BlockSpec((seqlen, D_MIX), lambda b, t: (b, 0)),
                  pl.BlockSpec((seqlen, D_MIX), lambda b, t: (b, 0)),
                  pl.BlockSpec((1, KMEAN_ROWS, D_MIX), lambda b, t: (b, 0, 0))],
        out_specs=pl.BlockSpec((MOBA_BLOCK, D_MIX), lambda b, t: (b * nb + t, 0)),
        out_shape=jax.ShapeDtypeStruct((bsz * seqlen, D_MIX), BF16),
        compiler_params=_cparams("parallel", "arbitrary"),
        name="moba_prompt",
    )(q_rope, k_bf, v_bf, kmean)


SAMPLE_GROUP = SUBLANES


def _columns(rows):
    pad = jnp.zeros((HEAD_DIM - SAMPLE_GROUP, HEAD_DIM), F32)
    return jnp.concatenate([rows, pad], axis=0).T


def _sample_state_kernel(p_ref, g_ref, conv_ref, s_ref, c_ref, n_ref, m_ref,
                         cw_ref, alog_ref, dtb_ref, nw_ref, gbi_ref, gbf_ref,
                         gmnw_ref, gmw0_ref, gmb0_ref, cos_ref, sin_ref,
                         oa_ref, ob_ref, od_ref, convo_ref, so_ref, co_ref, no_ref, mo_ref,
                         qr_ref, kr_ref, gv_ref):
    bg = SAMPLE_GROUP
    gates = g_ref[...]

    cw = cw_ref[...]
    xa = p_ref[:, COL_A_Q:COL_A_Q + 3 * D_MIX]
    y = cw[CONV_W - 1:CONV_W] * xa
    for j in range(CONV_W - 1):
        y = y + cw[j:j + 1] * conv_ref[j]
    y = _silu(y)
    for j in range(CONV_W - 2):
        convo_ref[j] = conv_ref[j + 1]
    convo_ref[CONV_W - 2] = xa
    beta_all = _sigmoid(gates)
    eg_all = jnp.exp(-jnp.exp(alog_ref[...]) * _softplus(gates + dtb_ref[...]))
    nw = nw_ref[...]
    for h in range(H_MIX):
        hs = slice(h * HEAD_DIM, (h + 1) * HEAD_DIM)
        q = y[:, COL_A_Q + h * HEAD_DIM:COL_A_Q + (h + 1) * HEAD_DIM]
        k = y[:, COL_A_K + h * HEAD_DIM:COL_A_K + (h + 1) * HEAD_DIM]
        v = y[:, COL_A_V + h * HEAD_DIM:COL_A_V + (h + 1) * HEAD_DIM]
        z = p_ref[:, COL_A_Z + h * HEAD_DIM:COL_A_Z + (h + 1) * HEAD_DIM]
        q = q * lax.rsqrt(jnp.sum(q * q, axis=-1, keepdims=True) + 1e-6) * (HEAD_DIM ** -0.5)
        k = k * lax.rsqrt(jnp.sum(k * k, axis=-1, keepdims=True) + 1e-6)
        beta = beta_all[:, GATE_A_BETA + h:GATE_A_BETA + h + 1]
        eg = eg_all[:, GATE_A_DEC + h:GATE_A_DEC + h + 1]
        qk = jnp.sum(q * k, axis=-1, keepdims=True)
        kt = _columns(k)
        qt = _columns(q)
        for i in range(bg):
            s = s_ref[i, h]
            kcol = kt[:, i:i + 1]
            e_i = eg[i:i + 1, :]
            ks = jnp.sum(kcol * s, axis=0, keepdims=True)
            qs = jnp.sum(qt[:, i:i + 1] * s, axis=0, keepdims=True)
            v_new = beta[i:i + 1, :] * (v[i:i + 1, :] - e_i * ks)
            oa_ref[i:i + 1, hs] = e_i * qs + qk[i:i + 1, :] * v_new
            so_ref[i, h] = e_i * s + kcol * v_new
        o = oa_ref[:, hs]
        o = o * lax.rsqrt(jnp.mean(o * o, axis=-1, keepdims=True) + LN_EPS) * nw
        oa_ref[:, hs] = o * _silu(z)

    li_all = gates + gbi_ref[...]
    lf_all = _log_sigmoid(gates + gbf_ref[...])
    for h in range(H_MIX):
        hs = slice(h * HEAD_DIM, (h + 1) * HEAD_DIM)
        q = p_ref[:, COL_B_Q + h * HEAD_DIM:COL_B_Q + (h + 1) * HEAD_DIM]
        k = p_ref[:, COL_B_K + h * HEAD_DIM:COL_B_K + (h + 1) * HEAD_DIM] * (HEAD_DIM ** -0.5)
        v = p_ref[:, COL_B_V + h * HEAD_DIM:COL_B_V + (h + 1) * HEAD_DIM]
        og = p_ref[:, COL_B_O + h * HEAD_DIM:COL_B_O + (h + 1) * HEAD_DIM]
        li = li_all[:, GATE_B_I + h:GATE_B_I + h + 1]
        lf = lf_all[:, GATE_B_F + h:GATE_B_F + h + 1]
        m_prev = m_ref[:, h:h + 1]
        n_prev = n_ref[:, hs]
        m_new = jnp.maximum(lf + m_prev, li)
        w_in = jnp.exp(li - m_new)
        dec = jnp.exp(lf + m_prev - m_new)
        sc = jnp.sum(q * k, axis=-1, keepdims=True) * w_in
        den = sc + dec * jnp.sum(q * n_prev, axis=-1, keepdims=True)
        denom = jnp.maximum(jnp.abs(den), jnp.exp(-m_new))
        no_ref[:, hs] = dec * n_prev + w_in * k
        mo_ref[:, h:h + 1] = m_new
        kt = _columns(k)
        qt = _columns(q)
        wv = w_in * v
        for i in range(bg):
            c = c_ref[i, h]
            d_i = dec[i:i + 1, :]
            qc = jnp.sum(qt[:, i:i + 1] * c, axis=0, keepdims=True)
            ob_ref[i:i + 1, hs] = sc[i:i + 1, :] * v[i:i + 1, :] + d_i * qc
            co_ref[i, h] = d_i * c + kt[:, i:i + 1] * wv[i:i + 1, :]
        ob_ref[:, hs] = _sigmoid(og) * (ob_ref[:, hs] / denom)

    gu = _gelu_tanh(p_ref[:, COL_D_U:COL_D_U + D_MIX])
    gv = _gmlp_gv(p_ref[:, COL_D_V:COL_D_V + D_MIX], gmnw_ref[...])
    gv_ref[...] = gv
    od_ref[...] = gu * (gmw0_ref[...] * gv + gmb0_ref[...])

    cos = cos_ref[...]
    sin = sin_ref[...]
    qr_ref[...] = _rope(p_ref[:, COL_C_Q:COL_C_Q + D_MIX], cos, sin)
    kr_ref[...] = _rope(p_ref[:, COL_C_K:COL_C_K + D_MIX], cos, sin)


def _sample_state(proj, gates, conv_t, s0, c0, n0, m0, conv_w, alog_row, dtb_row, nw_row, gbi_row, gbf_row,
                  gm_nw_row, gm_w0_row, gm_b0_row, cos_row, sin_row, *, layer):
    bg = SAMPLE_GROUP
    nrows = proj.shape[0]
    full = lambda shape: pl.BlockSpec(shape, lambda i: (0,) * len(shape))
    lp = lambda shape: _layer_param(shape, layer)
    row_blk = lambda w: pl.BlockSpec((bg, w), lambda i: (i, 0))
    mat_blk = pl.BlockSpec((bg, H_MIX, HEAD_DIM, HEAD_DIM), lambda i: (i, 0, 0, 0))
    mat_in = pl.BlockSpec((None, bg, H_MIX, HEAD_DIM, HEAD_DIM), lambda i: (layer, i, 0, 0, 0))
    conv_blk = pl.BlockSpec((CONV_W - 1, bg, 3 * D_MIX), lambda i: (0, i, 0))
    conv_in = pl.BlockSpec((None, CONV_W - 1, bg, 3 * D_MIX), lambda i: (layer, 0, i, 0))
    row_in = lambda w: pl.BlockSpec((None, bg, w), lambda i: (layer, i, 0))
    f = lambda shape: jax.ShapeDtypeStruct(shape, F32)
    return pl.pallas_call(
        _sample_state_kernel,
        grid=(nrows // bg,),
        in_specs=[row_blk(N_MAIN), row_blk(LANES),
                  conv_in, mat_in, mat_in, row_in(D_MIX), row_in(H_MIX),
                  lp((CONV_W, 3 * D_MIX)), lp((1, LANES)), lp((1, LANES)), lp((1, HEAD_DIM)),
                  lp((1, LANES)), lp((1, LANES)),
                  lp((1, D_MIX)), lp((1, D_MIX)), lp((1, D_MIX)),
                  full((1, HEAD_DIM)), full((1, HEAD_DIM))],
        out_specs=[row_blk(D_MIX), row_blk(D_MIX), row_blk(D_MIX), conv_blk, mat_blk, mat_blk,
                   row_blk(D_MIX), row_blk(H_MIX), row_blk(D_MIX), row_blk(D_MIX), row_blk(D_MIX)],
        out_shape=[f((nrows, D_MIX)), f((nrows, D_MIX)), f((nrows, D_MIX)),
                   f((CONV_W - 1, nrows, 3 * D_MIX)),
                   f((nrows, H_MIX, HEAD_DIM, HEAD_DIM)), f((nrows, H_MIX, HEAD_DIM, HEAD_DIM)),
                   f((nrows, D_MIX)), f((nrows, H_MIX)),
                   f((nrows, D_MIX)), f((nrows, D_MIX)), f((nrows, D_MIX))],
        compiler_params=_cparams("parallel"),
        name="sample_state",
    )(proj, gates, conv_t, s0, c0, n0, m0, conv_w, alog_row, dtb_row, nw_row, gbi_row, gbf_row,
      gm_nw_row, gm_w0_row, gm_b0_row, cos_row, sin_row)


def _head_rows(row):
    sub = lax.broadcasted_iota(jnp.int32, (SUBLANES, HEAD_DIM), 0)
    out = jnp.zeros((SUBLANES, HEAD_DIM), F32)
    for h in range(H_MIX):
        out = jnp.where(sub == h, jnp.broadcast_to(row[:, h * HEAD_DIM:(h + 1) * HEAD_DIM], (SUBLANES, HEAD_DIM)), out)
    return out


def _moba_decode_row(n_pages, page_rows, q_row, kn_row, vn_row, k_refs, v_refs):
    page_size = page_rows // H_MIX
    pages_per_block = MOBA_BLOCK // page_size
    n_blocks = n_pages // pages_per_block
    scale = HEAD_DIM ** -0.5

    q8 = _head_rows(q_row)
    q8_bf = q8.astype(BF16)
    sub = lax.broadcasted_iota(jnp.int32, (SUBLANES, page_rows), 0)
    col = lax.broadcasted_iota(jnp.int32, (SUBLANES, page_rows), 1)
    head_mask = (col % H_MIX) == sub
    sub_d = lax.broadcasted_iota(jnp.int32, (SUBLANES, HEAD_DIM), 0)

    gates = []
    for j in range(n_blocks):
        fold = jnp.zeros((SUBLANES, HEAD_DIM), F32)
        for p in range(j * pages_per_block, (j + 1) * pages_per_block):
            fold = fold + jnp.sum(k_refs[p][0, 0].reshape(page_rows // SUBLANES, SUBLANES, HEAD_DIM), axis=0)
        ksum = fold
        for g in range(1, SUBLANES // H_MIX):
            ksum = ksum + pltpu.roll(fold, SUBLANES - g * H_MIX, 0)
        ksum = jnp.where(sub_d < H_MIX, ksum, 0.0)
        gates.append(jnp.sum(q8 * ksum, axis=1, keepdims=True) * (1.0 / MOBA_BLOCK))
    sels = []
    for n in range(n_blocks):
        rank = jnp.zeros((SUBLANES, 1), F32)
        for m in range(n_blocks):
            if m == n:
                continue
            ahead = gates[m] > gates[n]
            if m < n:
                ahead = ahead | (gates[m] == gates[n])
            rank = rank + jnp.where(ahead, 1.0, 0.0)
        sels.append(rank < float(MOBA_TOPK))

    keeps = [head_mask & sels[p // pages_per_block] for p in range(n_pages)]
    scores = [jnp.where(keeps[p],
                        lax.dot_general(q8_bf, k_refs[p][0, 0].astype(BF16), (((1,), (1,)), ((), ())),
                                        preferred_element_type=F32) * scale, NEG_BIG)
              for p in range(n_pages)]
    s_own = jnp.sum(q8 * _head_rows(kn_row), axis=1, keepdims=True) * scale
    m_all = s_own
    for s in scores:
        m_all = jnp.maximum(m_all, jnp.max(s, axis=1, keepdims=True))
    probs = [jnp.where(keeps[p], jnp.exp(scores[p] - m_all), 0.0) for p in range(n_pages)]
    e_own = jnp.exp(s_own - m_all)
    l_all = e_own
    acc = e_own * _head_rows(vn_row)
    for p in range(n_pages):
        l_all = l_all + jnp.sum(probs[p], axis=1, keepdims=True)
        acc = acc + _dot(probs[p], v_refs[p][0, 0])
    return acc / l_all


def _moba_decode_kernel(rows_per_step, n_pages, page_rows, pt_ref, q_ref, kn_ref, vn_ref, *refs):
    del pt_ref
    o_ref = refs[2 * rows_per_step * n_pages]
    for r in range(rows_per_step):
        k_refs = refs[r * n_pages:(r + 1) * n_pages]
        v_refs = refs[(rows_per_step + r) * n_pages:(rows_per_step + r + 1) * n_pages]
        out = _moba_decode_row(n_pages, page_rows, q_ref[r], kn_ref[r], vn_ref[r], k_refs, v_refs)
        for h in range(H_MIX):
            o_ref[r, :, h * HEAD_DIM:(h + 1) * HEAD_DIM] = out[h:h + 1, :]


DECODE_ROWS_PER_STEP = 2


def _moba_decode(page_table, q_rope, k_new, v_new, cache_k, cache_v, *, layer):
    bsz, n_pages = page_table.shape
    page_rows = cache_k.shape[2]
    page_size = page_rows // H_MIX
    rps = DECODE_ROWS_PER_STEP if bsz % DECODE_ROWS_PER_STEP == 0 else 1
    assert MOBA_BLOCK % page_size == 0 and (n_pages * page_size) % MOBA_BLOCK == 0
    assert (n_pages * page_size) // MOBA_BLOCK >= MOBA_TOPK and SUBLANES % H_MIX == 0
    row = pl.BlockSpec((rps, 1, D_MIX), lambda b, pt: (b, 0, 0))
    page_specs = [pl.BlockSpec((1, 1, page_rows, HEAD_DIM),
                               lambda b, pt, r=r, p=p: (layer, pt[b * rps + r, p], 0, 0))
                  for r in range(rps) for p in range(n_pages)]
    return pl.pallas_call(
        functools.partial(_moba_decode_kernel, rps, n_pages, page_rows),
        grid_spec=pltpu.PrefetchScalarGridSpec(
            num_scalar_prefetch=1,
            grid=(bsz // rps,),
            in_specs=[row, row, row] + page_specs + page_specs,
            out_specs=row),
        out_shape=jax.ShapeDtypeStruct((bsz, 1, D_MIX), F32),
        compiler_params=_cparams("parallel"),
        name="moba_decode",
    )(page_table, q_rope, k_new, v_new, *([cache_k] * (rps * n_pages)), *([cache_v] * (rps * n_pages)))


def _largest_divisor(n, candidates):
    for c in candidates:
        if n % c == 0:
            return c
    raise ValueError(f"no tile in {candidates} divides {n}")


def _tiles(rows, seqlen):
    return dict(
        proj_tm=_largest_divisor(rows, (1024, 512, 256, 128)),
        proj_tn=512 if rows > 512 else N_MAIN // 2,
        out_tm=_largest_divisor(rows, (512, 256, 128)),
        ffn_tm=_largest_divisor(rows, (512, 256, 128)),
        ffn_tf=1024,
        scan_tb=_largest_divisor(seqlen, (256, 128, 64, 32, 16, 8)),
        mlstm_tb=_largest_divisor(seqlen, (512, 256, 128, 64, 32, 16, 8)),
        gmlp_tb=_largest_divisor(seqlen, (512, 256, 128)),
    )


def _lane_rows(vals, offset):
    out = jnp.zeros((vals.shape[0], 1, LANES), F32)
    return out.at[:, 0, offset:offset + vals.shape[1]].set(vals.astype(F32))


def kernel(x_prompt, x_sample, state_gdn_conv, state_gdn_s, state_mlstm_c, state_mlstm_n, state_mlstm_m,
           cache_k, cache_v, page_table, w_in, gdn_conv_w, gdn_a_log, gdn_dt_bias, gdn_norm_w, mlstm_gate_b,
           gmlp_norm_w, gmlp_ws, gmlp_b, w_out, ln1_w, ln1_b, w_up, w_down, ln2_w, ln2_b):
    bsz, seq, d_model = x_prompt.shape
    dec_b, dec_s, _ = x_sample.shape
    depth = w_in.shape[0]
    assert dec_s == 1 and d_model == N_MIXERS * D_MIX and seq >= CONV_W - 1
    assert seq % MOBA_BLOCK == 0 and seq // MOBA_BLOCK <= KMEAN_ROWS and seq % GM_CHUNK == 0
    assert w_in.shape[2] == N_MAIN + 4 * H_MIX and cache_k.shape[3] == H_MIX
    alpha = (2.0 * depth) ** 0.25
    mp = bsz * seq
    past_len = page_table.shape[1] * cache_k.shape[2]
    n_pool, page_size = cache_k.shape[1], cache_k.shape[2]
    tp = _tiles(mp, seq)
    ts = _tiles(dec_b, seq)
    gw = D_MIX // GM_GROUPS

    cos_p, sin_p = _rope_tables(jnp.arange(seq, dtype=jnp.int32))
    cos_s, sin_s = _rope_tables(past_len + jnp.arange(dec_s, dtype=jnp.int32))
    ck = cache_k.reshape(depth, n_pool, page_size * H_MIX, HEAD_DIM)
    cv = cache_v.reshape(depth, n_pool, page_size * H_MIX, HEAD_DIM)

    w_main, w_gate = _pack_w_in(jnp.transpose(w_in.astype(F32), (0, 2, 1)), tc=_largest_divisor(d_model, (256, 128)))
    w_out_bf = w_out.astype(BF16)
    w_up_bf = w_up.astype(BF16)
    w_down_bf = w_down.astype(BF16)
    gdn_s_in = state_gdn_s.astype(F32)
    ml_c_in = state_mlstm_c.astype(F32)

    alog_row = _lane_rows(gdn_a_log, GATE_A_DEC)
    dtb_row = _lane_rows(gdn_dt_bias, GATE_A_DEC)
    nw_row = gdn_norm_w.reshape(depth, 1, HEAD_DIM).astype(F32)
    gbi_row = _lane_rows(mlstm_gate_b[:, :H_MIX], GATE_B_I)
    gbf_row = _lane_rows(mlstm_gate_b[:, H_MIX:], GATE_B_F)
    gm_nw_row = gmlp_norm_w.reshape(depth, 1, D_MIX).astype(F32)
    gm_ws = gmlp_ws.astype(F32)
    gm_bt = jnp.transpose(gmlp_b.astype(F32), (0, 2, 1))
    gm_w0_row = jnp.repeat(gmlp_ws[:, :, 0, 0].astype(F32), gw, axis=1).reshape(depth, 1, D_MIX)
    gm_b0_row = jnp.repeat(gmlp_b[:, :, 0].astype(F32), gw, axis=1).reshape(depth, 1, D_MIX)
    conv_w = gdn_conv_w.astype(F32)
    ln1 = (ln1_w.reshape(depth, 1, d_model).astype(F32), ln1_b.reshape(depth, 1, d_model).astype(F32))
    ln2 = (ln2_w.reshape(depth, 1, d_model).astype(F32), ln2_b.reshape(depth, 1, d_model).astype(F32))
    conv_in = jnp.transpose(state_gdn_conv.astype(F32), (0, 2, 1, 3))
    ml_n_in = state_mlstm_n.astype(F32).reshape(depth, dec_b, D_MIX)
    ml_m_in = state_mlstm_m.astype(F32)

    xp = x_prompt.reshape(mp, d_model)
    xs = x_sample.reshape(dec_b, d_model)
    xp_in, xs_in = xp, xs
    p_st, s_st = [], []
    for l in range(depth):
        proj, gates = _proj(xp_in, w_main, w_gate, layer=l, tm=tp["proj_tm"], tn=tp["proj_tn"])
        oa, gdn_s_p = _gdn_prompt(proj, gates, conv_w, alog_row, dtb_row, nw_row,
                                  layer=l, bsz=bsz, seqlen=seq, tb=tp["scan_tb"])
        ob, ml_c_p, ml_n_p, ml_m_p = _mlstm_prompt(proj, gates, gbi_row, gbf_row,
                                                   layer=l, bsz=bsz, seqlen=seq, tb=tp["mlstm_tb"])
        q_rope, k_rows, v_rows, k_bf, v_bf, kmean = _rope_prompt(proj, cos_p, sin_p, bsz=bsz, seqlen=seq)
        oc = _moba_prompt(q_rope, k_bf, v_bf, kmean, bsz=bsz, seqlen=seq)
        od = _gmlp_prompt(proj, gm_nw_row, gm_ws, gm_bt, layer=l, rows=mp, tb=tp["gmlp_tb"])
        x1, x1_bf = _outproj_ln((oa, ob, oc, od), w_out_bf, xp, *ln1, layer=l, alpha=alpha, tm=tp["out_tm"])
        xp, xp_in = _ffn_ln(x1_bf, w_up_bf, w_down_bf, x1, *ln2, layer=l, alpha=alpha,
                            tm=tp["ffn_tm"], tf=tp["ffn_tf"])
        proj_p = proj.reshape(bsz, seq, N_MAIN)
        p_st.append((proj_p[:, seq - (CONV_W - 1):, COL_A_Q:COL_A_Q + 3 * D_MIX],
                     gdn_s_p, ml_c_p, ml_n_p[:, :H_MIX], ml_m_p[:, :H_MIX, 0],
                     k_rows.reshape(bsz, seq, H_MIX, HEAD_DIM), v_rows.reshape(bsz, seq, H_MIX, HEAD_DIM)))

        proj, gates = _proj(xs_in, w_main, w_gate, layer=l, tm=ts["proj_tm"], tn=ts["proj_tn"])
        (oa, ob, od, conv_s, gdn_s_s, ml_c_s, ml_n_s, ml_m_s, q_s, k_s, gv_s) = _sample_state(
            proj, gates, conv_in, gdn_s_in, ml_c_in, ml_n_in, ml_m_in,
            conv_w, alog_row, dtb_row, nw_row, gbi_row, gbf_row, gm_nw_row, gm_w0_row, gm_b0_row,
            cos_s, sin_s, layer=l)
        v_s = proj[:, COL_C_V:COL_C_V + D_MIX]
        oc = _moba_decode(page_table, q_s.reshape(dec_b, 1, D_MIX), k_s.reshape(dec_b, 1, D_MIX),
                          v_s.reshape(dec_b, 1, D_MIX), ck, cv, layer=l).reshape(dec_b, D_MIX)
        x1, x1_bf = _outproj_ln((oa, ob, oc, od), w_out_bf, xs, *ln1, layer=l, alpha=alpha, tm=ts["out_tm"])
        xs, xs_in = _ffn_ln(x1_bf, w_up_bf, w_down_bf, x1, *ln2, layer=l, alpha=alpha,
                            tm=ts["ffn_tm"], tf=ts["ffn_tf"])
        s_st.append((jnp.transpose(conv_s, (1, 0, 2)), gdn_s_s, ml_c_s,
                     ml_n_s.reshape(dec_b, H_MIX, HEAD_DIM), ml_m_s,
                     k_s.reshape(dec_b, dec_s, H_MIX, HEAD_DIM), v_s.reshape(dec_b, dec_s, H_MIX, HEAD_DIM),
                     gv_s.reshape(dec_b, dec_s, D_MIX)))

    def stk(sts, i):
        return jnp.stack([s[i] for s in sts], axis=0)

    dt = x_prompt.dtype
    yp = xp.reshape(bsz, seq, d_model).astype(dt)
    ys = xs.reshape(dec_b, dec_s, d_model).astype(dt)
    return (yp, ys) + tuple(stk(p_st, i).astype(dt) for i in range(7)) + tuple(stk(s_st, i).astype(dt) for i in range(8))
```

```python
import functools
import math

import jax
import jax.numpy as jnp
from jax import lax
from jax.experimental import pallas as pl
from jax.experimental.pallas import tpu as pltpu

F32 = jnp.float32
BF16 = jnp.bfloat16
HIGHEST = lax.Precision.HIGHEST

HEAD_DIM = 128
N_MIXERS = 4
CONV_W = 4
GDN_CHUNK = 64
MLSTM_CHUNK = 64
MOBA_BLOCK = 256
MOBA_TOPK = 3
GM_CHUNK = 128
GM_GROUPS = 4
ROPE_THETA = 500000.0
ROT_DIM = HEAD_DIM // 4
LN_EPS = 1e-5
L2_EPS = 1e-6
NEG_BIG = -1e30

LANES = 128
SUBLANES = 8
VMEM_LIMIT_BYTES = 56 * 1024 * 1024


def _cparams(*sem):
    return pltpu.CompilerParams(dimension_semantics=sem, vmem_limit_bytes=VMEM_LIMIT_BYTES)


def _dot(a, b):
    return jnp.dot(a.astype(BF16), b.astype(BF16), preferred_element_type=F32)


def _dot_nt(a, b):
    return lax.dot_general(a.astype(BF16), b.astype(BF16), (((1,), (1,)), ((), ())),
                           preferred_element_type=F32)


def _dot_tn(a, b):
    return lax.dot_general(a.astype(BF16), b.astype(BF16), (((0,), (0,)), ((), ())),
                           preferred_element_type=F32)


def _dot_nt_hi(a, b):
    return lax.dot_general(a, b, (((1,), (1,)), ((), ())), precision=HIGHEST,
                           preferred_element_type=F32)


def _split2(x):
    hi = x.astype(BF16)
    return hi, (x - hi.astype(F32)).astype(BF16)


def _dot_split(a_parts, b_parts):
    ah, al = a_parts
    bh, bl = b_parts
    return (jnp.dot(ah, bh, preferred_element_type=F32) + jnp.dot(ah, bl, preferred_element_type=F32)
            + jnp.dot(al, bh, preferred_element_type=F32))


def _cumsum_rows(tril_bf, x):
    x0 = x.astype(BF16)
    r1 = x - x0.astype(F32)
    x1 = r1.astype(BF16)
    x2 = (r1 - x1.astype(F32)).astype(BF16)
    return (jnp.dot(tril_bf, x0, preferred_element_type=F32) + jnp.dot(tril_bf, x1, preferred_element_type=F32)
            + jnp.dot(tril_bf, x2, preferred_element_type=F32))


def _sigmoid(x):
    return 1.0 / (1.0 + jnp.exp(-x))


def _silu(x):
    return x * _sigmoid(x)


def _softplus(x):
    return jnp.maximum(x, 0.0) + jnp.log(1.0 + jnp.exp(-jnp.abs(x)))


def _log_sigmoid(x):
    return -_softplus(-x)


def _gelu_tanh(x):
    return 0.5 * x * (1.0 + jnp.tanh(math.sqrt(2.0 / math.pi) * (x + 0.044715 * (x * x * x))))


def _layer_norm(x, w, b):
    mu = jnp.mean(x, axis=-1, keepdims=True)
    xc = x - mu
    var = jnp.mean(xc * xc, axis=-1, keepdims=True)
    return xc * lax.rsqrt(var + LN_EPS) * w + b


D_MIX = 512
H_MIX = D_MIX // HEAD_DIM
COL_A_Q, COL_A_K, COL_A_V, COL_A_Z = 0, 512, 1024, 1536
COL_B_Q, COL_B_K, COL_B_V, COL_B_O = 2048, 2560, 3072, 3584
COL_C_Q, COL_C_K, COL_C_V = 4096, 4608, 5120
COL_D_U, COL_D_V = 5632, 6144
N_MAIN = 6656
GATE_A_BETA, GATE_A_DEC, GATE_B_I, GATE_B_F = 0, 4, 8, 12


def _layer_param(shape, layer):
    return pl.BlockSpec((None,) + tuple(shape), lambda *_: (layer,) + (0,) * len(shape))


def _tri_masks(n):
    r = lax.broadcasted_iota(jnp.int32, (n, n), 0)
    c = lax.broadcasted_iota(jnp.int32, (n, n), 1)
    return r >= c, r > c


def _pack_w_in_kernel(w_ref, main_ref, gate_ref):
    a_end = 4 * D_MIX
    b0 = a_end + 2 * H_MIX
    b_end = b0 + 4 * D_MIX
    c0 = b_end + 2 * H_MIX
    main_ref[0, 0:a_end, :] = w_ref[0, 0:a_end, :].astype(BF16)
    main_ref[0, a_end:2 * a_end, :] = w_ref[0, b0:b_end, :].astype(BF16)
    main_ref[0, 2 * a_end:N_MAIN, :] = w_ref[0, c0:c0 + N_MAIN - 2 * a_end, :].astype(BF16)
    cols = w_ref.shape[2]
    gate_ref[0] = jnp.concatenate([w_ref[0, a_end:b0, :], w_ref[0, b_end:c0, :],
                                   jnp.zeros((LANES - 4 * H_MIX, cols), F32)], axis=0).astype(BF16)


def _pack_w_in(w_in_t, *, tc):
    depth, n_in, d = w_in_t.shape
    return pl.pallas_call(
        _pack_w_in_kernel,
        grid=(depth, d // tc),
        in_specs=[pl.BlockSpec((1, n_in, tc), lambda l, i: (l, 0, i))],
        out_specs=[pl.BlockSpec((1, N_MAIN, tc), lambda l, i: (l, 0, i)),
                   pl.BlockSpec((1, LANES, tc), lambda l, i: (l, 0, i))],
        out_shape=[jax.ShapeDtypeStruct((depth, N_MAIN, d), BF16),
                   jax.ShapeDtypeStruct((depth, LANES, d), BF16)],
        compiler_params=_cparams("parallel", "parallel"),
        name="pack_w_in",
    )(w_in_t)


def _proj_kernel(x_ref, w_ref, wg_ref, o_ref, g_ref, xb_ref):
    nt = (((1,), (1,)), ((), ()))

    @pl.when(pl.program_id(1) == 0)
    def _():
        xb_ref[...] = x_ref[...].astype(BF16)
        g_ref[...] = lax.dot_general(xb_ref[...], wg_ref[0], nt, preferred_element_type=F32)

    o_ref[...] = lax.dot_general(xb_ref[...], w_ref[0], nt, preferred_element_type=F32)


def _proj(x, w_main_t, w_gate_t, *, layer, tm, tn):
    m, d = x.shape
    n = w_main_t.shape[1]
    return pl.pallas_call(
        _proj_kernel,
        grid=(m // tm, n // tn),
        in_specs=[pl.BlockSpec((tm, d), lambda i, j: (i, 0)),
                  pl.BlockSpec((1, tn, d), lambda i, j: (layer, j, 0)),
                  pl.BlockSpec((1, LANES, d), lambda i, j: (layer, 0, 0))],
        out_specs=[pl.BlockSpec((tm, tn), lambda i, j: (i, j)),
                   pl.BlockSpec((tm, LANES), lambda i, j: (i, 0))],
        out_shape=[jax.ShapeDtypeStruct((m, n), F32), jax.ShapeDtypeStruct((m, LANES), F32)],
        scratch_shapes=[pltpu.VMEM((tm, d), BF16)],
        compiler_params=_cparams("parallel", "arbitrary"),
        name="proj",
    )(x, w_main_t, w_gate_t)


def _outproj_ln_kernel(alpha, a_ref, b_ref, c_ref, d_ref, w_ref, x_ref, lw_ref, lb_ref, o_ref):
    y = alpha * x_ref[...]
    for i, m_ref in enumerate((a_ref, b_ref, c_ref, d_ref)):
        y = y + jnp.dot(m_ref[...].astype(BF16), w_ref[0, i * D_MIX:(i + 1) * D_MIX, :],
                        preferred_element_type=F32)
    o_ref[...] = _layer_norm(y, lw_ref[...], lb_ref[...])


def _outproj_ln(mixes, w_out_bf, x, ln_w, ln_b, *, layer, alpha, tm):
    m, d = x.shape
    mix_spec = pl.BlockSpec((tm, D_MIX), lambda i: (i, 0))
    return pl.pallas_call(
        functools.partial(_outproj_ln_kernel, alpha),
        grid=(m // tm,),
        in_specs=[mix_spec, mix_spec, mix_spec, mix_spec,
                  pl.BlockSpec((1, d, d), lambda i: (layer, 0, 0)),
                  pl.BlockSpec((tm, d), lambda i: (i, 0)),
                  _layer_param((1, d), layer), _layer_param((1, d), layer)],
        out_specs=pl.BlockSpec((tm, d), lambda i: (i, 0)),
        out_shape=jax.ShapeDtypeStruct((m, d), F32),
        compiler_params=_cparams("parallel"),
        name="outproj_ln",
    )(*mixes, w_out_bf, x, ln_w, ln_b)


def _ffn_ln_kernel(alpha, x_ref, wu_ref, wd_ref, lw_ref, lb_ref, o_ref, acc_ref, xb_ref):
    j = pl.program_id(1)

    @pl.when(j == 0)
    def _():
        acc_ref[...] = alpha * x_ref[...]
        xb_ref[...] = x_ref[...].astype(BF16)

    h = jnp.dot(xb_ref[...], wu_ref[0], preferred_element_type=F32)
    h = jnp.maximum(h, 0.0)
    h = (h * h).astype(BF16)
    acc_ref[...] += jnp.dot(h, wd_ref[0], preferred_element_type=F32)

    @pl.when(j == pl.num_programs(1) - 1)
    def _():
        o_ref[...] = _layer_norm(acc_ref[...], lw_ref[...], lb_ref[...])


def _ffn_ln(x, w_up_bf, w_down_bf, ln_w, ln_b, *, layer, alpha, tm, tf):
    m, d = x.shape
    f = w_up_bf.shape[2]
    return pl.pallas_call(
        functools.partial(_ffn_ln_kernel, alpha),
        grid=(m // tm, f // tf),
        in_specs=[pl.BlockSpec((tm, d), lambda i, j: (i, 0)),
                  pl.BlockSpec((1, d, tf), lambda i, j: (layer, 0, j)),
                  pl.BlockSpec((1, tf, d), lambda i, j: (layer, j, 0)),
                  _layer_param((1, d), layer), _layer_param((1, d), layer)],
        out_specs=pl.BlockSpec((tm, d), lambda i, j: (i, 0)),
        out_shape=jax.ShapeDtypeStruct((m, d), F32),
        scratch_shapes=[pltpu.VMEM((tm, d), F32), pltpu.VMEM((tm, d), BF16)],
        compiler_params=_cparams("parallel", "arbitrary"),
        name="ffn_ln",
    )(x, w_up_bf, w_down_bf, ln_w, ln_b)


def _unit_lower_inverses(mats):
    n = mats[0].shape[0]
    r = lax.broadcasted_iota(jnp.int32, (n, n), 0)
    c = lax.broadcasted_iota(jnp.int32, (n, n), 1)
    eye = jnp.where(r == c, 1.0, 0.0).astype(F32)
    ts = [eye - a for a in mats]
    ps = [_split2(a) for a in mats]
    span = 2
    while span < n:
        ps = [_split2(_dot_split(p, p)) for p in ps]
        yield
        ts = [t + _dot_split(_split2(t), p) for t, p in zip(ts, ps)]
        yield
        span *= 2
    return ts


def _gdn_prompt_kernel(tb, cl, x_ref, g_ref, cw_ref, alog_ref, dtb_ref, nw_ref,
                       o_ref, s_out_ref, xbuf, s_ref):
    t = pl.program_id(1)
    dqkv = 3 * D_MIX

    @pl.when(t == 0)
    def _():
        xbuf[0:SUBLANES, :] = jnp.zeros((SUBLANES, dqkv), F32)
        s_ref[...] = jnp.zeros_like(s_ref)

    @pl.when(t > 0)
    def _():
        xbuf[0:SUBLANES, :] = xbuf[tb:tb + SUBLANES, :]

    xbuf[SUBLANES:SUBLANES + tb, :] = x_ref[:, 0:dqkv]
    cw = cw_ref[...]
    y = cw[0:1] * xbuf[SUBLANES - 3:SUBLANES - 3 + tb, :]
    for j in range(1, CONV_W):
        y = y + cw[j:j + 1] * xbuf[SUBLANES - 3 + j:SUBLANES - 3 + j + tb, :]
    y = _silu(y)

    gates = g_ref[...]
    beta_all = _sigmoid(gates)
    g_all = -jnp.exp(alog_ref[...]) * _softplus(gates + dtb_ref[...])
    tri, tri_s = _tri_masks(cl)
    tril_bf = jnp.where(tri, 1.0, 0.0).astype(BF16)
    nw = nw_ref[...]
    states = [s_ref[h] for h in range(H_MIX)]

    a_mats, pre = [], []
    for c in range(tb // cl):
        r0 = c * cl
        gcum = _cumsum_rows(tril_bf, g_all[r0:r0 + cl, :])
        gcum_t = gcum.T
        for h in range(H_MIX):
            q = y[r0:r0 + cl, COL_A_Q + h * HEAD_DIM:COL_A_Q + (h + 1) * HEAD_DIM]
            k = y[r0:r0 + cl, COL_A_K + h * HEAD_DIM:COL_A_K + (h + 1) * HEAD_DIM]
            v = y[r0:r0 + cl, COL_A_V + h * HEAD_DIM:COL_A_V + (h + 1) * HEAD_DIM]
            q = q * lax.rsqrt(jnp.sum(q * q, axis=-1, keepdims=True) + L2_EPS) * (HEAD_DIM ** -0.5)
            k = k * lax.rsqrt(jnp.sum(k * k, axis=-1, keepdims=True) + L2_EPS)
            beta = beta_all[r0:r0 + cl, GATE_A_BETA + h:GATE_A_BETA + h + 1]
            gc = gcum[:, GATE_A_DEC + h:GATE_A_DEC + h + 1]
            gr = gcum_t[GATE_A_DEC + h:GATE_A_DEC + h + 1, :]
            decay = jnp.exp(jnp.where(tri, gc - gr, NEG_BIG))
            kb = k * beta
            egc = jnp.exp(gc)
            gc_last = gc[cl - 1:cl, :]
            kk = _dot_nt(jnp.concatenate([kb, q], axis=0), k)
            a_mats.append(jnp.where(tri_s, kk[0:cl] * decay, 0.0))
            pre.append(dict(attn=jnp.where(tri, kk[cl:2 * cl] * decay, 0.0),
                            rhs=jnp.concatenate([v * beta, kb * egc], axis=1),
                            q_dec=q * egc, k_tail_t=(k * jnp.exp(gc_last - gc)).T, g_tot=jnp.exp(gc_last)))
        yield
    t_invs = yield from _unit_lower_inverses(a_mats)
    uws = [_dot(t_inv, p["rhs"]) for t_inv, p in zip(t_invs, pre)]

    for c in range(tb // cl):
        r0 = c * cl
        for h in range(H_MIX):
            p, uw = pre[c * H_MIX + h], uws[c * H_MIX + h]
            z = x_ref[r0:r0 + cl, COL_A_Z + h * HEAD_DIM:COL_A_Z + (h + 1) * HEAD_DIM]
            s = states[h]
            ws_qs = _dot(jnp.concatenate([uw[:, HEAD_DIM:], p["q_dec"]], axis=0), s)
            v_new = uw[:, :HEAD_DIM] - ws_qs[0:cl]
            o = ws_qs[cl:2 * cl] + _dot(p["attn"], v_new)
            states[h] = s * p["g_tot"] + _dot(p["k_tail_t"], v_new)
            o = o * lax.rsqrt(jnp.mean(o * o, axis=-1, keepdims=True) + LN_EPS) * nw
            o_ref[r0:r0 + cl, h * HEAD_DIM:(h + 1) * HEAD_DIM] = (o * _silu(z)).astype(o_ref.dtype)
        yield

    for h in range(H_MIX):
        s_ref[h] = states[h]

    @pl.when(t == pl.num_programs(1) - 1)
    def _():
        s_out_ref[0] = s_ref[...]


def _mlstm_prompt_kernel(tb, cl, x_ref, g_ref, gbi_ref, gbf_ref,
                         o_ref, c_out_ref, n_out_ref, m_out_ref, c_ref, n_ref, m_ref):
    t = pl.program_id(1)

    @pl.when(t == 0)
    def _():
        c_ref[...] = jnp.zeros_like(c_ref)
        n_ref[...] = jnp.zeros_like(n_ref)
        m_ref[...] = jnp.zeros_like(m_ref)

    gates = g_ref[...]
    li_all = gates + gbi_ref[...]
    lf_all = _log_sigmoid(gates + gbf_ref[...])
    tri, _ = _tri_masks(cl)
    tril_bf = jnp.where(tri, 1.0, 0.0).astype(BF16)
    cs = [c_ref[h] for h in range(H_MIX)]
    ns = [n_ref[h:h + 1, :] for h in range(H_MIX)]
    ms = [m_ref[h:h + 1, 0:1] for h in range(H_MIX)]

    for c in range(tb // cl):
        r0 = c * cl
        bcum = _cumsum_rows(tril_bf, lf_all[r0:r0 + cl, :])
        bcum_t = bcum.T
        li_t = li_all[r0:r0 + cl, :].T
        for h in range(H_MIX):
            q = x_ref[r0:r0 + cl, h * HEAD_DIM:(h + 1) * HEAD_DIM]
            k = x_ref[r0:r0 + cl, D_MIX + h * HEAD_DIM:D_MIX + (h + 1) * HEAD_DIM] * (HEAD_DIM ** -0.5)
            v = x_ref[r0:r0 + cl, 2 * D_MIX + h * HEAD_DIM:2 * D_MIX + (h + 1) * HEAD_DIM]
            og = x_ref[r0:r0 + cl, 3 * D_MIX + h * HEAD_DIM:3 * D_MIX + (h + 1) * HEAD_DIM]
            b = bcum[:, GATE_B_F + h:GATE_B_F + h + 1]
            br = bcum_t[GATE_B_F + h:GATE_B_F + h + 1, :]
            ic = li_all[r0:r0 + cl, GATE_B_I + h:GATE_B_I + h + 1]
            ir = li_t[GATE_B_I + h:GATE_B_I + h + 1, :]
            m_prev = ms[h]
            dlog = jnp.where(tri, b - br + ir, NEG_BIG)
            inter = b + m_prev
            m = jnp.maximum(jnp.max(dlog, axis=1, keepdims=True), inter)
            s = _dot_nt(q, k) * jnp.exp(dlog - m)
            scale_prev = jnp.exp(inter - m)
            num = _dot(s, v) + scale_prev * _dot(q, cs[h])
            den = jnp.sum(s, axis=1, keepdims=True) + scale_prev * jnp.sum(q * ns[h], axis=1, keepdims=True)
            hh = num / jnp.maximum(jnp.abs(den), jnp.exp(-m))
            b_end = b[cl - 1:cl, :]
            wlog = b_end - b + ic
            m_new = jnp.maximum(b_end + m_prev, jnp.max(wlog, axis=0, keepdims=True))
            wk = jnp.exp(wlog - m_new) * k
            dec = jnp.exp(b_end + m_prev - m_new)
            cs[h] = dec * cs[h] + _dot_tn(wk, v)
            ns[h] = dec * ns[h] + jnp.sum(wk, axis=0, keepdims=True)
            ms[h] = m_new
            o_ref[r0:r0 + cl, h * HEAD_DIM:(h + 1) * HEAD_DIM] = (_sigmoid(og) * hh).astype(o_ref.dtype)
            yield

    for h in range(H_MIX):
        c_ref[h] = cs[h]
        n_ref[h:h + 1, :] = ns[h]
        m_ref[h:h + 1, :] = jnp.broadcast_to(ms[h], (1, LANES))

    @pl.when(t == pl.num_programs(1) - 1)
    def _():
        c_out_ref[0] = c_ref[...]
        n_out_ref[0] = n_ref[...]
        m_out_ref[0] = m_ref[...]


def _scan_prompt_kernel(tb, cl, xa_ref, xb_ref, g_ref, cw_ref, alog_ref, dtb_ref, nw_ref, gbi_ref, gbf_ref,
                        oa_ref, s_out_ref, ob_ref, c_out_ref, n_out_ref, m_out_ref,
                        xbuf, s_ref, c_ref, n_ref, m_ref):
    parts = [_gdn_prompt_kernel(tb, cl, xa_ref, g_ref, cw_ref, alog_ref, dtb_ref, nw_ref, oa_ref, s_out_ref, xbuf, s_ref),
             _mlstm_prompt_kernel(tb, cl, xb_ref, g_ref, gbi_ref, gbf_ref, ob_ref, c_out_ref, n_out_ref, m_out_ref,
                                  c_ref, n_ref, m_ref)]
    while parts:
        for part in list(parts):
            if next(part, StopIteration) is StopIteration:
                parts.remove(part)


def _scan_prompt(proj, gates, conv_w, alog_row, dtb_row, nw_row, gbi_row, gbf_row, *, layer, bsz, seqlen, tb):
    cl = math.gcd(seqlen, GDN_CHUNK)
    assert cl == math.gcd(seqlen, MLSTM_CHUNK)
    nt = seqlen // tb
    rows = lambda c: pl.BlockSpec((tb, 4 * D_MIX), lambda b, t: (b * nt + t, c // (4 * D_MIX)))
    out_rows = pl.BlockSpec((tb, D_MIX), lambda b, t: (b * nt + t, 0))
    mat = pl.BlockSpec((1, H_MIX, HEAD_DIM, HEAD_DIM), lambda b, t: (b, 0, 0, 0))
    vec = pl.BlockSpec((1, SUBLANES, LANES), lambda b, t: (b, 0, 0))
    return pl.pallas_call(
        functools.partial(_scan_prompt_kernel, tb, cl),
        grid=(bsz, nt),
        in_specs=[rows(COL_A_Q), rows(COL_B_Q),
                  pl.BlockSpec((tb, LANES), lambda b, t: (b * nt + t, 0)),
                  _layer_param((CONV_W, 3 * D_MIX), layer),
                  _layer_param((1, LANES), layer), _layer_param((1, LANES), layer),
                  _layer_param((1, HEAD_DIM), layer),
                  _layer_param((1, LANES), layer), _layer_param((1, LANES), layer)],
        out_specs=[out_rows, mat, out_rows, mat, vec, vec],
        out_shape=[jax.ShapeDtypeStruct((bsz * seqlen, D_MIX), BF16),
                   jax.ShapeDtypeStruct((bsz, H_MIX, HEAD_DIM, HEAD_DIM), F32),
                   jax.ShapeDtypeStruct((bsz * seqlen, D_MIX), BF16),
                   jax.ShapeDtypeStruct((bsz, H_MIX, HEAD_DIM, HEAD_DIM), F32),
                   jax.ShapeDtypeStruct((bsz, SUBLANES, HEAD_DIM), F32),
                   jax.ShapeDtypeStruct((bsz, SUBLANES, LANES), F32)],
        scratch_shapes=[pltpu.VMEM((tb + SUBLANES, 3 * D_MIX), F32),
                        pltpu.VMEM((H_MIX, HEAD_DIM, HEAD_DIM), F32),
                        pltpu.VMEM((H_MIX, HEAD_DIM, HEAD_DIM), F32),
                        pltpu.VMEM((SUBLANES, HEAD_DIM), F32),
                        pltpu.VMEM((SUBLANES, LANES), F32)],
        compiler_params=_cparams("parallel", "arbitrary"),
        name="scan_prompt",
    )(proj, proj, gates, conv_w, alog_row, dtb_row, nw_row, gbi_row, gbf_row)


def _gmlp_gv(dv, nw):
    gv = _gelu_tanh(dv)
    mu = jnp.mean(gv, axis=-1, keepdims=True)
    gc = gv - mu
    var = jnp.mean(gc * gc, axis=-1, keepdims=True)
    return gc * lax.rsqrt(var + LN_EPS) * nw


def _gmlp_prompt_kernel(tb, u_ref, v_ref, nw_ref, ws_ref, bt_ref, o_ref):
    gu = _gelu_tanh(u_ref[...])
    gv = _gmlp_gv(v_ref[...], nw_ref[...])
    tri, _ = _tri_masks(GM_CHUNK)
    gw = D_MIX // GM_GROUPS
    for g in range(GM_GROUPS):
        wm = jnp.where(tri, ws_ref[g], 0.0)
        bias = bt_ref[:, g:g + 1]
        for c in range(tb // GM_CHUNK):
            r0 = c * GM_CHUNK
            z = _dot(wm, gv[r0:r0 + GM_CHUNK, g * gw:(g + 1) * gw]) + bias
            o_ref[r0:r0 + GM_CHUNK, g * gw:(g + 1) * gw] = (
                gu[r0:r0 + GM_CHUNK, g * gw:(g + 1) * gw] * z).astype(o_ref.dtype)


def _gmlp_prompt(proj, nw_row, ws, b_t, *, layer, rows, tb):
    return pl.pallas_call(
        functools.partial(_gmlp_prompt_kernel, tb),
        grid=(rows // tb,),
        in_specs=[pl.BlockSpec((tb, D_MIX), lambda i: (i, COL_D_U // D_MIX)),
                  pl.BlockSpec((tb, D_MIX), lambda i: (i, COL_D_V // D_MIX)),
                  _layer_param((1, D_MIX), layer),
                  _layer_param((GM_GROUPS, GM_CHUNK, GM_CHUNK), layer),
                  _layer_param((GM_CHUNK, GM_GROUPS), layer)],
        out_specs=pl.BlockSpec((tb, D_MIX), lambda i: (i, 0)),
        out_shape=jax.ShapeDtypeStruct((rows, D_MIX), BF16),
        compiler_params=_cparams("parallel"),
        name="gmlp_prompt",
    )(proj, proj, nw_row, ws, b_t)


KMEAN_ROWS = 128


def _rope_tables(pos):
    half = ROT_DIM // 2
    inv_freq = ROPE_THETA ** (-jnp.arange(half, dtype=F32) * (2.0 / ROT_DIM))
    ang = pos.astype(F32)[:, None] * inv_freq[None, :]
    cos, sin = jnp.cos(ang), jnp.sin(ang)
    rest = jnp.ones((pos.shape[0], HEAD_DIM - ROT_DIM), F32)
    return (jnp.concatenate([cos, cos, rest], axis=1),
            jnp.concatenate([-sin, sin, 0.0 * rest], axis=1))


def _rope(x, cos, sin):
    lane = lax.broadcasted_iota(jnp.int32, (x.shape[0], HEAD_DIM), 1)
    first_half = lane < ROT_DIM // 2
    outs = []
    for h in range(H_MIX):
        xh = x[:, h * HEAD_DIM:(h + 1) * HEAD_DIM]
        rot = jnp.where(first_half, pltpu.roll(xh, HEAD_DIM - ROT_DIM // 2, 1), pltpu.roll(xh, ROT_DIM // 2, 1))
        outs.append(xh * cos + rot * sin)
    return jnp.concatenate(outs, axis=1)


def _rope_kernel(q_ref, k_ref, v_ref, cos_ref, sin_ref, qo_ref, ko_ref, vo_ref, kb_ref, vb_ref, km_ref):
    t = pl.program_id(1)
    cos = cos_ref[...]
    sin = sin_ref[...]
    qo_ref[...] = _rope(q_ref[...], cos, sin)
    kr = _rope(k_ref[...], cos, sin)
    v = v_ref[...]
    for h in range(H_MIX):
        ko_ref[pl.ds(h, MOBA_BLOCK, stride=H_MIX), :] = kr[:, h * HEAD_DIM:(h + 1) * HEAD_DIM]
        vo_ref[pl.ds(h, MOBA_BLOCK, stride=H_MIX), :] = v[:, h * HEAD_DIM:(h + 1) * HEAD_DIM]
    kb_ref[...] = kr.astype(BF16)
    vb_ref[...] = v.astype(BF16)

    @pl.when(t == 0)
    def _():
        km_ref[...] = jnp.zeros_like(km_ref)

    km_ref[0, pl.ds(t, 1), :] = jnp.mean(kr, axis=0, keepdims=True)


def _rope_prompt(proj, cos_tab, sin_tab, *, bsz, seqlen):
    nb = seqlen // MOBA_BLOCK
    col = lambda c: pl.BlockSpec((MOBA_BLOCK, D_MIX), lambda b, t: (b * nb + t, c // D_MIX))
    row = pl.BlockSpec((MOBA_BLOCK, D_MIX), lambda b, t: (b * nb + t, 0))
    head_rows = pl.BlockSpec((MOBA_BLOCK * H_MIX, HEAD_DIM), lambda b, t: (b * nb + t, 0))
    tab = pl.BlockSpec((MOBA_BLOCK, HEAD_DIM), lambda b, t: (t, 0))
    return pl.pallas_call(
        _rope_kernel,
        grid=(bsz, nb),
        in_specs=[col(COL_C_Q), col(COL_C_K), col(COL_C_V), tab, tab],
        out_specs=[row, head_rows, head_rows, row, row,
                   pl.BlockSpec((1, KMEAN_ROWS, D_MIX), lambda b, t: (b, 0, 0))],
        out_shape=[jax.ShapeDtypeStruct((bsz * seqlen, D_MIX), F32),
                   jax.ShapeDtypeStruct((bsz * seqlen * H_MIX, HEAD_DIM), F32),
                   jax.ShapeDtypeStruct((bsz * seqlen * H_MIX, HEAD_DIM), F32),
                   jax.ShapeDtypeStruct((bsz * seqlen, D_MIX), BF16),
                   jax.ShapeDtypeStruct((bsz * seqlen, D_MIX), BF16),
                   jax.ShapeDtypeStruct((bsz, KMEAN_ROWS, D_MIX), F32)],
        compiler_params=_cparams("parallel", "arbitrary"),
        name="rope_prompt",
    )(proj, proj, proj, cos_tab, sin_tab)


def _topk_block_rows(gate_t, n_valid):
    row = lax.broadcasted_iota(jnp.int32, gate_t.shape, 0)
    neg_inf = float("-inf")
    g = jnp.where(row < n_valid, gate_t, neg_inf)
    sel = jnp.zeros(gate_t.shape, F32)
    for _ in range(MOBA_TOPK):
        mx = jnp.max(g, axis=0, keepdims=True)
        first = jnp.min(jnp.where(g == mx, row, KMEAN_ROWS), axis=0, keepdims=True)
        pick = (row == first) & (mx > neg_inf)
        sel = jnp.where(pick, 1.0, sel)
        g = jnp.where(pick, neg_inf, g)
    return sel


def _moba_prompt_kernel(nb_pad, q_ref, k_ref, v_ref, km_ref, o_ref):
    qt = pl.program_id(1)
    blk = MOBA_BLOCK
    span = 2 * blk
    tri, _ = _tri_masks(blk)
    own0 = pl.multiple_of(qt * blk, blk)
    heads = [slice(h * HEAD_DIM, (h + 1) * HEAD_DIM) for h in range(H_MIX)]

    gates_t = [_dot_nt_hi(km_ref[0][0:nb_pad, hs], q_ref[:, hs]) for hs in heads]
    qhs = [(q_ref[:, hs] * (HEAD_DIM ** -0.5)).astype(BF16) for hs in heads]
    own_s = [jnp.where(tri, _dot_nt(qhs[h], k_ref[pl.ds(own0, blk), hs]), NEG_BIG) for h, hs in enumerate(heads)]
    own_m = [jnp.max(s, axis=1, keepdims=True) for s in own_s]
    own_p = [jnp.exp(s - m) for s, m in zip(own_s, own_m)]
    own_pv = [_dot(own_p[h], v_ref[pl.ds(own0, blk), hs]) for h, hs in enumerate(heads)]
    q_augs, carry = [], []
    for h in range(H_MIX):
        sel_t = _topk_block_rows(gates_t[h], qt)
        unpicked_t = jnp.concatenate([1.0 - sel_t, jnp.ones((KMEAN_ROWS - nb_pad, blk), F32)], axis=0)
        q_augs.append(jnp.concatenate([qhs[h], unpicked_t.T.astype(BF16)], axis=1))
        carry += [own_m[h], jnp.sum(own_p[h], axis=1, keepdims=True), own_pv[h]]

    def body(j, carry):
        r0 = pl.multiple_of(j * span, span)
        key_block = 2 * j + lax.broadcasted_iota(jnp.int32, (span, KMEAN_ROWS), 0) // blk
        block_bias = jnp.where(lax.broadcasted_iota(jnp.int32, (span, KMEAN_ROWS), 1) == key_block,
                               NEG_BIG, 0.0).astype(BF16)
        ss = [lax.dot_general(q_augs[h], jnp.concatenate([k_ref[pl.ds(r0, span), hs], block_bias], axis=1),
                              (((1,), (1,)), ((), ())), preferred_element_type=F32)
              for h, hs in enumerate(heads)]
        m_news = [jnp.maximum(carry[3 * h], jnp.max(ss[h], axis=1, keepdims=True)) for h in range(H_MIX)]
        ps = [jnp.exp(ss[h] - m_news[h]) for h in range(H_MIX)]
        pvs = [_dot(ps[h], v_ref[pl.ds(r0, span), hs]) for h, hs in enumerate(heads)]
        out = []
        for h in range(H_MIX):
            alpha = jnp.exp(carry[3 * h] - m_news[h])
            out += [m_news[h], alpha * carry[3 * h + 1] + jnp.sum(ps[h], axis=1, keepdims=True),
                    alpha * carry[3 * h + 2] + pvs[h]]
        return tuple(out)

    carry = lax.fori_loop(0, (qt + 1) // 2, body, tuple(carry))
    for h, hs in enumerate(heads):
        o_ref[:, hs] = (carry[3 * h + 2] / carry[3 * h + 1]).astype(o_ref.dtype)


def _moba_prompt(q_rope, k_bf, v_bf, kmean, *, bsz, seqlen):
    nb = seqlen // MOBA_BLOCK
    nb_pad = -(-nb // SUBLANES) * SUBLANES
    assert nb_pad <= KMEAN_ROWS
    return pl.pallas_call(
        functools.partial(_moba_prompt_kernel, nb_pad),
        grid=(bsz, nb),
        in_specs=[pl.BlockSpec((MOBA_BLOCK, D_MIX), lambda b, t: (b * nb + t, 0)),
                  pl.BlockSpec((seqlen, D_MIX), lambda b, t: (b, 0)),
                  pl.BlockSpec((seqlen, D_MIX), lambda b, t: (b, 0)),
                  pl.BlockSpec((1, KMEAN_ROWS, D_MIX), lambda b, t: (b, 0, 0))],
        out_specs=pl.BlockSpec((MOBA_BLOCK, D_MIX), lambda b, t: (b * nb + t, 0)),
        out_shape=jax.ShapeDtypeStruct((bsz * seqlen, D_MIX), BF16),
        compiler_params=_cparams("parallel", "arbitrary"),
        name="moba_prompt",
    )(q_rope, k_bf, v_bf, kmean)


SAMPLE_GROUP = SUBLANES


def _columns(rows):
    pad = jnp.zeros((HEAD_DIM - SAMPLE_GROUP, HEAD_DIM), F32)
    return jnp.concatenate([rows, pad], axis=0).T


def _sample_state_kernel(p_ref, g_ref, conv_ref, s_ref, c_ref, n_ref, m_ref,
                         cw_ref, alog_ref, dtb_ref, nw_ref, gbi_ref, gbf_ref,
                         gmnw_ref, gmw0_ref, gmb0_ref, cos_ref, sin_ref,
                         oa_ref, ob_ref, od_ref, convo_ref, so_ref, co_ref, no_ref, mo_ref,
                         qr_ref, kr_ref, gv_ref):
    bg = SAMPLE_GROUP
    gates = g_ref[...]

    cw = cw_ref[...]
    xa = p_ref[:, COL_A_Q:COL_A_Q + 3 * D_MIX]
    y = cw[CONV_W - 1:CONV_W] * xa
    for j in range(CONV_W - 1):
        y = y + cw[j:j + 1] * conv_ref[j]
    y = _silu(y)
    for j in range(CONV_W - 2):
        convo_ref[j] = conv_ref[j + 1]
    convo_ref[CONV_W - 2] = xa
    beta_all = _sigmoid(gates)
    eg_all = jnp.exp(-jnp.exp(alog_ref[...]) * _softplus(gates + dtb_ref[...]))
    nw = nw_ref[...]
    for h in range(H_MIX):
        hs = slice(h * HEAD_DIM, (h + 1) * HEAD_DIM)
        q = y[:, COL_A_Q + h * HEAD_DIM:COL_A_Q + (h + 1) * HEAD_DIM]
        k = y[:, COL_A_K + h * HEAD_DIM:COL_A_K + (h + 1) * HEAD_DIM]
        v = y[:, COL_A_V + h * HEAD_DIM:COL_A_V + (h + 1) * HEAD_DIM]
        z = p_ref[:, COL_A_Z + h * HEAD_DIM:COL_A_Z + (h + 1) * HEAD_DIM]
        q = q * lax.rsqrt(jnp.sum(q * q, axis=-1, keepdims=True) + L2_EPS) * (HEAD_DIM ** -0.5)
        k = k * lax.rsqrt(jnp.sum(k * k, axis=-1, keepdims=True) + L2_EPS)
        beta = beta_all[:, GATE_A_BETA + h:GATE_A_BETA + h + 1]
        eg = eg_all[:, GATE_A_DEC + h:GATE_A_DEC + h + 1]
        qk = jnp.sum(q * k, axis=-1, keepdims=True)
        kt = _columns(k)
        qt = _columns(q)
        for i in range(bg):
            s = s_ref[i, h]
            kcol = kt[:, i:i + 1]
            e_i = eg[i:i + 1, :]
            ks = jnp.sum(kcol * s, axis=0, keepdims=True)
            qs = jnp.sum(qt[:, i:i + 1] * s, axis=0, keepdims=True)
            v_new = beta[i:i + 1, :] * (v[i:i + 1, :] - e_i * ks)
            oa_ref[i:i + 1, hs] = e_i * qs + qk[i:i + 1, :] * v_new
            so_ref[i, h] = e_i * s + kcol * v_new
        o = oa_ref[:, hs]
        o = o * lax.rsqrt(jnp.mean(o * o, axis=-1, keepdims=True) + LN_EPS) * nw
        oa_ref[:, hs] = o * _silu(z)

    li_all = gates + gbi_ref[...]
    lf_all = _log_sigmoid(gates + gbf_ref[...])
    for h in range(H_MIX):
        hs = slice(h * HEAD_DIM, (h + 1) * HEAD_DIM)
        q = p_ref[:, COL_B_Q + h * HEAD_DIM:COL_B_Q + (h + 1) * HEAD_DIM]
        k = p_ref[:, COL_B_K + h * HEAD_DIM:COL_B_K + (h + 1) * HEAD_DIM] * (HEAD_DIM ** -0.5)
        v = p_ref[:, COL_B_V + h * HEAD_DIM:COL_B_V + (h + 1) * HEAD_DIM]
        og = p_ref[:, COL_B_O + h * HEAD_DIM:COL_B_O + (h + 1) * HEAD_DIM]
        li = li_all[:, GATE_B_I + h:GATE_B_I + h + 1]
        lf = lf_all[:, GATE_B_F + h:GATE_B_F + h + 1]
        m_prev = m_ref[:, h:h + 1]
        n_prev = n_ref[:, hs]
        m_new = jnp.maximum(lf + m_prev, li)
        w_in = jnp.exp(li - m_new)
        dec = jnp.exp(lf + m_prev - m_new)
        sc = jnp.sum(q * k, axis=-1, keepdims=True) * w_in
        den = sc + dec * jnp.sum(q * n_prev, axis=-1, keepdims=True)
        denom = jnp.maximum(jnp.abs(den), jnp.exp(-m_new))
        no_ref[:, hs] = dec * n_prev + w_in * k
        mo_ref[:, h:h + 1] = m_new
        kt = _columns(k)
        qt = _columns(q)
        wv = w_in * v
        for i in range(bg):
            c = c_ref[i, h]
            d_i = dec[i:i + 1, :]
            qc = jnp.sum(qt[:, i:i + 1] * c, axis=0, keepdims=True)
            ob_ref[i:i + 1, hs] = sc[i:i + 1, :] * v[i:i + 1, :] + d_i * qc
            co_ref[i, h] = d_i * c + kt[:, i:i + 1] * wv[i:i + 1, :]
        ob_ref[:, hs] = _sigmoid(og) * (ob_ref[:, hs] / denom)

    gu = _gelu_tanh(p_ref[:, COL_D_U:COL_D_U + D_MIX])
    gv = _gmlp_gv(p_ref[:, COL_D_V:COL_D_V + D_MIX], gmnw_ref[...])
    gv_ref[...] = gv
    od_ref[...] = gu * (gmw0_ref[...] * gv + gmb0_ref[...])

    cos = cos_ref[...]
    sin = sin_ref[...]
    qr_ref[...] = _rope(p_ref[:, COL_C_Q:COL_C_Q + D_MIX], cos, sin)
    kr_ref[...] = _rope(p_ref[:, COL_C_K:COL_C_K + D_MIX], cos, sin)


def _sample_state(proj, gates, conv_t, s0, c0, n0, m0, conv_w, alog_row, dtb_row, nw_row, gbi_row, gbf_row,
                  gm_nw_row, gm_w0_row, gm_b0_row, cos_row, sin_row, *, layer):
    bg = SAMPLE_GROUP
    nrows = proj.shape[0]
    full = lambda shape: pl.BlockSpec(shape, lambda i: (0,) * len(shape))
    lp = lambda shape: _layer_param(shape, layer)
    row_blk = lambda w: pl.BlockSpec((bg, w), lambda i: (i, 0))
    mat_blk = pl.BlockSpec((bg, H_MIX, HEAD_DIM, HEAD_DIM), lambda i: (i, 0, 0, 0))
    mat_in = pl.BlockSpec((None, bg, H_MIX, HEAD_DIM, HEAD_DIM), lambda i: (layer, i, 0, 0, 0))
    conv_blk = pl.BlockSpec((CONV_W - 1, bg, 3 * D_MIX), lambda i: (0, i, 0))
    conv_in = pl.BlockSpec((None, CONV_W - 1, bg, 3 * D_MIX), lambda i: (layer, 0, i, 0))
    row_in = lambda w: pl.BlockSpec((None, bg, w), lambda i: (layer, i, 0))
    f = lambda shape: jax.ShapeDtypeStruct(shape, F32)
    return pl.pallas_call(
        _sample_state_kernel,
        grid=(nrows // bg,),
        in_specs=[row_blk(N_MAIN), row_blk(LANES),
                  conv_in, mat_in, mat_in, row_in(D_MIX), row_in(H_MIX),
                  lp((CONV_W, 3 * D_MIX)), lp((1, LANES)), lp((1, LANES)), lp((1, HEAD_DIM)),
                  lp((1, LANES)), lp((1, LANES)),
                  lp((1, D_MIX)), lp((1, D_MIX)), lp((1, D_MIX)),
                  full((1, HEAD_DIM)), full((1, HEAD_DIM))],
        out_specs=[row_blk(D_MIX), row_blk(D_MIX), row_blk(D_MIX), conv_blk, mat_blk, mat_blk,
                   row_blk(D_MIX), row_blk(H_MIX), row_blk(D_MIX), row_blk(D_MIX), row_blk(D_MIX)],
        out_shape=[f((nrows, D_MIX)), f((nrows, D_MIX)), f((nrows, D_MIX)),
                   f((CONV_W - 1, nrows, 3 * D_MIX)),
                   f((nrows, H_MIX, HEAD_DIM, HEAD_DIM)), f((nrows, H_MIX, HEAD_DIM, HEAD_DIM)),
                   f((nrows, D_MIX)), f((nrows, H_MIX)),
                   f((nrows, D_MIX)), f((nrows, D_MIX)), f((nrows, D_MIX))],
        compiler_params=_cparams("parallel"),
        name="sample_state",
    )(proj, gates, conv_t, s0, c0, n0, m0, conv_w, alog_row, dtb_row, nw_row, gbi_row, gbf_row,
      gm_nw_row, gm_w0_row, gm_b0_row, cos_row, sin_row)


def _head_rows(row):
    sub = lax.broadcasted_iota(jnp.int32, (SUBLANES, HEAD_DIM), 0)
    out = jnp.zeros((SUBLANES, HEAD_DIM), F32)
    for h in range(H_MIX):
        out = jnp.where(sub == h, jnp.broadcast_to(row[:, h * HEAD_DIM:(h + 1) * HEAD_DIM], (SUBLANES, HEAD_DIM)), out)
    return out


def _moba_decode_row(n_pages, page_rows, q_row, kn_row, vn_row, k_refs, v_refs):
    page_size = page_rows // H_MIX
    pages_per_block = MOBA_BLOCK // page_size
    n_blocks = n_pages // pages_per_block
    scale = HEAD_DIM ** -0.5

    q8 = _head_rows(q_row)
    q8_bf = q8.astype(BF16)
    sub = lax.broadcasted_iota(jnp.int32, (SUBLANES, page_rows), 0)
    col = lax.broadcasted_iota(jnp.int32, (SUBLANES, page_rows), 1)
    head_mask = (col % H_MIX) == sub
    sub_d = lax.broadcasted_iota(jnp.int32, (SUBLANES, HEAD_DIM), 0)

    gates = []
    for j in range(n_blocks):
        fold = jnp.zeros((SUBLANES, HEAD_DIM), F32)
        for p in range(j * pages_per_block, (j + 1) * pages_per_block):
            fold = fold + jnp.sum(k_refs[p][0, 0].reshape(page_rows // SUBLANES, SUBLANES, HEAD_DIM), axis=0)
        ksum = fold
        for g in range(1, SUBLANES // H_MIX):
            ksum = ksum + pltpu.roll(fold, SUBLANES - g * H_MIX, 0)
        ksum = jnp.where(sub_d < H_MIX, ksum, 0.0)
        gates.append(jnp.sum(q8 * ksum, axis=1, keepdims=True) * (1.0 / MOBA_BLOCK))
    sels = []
    for n in range(n_blocks):
        rank = jnp.zeros((SUBLANES, 1), F32)
        for m in range(n_blocks):
            if m == n:
                continue
            ahead = gates[m] > gates[n]
            if m < n:
                ahead = ahead | (gates[m] == gates[n])
            rank = rank + jnp.where(ahead, 1.0, 0.0)
        sels.append(rank < float(MOBA_TOPK))

    keeps = [head_mask & sels[p // pages_per_block] for p in range(n_pages)]
    scores = [jnp.where(keeps[p],
                        lax.dot_general(q8_bf, k_refs[p][0, 0].astype(BF16), (((1,), (1,)), ((), ())),
                                        preferred_element_type=F32) * scale, NEG_BIG)
              for p in range(n_pages)]
    s_own = jnp.sum(q8 * _head_rows(kn_row), axis=1, keepdims=True) * scale
    m_all = s_own
    for s in scores:
        m_all = jnp.maximum(m_all, jnp.max(s, axis=1, keepdims=True))
    probs = [jnp.where(keeps[p], jnp.exp(scores[p] - m_all), 0.0) for p in range(n_pages)]
    e_own = jnp.exp(s_own - m_all)
    l_all = e_own
    acc = e_own * _head_rows(vn_row)
    for p in range(n_pages):
        l_all = l_all + jnp.sum(probs[p], axis=1, keepdims=True)
        acc = acc + _dot(probs[p], v_refs[p][0, 0])
    return acc / l_all


def _moba_decode_kernel(rows_per_step, n_pages, page_rows, pt_ref, q_ref, kn_ref, vn_ref, *refs):
    del pt_ref
    o_ref = refs[2 * rows_per_step * n_pages]
    for r in range(rows_per_step):
        k_refs = refs[r * n_pages:(r + 1) * n_pages]
        v_refs = refs[(rows_per_step + r) * n_pages:(rows_per_step + r + 1) * n_pages]
        out = _moba_decode_row(n_pages, page_rows, q_ref[r], kn_ref[r], vn_ref[r], k_refs, v_refs)
        for h in range(H_MIX):
            o_ref[r, :, h * HEAD_DIM:(h + 1) * HEAD_DIM] = out[h:h + 1, :]


DECODE_ROWS_PER_STEP = 2


def _moba_decode(page_table, q_rope, k_new, v_new, cache_k, cache_v, *, layer):
    bsz, n_pages = page_table.shape
    page_rows = cache_k.shape[2]
    page_size = page_rows // H_MIX
    rps = DECODE_ROWS_PER_STEP if bsz % DECODE_ROWS_PER_STEP == 0 else 1
    assert MOBA_BLOCK % page_size == 0 and (n_pages * page_size) % MOBA_BLOCK == 0
    assert (n_pages * page_size) // MOBA_BLOCK >= MOBA_TOPK and SUBLANES % H_MIX == 0
    row = pl.BlockSpec((rps, 1, D_MIX), lambda b, pt: (b, 0, 0))
    page_specs = [pl.BlockSpec((1, 1, page_rows, HEAD_DIM),
                               lambda b, pt, r=r, p=p: (layer, pt[b * rps + r, p], 0, 0))
                  for r in range(rps) for p in range(n_pages)]
    return pl.pallas_call(
        functools.partial(_moba_decode_kernel, rps, n_pages, page_rows),
        grid_spec=pltpu.PrefetchScalarGridSpec(
            num_scalar_prefetch=1,
            grid=(bsz // rps,),
            in_specs=[row, row, row] + page_specs + page_specs,
            out_specs=row),
        out_shape=jax.ShapeDtypeStruct((bsz, 1, D_MIX), F32),
        compiler_params=_cparams("parallel"),
        name="moba_decode",
    )(page_table, q_rope, k_new, v_new, *([cache_k] * (rps * n_pages)), *([cache_v] * (rps * n_pages)))


def _largest_divisor(n, candidates):
    for c in candidates:
        if n % c == 0:
            return c
    raise ValueError(f"no tile in {candidates} divides {n}")


def _tiles(rows, seqlen):
    return dict(
        proj_tm=_largest_divisor(rows, (1024, 512, 256, 128)),
        proj_tn=512 if rows > 512 else N_MAIN // 2,
        out_tm=_largest_divisor(rows, (512, 256, 128)),
        ffn_tm=_largest_divisor(rows, (512, 256, 128)),
        ffn_tf=1024,
        scan_tb=_largest_divisor(seqlen, (256, 128, 64, 32, 16, 8)),
        gmlp_tb=_largest_divisor(seqlen, (512, 256, 128)),
    )


def _lane_rows(vals, offset):
    out = jnp.zeros((vals.shape[0], 1, LANES), F32)
    return out.at[:, 0, offset:offset + vals.shape[1]].set(vals.astype(F32))


def kernel(x_prompt, x_sample, state_gdn_conv, state_gdn_s, state_mlstm_c, state_mlstm_n, state_mlstm_m,
           cache_k, cache_v, page_table, w_in, gdn_conv_w, gdn_a_log, gdn_dt_bias, gdn_norm_w, mlstm_gate_b,
           gmlp_norm_w, gmlp_ws, gmlp_b, w_out, ln1_w, ln1_b, w_up, w_down, ln2_w, ln2_b):
    bsz, seq, d_model = x_prompt.shape
    dec_b, dec_s, _ = x_sample.shape
    depth = w_in.shape[0]
    assert dec_s == 1 and d_model == N_MIXERS * D_MIX and seq >= CONV_W - 1
    assert seq % MOBA_BLOCK == 0 and seq // MOBA_BLOCK <= KMEAN_ROWS and seq % GM_CHUNK == 0
    assert w_in.shape[2] == N_MAIN + 4 * H_MIX and cache_k.shape[3] == H_MIX
    alpha = (2.0 * depth) ** 0.25
    mp = bsz * seq
    past_len = page_table.shape[1] * cache_k.shape[2]
    n_pool, page_size = cache_k.shape[1], cache_k.shape[2]
    tp = _tiles(mp, seq)
    ts = _tiles(dec_b, seq)
    gw = D_MIX // GM_GROUPS

    cos_p, sin_p = _rope_tables(jnp.arange(seq, dtype=jnp.int32))
    cos_s, sin_s = _rope_tables(past_len + jnp.arange(dec_s, dtype=jnp.int32))
    ck = cache_k.reshape(depth, n_pool, page_size * H_MIX, HEAD_DIM)
    cv = cache_v.reshape(depth, n_pool, page_size * H_MIX, HEAD_DIM)

    w_main, w_gate = _pack_w_in(jnp.transpose(w_in.astype(F32), (0, 2, 1)), tc=_largest_divisor(d_model, (256, 128)))
    w_out_bf = w_out.astype(BF16)
    w_up_bf = w_up.astype(BF16)
    w_down_bf = w_down.astype(BF16)
    gdn_s_in = state_gdn_s.astype(F32)
    ml_c_in = state_mlstm_c.astype(F32)

    alog_row = _lane_rows(gdn_a_log, GATE_A_DEC)
    dtb_row = _lane_rows(gdn_dt_bias, GATE_A_DEC)
    nw_row = gdn_norm_w.reshape(depth, 1, HEAD_DIM).astype(F32)
    gbi_row = _lane_rows(mlstm_gate_b[:, :H_MIX], GATE_B_I)
    gbf_row = _lane_rows(mlstm_gate_b[:, H_MIX:], GATE_B_F)
    gm_nw_row = gmlp_norm_w.reshape(depth, 1, D_MIX).astype(F32)
    gm_ws = gmlp_ws.astype(F32)
    gm_bt = jnp.transpose(gmlp_b.astype(F32), (0, 2, 1))
    gm_w0_row = jnp.repeat(gmlp_ws[:, :, 0, 0].astype(F32), gw, axis=1).reshape(depth, 1, D_MIX)
    gm_b0_row = jnp.repeat(gmlp_b[:, :, 0].astype(F32), gw, axis=1).reshape(depth, 1, D_MIX)
    conv_w = gdn_conv_w.astype(F32)
    ln1 = (ln1_w.reshape(depth, 1, d_model).astype(F32), ln1_b.reshape(depth, 1, d_model).astype(F32))
    ln2 = (ln2_w.reshape(depth, 1, d_model).astype(F32), ln2_b.reshape(depth, 1, d_model).astype(F32))
    conv_in = jnp.transpose(state_gdn_conv.astype(F32), (0, 2, 1, 3))
    ml_n_in = state_mlstm_n.astype(F32).reshape(depth, dec_b, D_MIX)
    ml_m_in = state_mlstm_m.astype(F32)

    xp = x_prompt.reshape(mp, d_model)
    xs = x_sample.reshape(dec_b, d_model)
    p_st, s_st = [], []
    for l in range(depth):
        proj, gates = _proj(xp, w_main, w_gate, layer=l, tm=tp["proj_tm"], tn=tp["proj_tn"])
        oa, gdn_s_p, ob, ml_c_p, ml_n_p, ml_m_p = _scan_prompt(
            proj, gates, conv_w, alog_row, dtb_row, nw_row, gbi_row, gbf_row,
            layer=l, bsz=bsz, seqlen=seq, tb=tp["scan_tb"])
        q_rope, k_rows, v_rows, k_bf, v_bf, kmean = _rope_prompt(proj, cos_p, sin_p, bsz=bsz, seqlen=seq)
        oc = _moba_prompt(q_rope, k_bf, v_bf, kmean, bsz=bsz, seqlen=seq)
        od = _gmlp_prompt(proj, gm_nw_row, gm_ws, gm_bt, layer=l, rows=mp, tb=tp["gmlp_tb"])
        x1 = _outproj_ln((oa, ob, oc, od), w_out_bf, xp, *ln1, layer=l, alpha=alpha, tm=tp["out_tm"])
        xp = _ffn_ln(x1, w_up_bf, w_down_bf, *ln2, layer=l, alpha=alpha, tm=tp["ffn_tm"], tf=tp["ffn_tf"])
        proj_p = proj.reshape(bsz, seq, N_MAIN)
        p_st.append((proj_p[:, seq - (CONV_W - 1):, COL_A_Q:COL_A_Q + 3 * D_MIX],
                     gdn_s_p, ml_c_p, ml_n_p[:, :H_MIX], ml_m_p[:, :H_MIX, 0],
                     k_rows.reshape(bsz, seq, H_MIX, HEAD_DIM), v_rows.reshape(bsz, seq, H_MIX, HEAD_DIM)))

        proj, gates = _proj(xs, w_main, w_gate, layer=l, tm=ts["proj_tm"], tn=ts["proj_tn"])
        (oa, ob, od, conv_s, gdn_s_s, ml_c_s, ml_n_s, ml_m_s, q_s, k_s, gv_s) = _sample_state(
            proj, gates, conv_in, gdn_s_in, ml_c_in, ml_n_in, ml_m_in,
            conv_w, alog_row, dtb_row, nw_row, gbi_row, gbf_row, gm_nw_row, gm_w0_row, gm_b0_row,
            cos_s, sin_s, layer=l)
        v_s = proj[:, COL_C_V:COL_C_V + D_MIX]
        oc = _moba_decode(page_table, q_s.reshape(dec_b, 1, D_MIX), k_s.reshape(dec_b, 1, D_MIX),
                          v_s.reshape(dec_b, 1, D_MIX), ck, cv, layer=l).reshape(dec_b, D_MIX)
        x1 = _outproj_ln((oa, ob, oc, od), w_out_bf, xs, *ln1, layer=l, alpha=alpha, tm=ts["out_tm"])
        xs = _ffn_ln(x1, w_up_bf, w_down_bf, *ln2, layer=l, alpha=alpha, tm=ts["ffn_tm"], tf=ts["ffn_tf"])
        s_st.append((jnp.transpose(conv_s, (1, 0, 2)), gdn_s_s, ml_c_s,
                     ml_n_s.reshape(dec_b, H_MIX, HEAD_DIM), ml_m_s,
                     k_s.reshape(dec_b, dec_s, H_MIX, HEAD_DIM), v_s.reshape(dec_b, dec_s, H_MIX, HEAD_DIM),
                     gv_s.reshape(dec_b, dec_s, D_MIX)))

    def stk(sts, i):
        return jnp.stack([s[i] for s in sts], axis=0)

    dt = x_prompt.dtype
    yp = xp.reshape(bsz, seq, d_model).astype(dt)
    ys = xs.reshape(dec_b, dec_s, d_model).astype(dt)
    return (yp, ys) + tuple(stk(p_st, i).astype(dt) for i in range(7)) + tuple(stk(s_st, i).astype(dt) for i in range(8))
```
